```python
import math
import jax, jax.numpy as jnp
from jax import lax
import numpy as np

D_MODEL = 2048
BATCH = 2
SEQ = 8192
DEPTH = 1

EPS = 1e-6
F_GROUPS = 4
F_GROUP_DIM = 256
F_WIDTH = F_GROUPS * F_GROUP_DIM
A_HEADS = 8
A_HEAD_DIM = D_MODEL // (2 * A_HEADS)
A_V_DIM = 2 * A_HEAD_DIM
A_QK_WIDTH = A_HEADS * 2 * A_HEAD_DIM
A_V_WIDTH = A_HEADS * A_V_DIM
Q_BLOCK = 128
IN_WIDTH = F_WIDTH + 2 * A_QK_WIDTH + A_V_WIDTH + 2 * D_MODEL
P_HEADS = 8
N_KEYS = 128
N_EXPERTS = N_KEYS * N_KEYS
P_KEY_DIM = 256
P_HALF = P_KEY_DIM // 2
P_TOPK = 16
P_TOKEN_BLOCK = 128

kernel_name = "hybrid_fnet_diffattn_peer_encoder"


def rmsnorm(x, g):
    xf = x.astype(jnp.float32)
    y = xf * lax.rsqrt(jnp.mean(xf * xf, axis=-1, keepdims=True) + EPS)
    return (y * g.astype(jnp.float32)).astype(x.dtype)


def alibi_slopes():
    return 2.0 ** (-8.0 * jnp.arange(1, A_HEADS + 1, dtype=jnp.float32) / A_HEADS)


def lambda_init_for(layer_idx):
    return 0.8 - 0.6 * math.exp(-0.3 * layer_idx)


def fourier_mix(z):
    B, S = z.shape[:2]
    zg = z.astype(jnp.float32).reshape(B, S, F_GROUPS, F_GROUP_DIM).transpose(0, 2, 1, 3)
    y = jnp.fft.fft2(zg, norm="ortho").real
    return y.transpose(0, 2, 1, 3).reshape(B, S, F_WIDTH).astype(z.dtype)


def diff_attention(q, k, v, lam, slopes):
    B, S = q.shape[:2]
    nb = S // Q_BLOCK
    qb = q.reshape(B, nb, Q_BLOCK, A_HEADS, 2, A_HEAD_DIM).swapaxes(0, 1)
    starts = jnp.arange(nb, dtype=jnp.int32) * Q_BLOCK
    kpos = jnp.arange(S, dtype=jnp.int32)
    scale = A_HEAD_DIM ** -0.5

    def block(args):
        qi, s0 = args
        qpos = s0 + jnp.arange(Q_BLOCK, dtype=jnp.int32)
        dist = jnp.abs(qpos[:, None] - kpos[None, :]).astype(jnp.float32)
        bias = -slopes[:, None, None] * dist
        s = jnp.einsum('bqhcd,bkhcd->bhcqk', qi, k,
                       preferred_element_type=jnp.float32) * scale + bias[None, :, None]
        p = jax.nn.softmax(s, axis=-1)
        a = p[:, :, 0] - lam * p[:, :, 1]
        return jnp.einsum('bhqk,bkhe->bqhe', a.astype(v.dtype), v)

    out = lax.map(block, (qb, starts))
    return out.swapaxes(0, 1).reshape(B, S, A_HEADS, A_V_DIM)


def peer(xn, w_query, sub_keys, expert_u, expert_v):
    B, S, D = xn.shape
    nb = (B * S) // P_TOKEN_BLOCK
    xb = xn.reshape(nb, P_TOKEN_BLOCK, D)
    t = P_TOKEN_BLOCK

    def block(xt):
        q = (xt @ w_query).reshape(t, P_HEADS, 2, P_HALF)
        s = jnp.einsum('thcd,hcnd->thcn', q, sub_keys,
                       preferred_element_type=jnp.float32)
        sv, si = lax.top_k(s, P_TOPK)
        cand = sv[:, :, 0, :, None] + sv[:, :, 1, None, :]
        cidx = si[:, :, 0, :, None] * N_KEYS + si[:, :, 1, None, :]
        top_s, top_j = lax.top_k(cand.reshape(t, P_HEADS, P_TOPK * P_TOPK), P_TOPK)
        idx = jnp.take_along_axis(cidx.reshape(t, P_HEADS, P_TOPK * P_TOPK), top_j, axis=-1)
        g = jax.nn.softmax(top_s, axis=-1)
        ue = jnp.take(expert_u, idx, axis=0)
        ve = jnp.take(expert_v, idx, axis=0)
        act = jax.nn.gelu(jnp.einsum('td,thkd->thk', xt, ue,
                                     preferred_element_type=jnp.float32), approximate=False) * g
        return jnp.einsum('thk,thkd->td', act.astype(ve.dtype), ve)

    return lax.map(block, xb).reshape(B, S, D)


def setup_inputs(seed: int = 0) -> dict:
    key = jax.random.key(seed)
    ks = jax.random.split(key, 20)
    f32 = jnp.float32

    def nrm(k, shape, scale):
        return jax.random.normal(k, shape, f32) * scale

    def gain(k, shape):
        return 1.0 + 0.02 * jax.random.normal(k, shape, f32)

    L = DEPTH
    return {
        "x": jax.random.normal(ks[0], (BATCH, SEQ, D_MODEL), f32),
        "norm1_g": gain(ks[1], (L, D_MODEL)),
        "w_in": nrm(ks[2], (L, D_MODEL, IN_WIDTH), D_MODEL ** -0.5),
        "w_fourier": nrm(ks[3], (L, F_WIDTH, D_MODEL), F_WIDTH ** -0.5),
        "w_attn": nrm(ks[4], (L, A_V_WIDTH, D_MODEL), A_V_WIDTH ** -0.5),
        "q_norm_g": gain(ks[5], (L, A_HEAD_DIM)),
        "k_norm_g": gain(ks[6], (L, A_HEAD_DIM)),
        "lambda_q1": nrm(ks[7], (L, A_HEAD_DIM), 0.1),
        "lambda_k1": nrm(ks[8], (L, A_HEAD_DIM), 0.1),
        "lambda_q2": nrm(ks[9], (L, A_HEAD_DIM), 0.1),
        "lambda_k2": nrm(ks[10], (L, A_HEAD_DIM), 0.1),
        "subln_g": gain(ks[11], (L, A_V_DIM)),
        "w_out": nrm(ks[12], (L, D_MODEL, D_MODEL), D_MODEL ** -0.5),
        "norm2_g": gain(ks[13], (L, D_MODEL)),
        "w_query": nrm(ks[14], (L, D_MODEL, P_HEADS * P_KEY_DIM), D_MODEL ** -0.5),
        "sub_keys": nrm(ks[15], (L, P_HEADS, 2, N_KEYS, P_HALF), P_HALF ** -0.5),
        "expert_u": nrm(ks[16], (L, N_EXPERTS, D_MODEL), D_MODEL ** -0.5),
        "expert_v": nrm(ks[17], (L, N_EXPERTS, D_MODEL), P_HEADS ** -0.5),
    }


def reference(x, norm1_g, w_in, w_fourier, w_attn, q_norm_g, k_norm_g, lambda_q1, lambda_k1,
              lambda_q2, lambda_k2, subln_g, w_out, norm2_g, w_query, sub_keys, expert_u, expert_v):
    B, S, _ = x.shape
    slopes = alibi_slopes()
    o1 = F_WIDTH
    o2 = o1 + A_QK_WIDTH
    o3 = o2 + A_QK_WIDTH
    o4 = o3 + A_V_WIDTH
    o5 = o4 + D_MODEL
    h = x
    for i in range(DEPTH):
        lam_init = lambda_init_for(i)
        xn = rmsnorm(h, norm1_g[i])
        proj = xn @ w_in[i]
        z_f, q, k, v, gate_f, gate_a = jnp.split(proj, [o1, o2, o3, o4, o5], axis=-1)

        y_f = fourier_mix(z_f) @ w_fourier[i]

        q = rmsnorm(q.reshape(B, S, A_HEADS, 2, A_HEAD_DIM), q_norm_g[i])
        k = rmsnorm(k.reshape(B, S, A_HEADS, 2, A_HEAD_DIM), k_norm_g[i])
        v = v.reshape(B, S, A_HEADS, A_V_DIM)
        lam = (jnp.exp(jnp.sum(lambda_q1[i].astype(jnp.float32) * lambda_k1[i].astype(jnp.float32)))
               - jnp.exp(jnp.sum(lambda_q2[i].astype(jnp.float32) * lambda_k2[i].astype(jnp.float32)))
               + lam_init)
        o = diff_attention(q, k, v, lam, slopes)
        o = rmsnorm(o, subln_g[i]) * (1.0 - lam_init)
        y_a = o.reshape(B, S, A_V_WIDTH) @ w_attn[i]

        mixed = jax.nn.sigmoid(gate_f) * y_f + jax.nn.sigmoid(gate_a) * y_a
        h = h + mixed @ w_out[i]

        hn = rmsnorm(h, norm2_g[i])
        h = h + peer(hn, w_query[i], sub_keys[i], expert_u[i], expert_v[i])
    return h
```

```python
import functools
import math

import numpy as np
import jax
import jax.numpy as jnp
from jax import lax
from jax.experimental import pallas as pl
from jax.experimental.pallas import tpu as pltpu

EPS = 1e-6
F_GROUPS = 4
A_HEADS = 8
P_HEADS = 8
P_TOPK = 16
LANES = 128
FFT_INNER = 128
VMEM_LIMIT_BYTES = 56 * 1024 * 1024

F32 = jnp.float32
BF16 = jnp.bfloat16
NEG_INF = float("-inf")


def _params(*semantics):
    return pltpu.CompilerParams(dimension_semantics=semantics, vmem_limit_bytes=VMEM_LIMIT_BYTES)


def _rmsnorm_body(x_ref, g_ref, o_ref):
    x = x_ref[...]
    ms = jnp.mean(x * x, axis=-1, keepdims=True)
    o_ref[...] = (x * lax.rsqrt(ms + EPS) * g_ref[...]).astype(o_ref.dtype)


def rmsnorm_rows(x2d, g, tm=512):
    T, D = x2d.shape
    return pl.pallas_call(
        _rmsnorm_body,
        grid=(T // tm,),
        in_specs=[pl.BlockSpec((tm, D), lambda i: (i, 0)),
                  pl.BlockSpec((1, D), lambda i: (0, 0))],
        out_specs=pl.BlockSpec((tm, D), lambda i: (i, 0)),
        out_shape=jax.ShapeDtypeStruct((T, D), BF16),
        compiler_params=_params("parallel"),
        name="rmsnorm",
    )(x2d, g.reshape(1, D).astype(F32))


def _proj_body(x_ref, w_ref, g_ref, o_ref, *, mode, group):
    acc = jnp.dot(x_ref[...], w_ref[...], preferred_element_type=F32)
    if mode == "cast":
        o_ref[...] = acc.astype(o_ref.dtype)
    elif mode == "sigmoid":
        o_ref[...] = jax.nn.sigmoid(acc).astype(o_ref.dtype)
    else:
        for s in range(acc.shape[1] // group):
            blk = acc[:, s * group:(s + 1) * group]
            ms = jnp.mean(blk * blk, axis=-1, keepdims=True)
            o_ref[:, s * group:(s + 1) * group] = (
                blk * lax.rsqrt(ms + EPS) * g_ref[:, s * group:(s + 1) * group]).astype(o_ref.dtype)


def project(x, w, col0, ncols, mode, gain=None, group=LANES, tm=1024, tn=1024):
    T, K = x.shape
    tm = min(tm, T)
    if gain is None:
        gain = jnp.ones((1, ncols), F32)
    off = col0 // tn
    return pl.pallas_call(
        functools.partial(_proj_body, mode=mode, group=group),
        grid=(T // tm, ncols // tn),
        in_specs=[pl.BlockSpec((tm, K), lambda i, j: (i, 0)),
                  pl.BlockSpec((K, tn), lambda i, j: (0, off + j)),
                  pl.BlockSpec((1, tn), lambda i, j: (0, j))],
        out_specs=pl.BlockSpec((tm, tn), lambda i, j: (i, j)),
        out_shape=jax.ShapeDtypeStruct((T, ncols), BF16),
        compiler_params=_params("parallel", "arbitrary"),
        name="proj_" + mode,
    )(x, w, gain)


def _fourier_constants(S, C):
    Na = S // FFT_INNER
    n2 = np.arange(C, dtype=np.int64)
    ang = 2.0 * np.pi * ((n2[:, None] * n2[None, :]) % C) / C
    w0 = np.concatenate([np.cos(ang), -np.sin(ang)], axis=1)
    a = np.arange(Na, dtype=np.int64)
    phi = 2.0 * np.pi * ((a[:, None] * a[None, :]) % Na) / Na
    r = np.block([[np.cos(phi), np.sin(phi)], [-np.sin(phi), np.cos(phi)]])
    k = np.arange(S, dtype=np.int64)
    b = np.arange(FFT_INNER, dtype=np.int64)
    th = 2.0 * np.pi * ((k[:, None] * b[None, :]) % S) / S
    m = np.concatenate([np.cos(th), np.sin(th)], axis=1) / math.sqrt(S * C)
    m3 = m.reshape(FFT_INNER, Na, 2 * FFT_INNER).transpose(1, 0, 2)
    return (jnp.asarray(w0, BF16), jnp.asarray(r, BF16), jnp.asarray(m3, BF16))


def _mm_body(x_ref, w_ref, o_ref):
    o_ref[...] = jnp.dot(x_ref[...], w_ref[...], preferred_element_type=F32).astype(o_ref.dtype)


def _lmm_body(w_ref, x_ref, o_ref):
    o_ref[0] = jnp.dot(w_ref[...], x_ref[0], preferred_element_type=F32).astype(o_ref.dtype)


def _twiddle_body(m_ref, a_ref, o_ref):
    for ci in range(m_ref.shape[0]):
        o_ref[0, ci] = jnp.dot(m_ref[ci], a_ref[0, ci], preferred_element_type=F32).astype(o_ref.dtype)


def fourier_mix(z2d, B, S):
    T, W = z2d.shape
    G = F_GROUPS
    C = W // G
    Na = S // FFT_INNER
    w0, r, m3 = _fourier_constants(S, C)
    tm = min(1024, T)
    pq = pl.pallas_call(
        _mm_body,
        grid=(T // tm, G),
        in_specs=[pl.BlockSpec((tm, C), lambda i, g: (i, g)),
                  pl.BlockSpec((C, 2 * C), lambda i, g: (0, 0))],
        out_specs=pl.BlockSpec((tm, 2 * C), lambda i, g: (i, g)),
        out_shape=jax.ShapeDtypeStruct((T, G * 2 * C), BF16),
        compiler_params=_params("parallel", "arbitrary"),
        name="fourier_channel",
    )(z2d, w0)
    x1 = pq.reshape(B, Na, FFT_INNER, G, 2, C).transpose(0, 3, 4, 1, 2, 5)
    x1 = x1.reshape(B * G, 2 * Na, FFT_INNER * C)
    tn = min(8192, FFT_INNER * C)
    a1 = pl.pallas_call(
        _lmm_body,
        grid=(B * G, (FFT_INNER * C) // tn),
        in_specs=[pl.BlockSpec((2 * Na, 2 * Na), lambda s, j: (0, 0)),
                  pl.BlockSpec((1, 2 * Na, tn), lambda s, j: (s, 0, j))],
        out_specs=pl.BlockSpec((1, 2 * Na, tn), lambda s, j: (s, 0, j)),
        out_shape=jax.ShapeDtypeStruct((B * G, 2 * Na, FFT_INNER * C), BF16),
        compiler_params=_params("parallel", "arbitrary"),
        name="fourier_outer",
    )(r, x1)
    a2 = a1.reshape(B * G, 2, Na, FFT_INNER, C).transpose(0, 2, 1, 3, 4)
    a2 = a2.reshape(B * G, Na, 2 * FFT_INNER, C)
    cb = min(8, Na)
    y = pl.pallas_call(
        _twiddle_body,
        grid=(B * G, Na // cb),
        in_specs=[pl.BlockSpec((cb, FFT_INNER, 2 * FFT_INNER), lambda s, j: (j, 0, 0)),
                  pl.BlockSpec((1, cb, 2 * FFT_INNER, C), lambda s, j: (s, j, 0, 0))],
        out_specs=pl.BlockSpec((1, cb, FFT_INNER, C), lambda s, j: (s, j, 0, 0)),
        out_shape=jax.ShapeDtypeStruct((B * G, Na, FFT_INNER, C), BF16),
        compiler_params=_params("parallel", "arbitrary"),
        name="fourier_inner",
    )(m3, a2)
    y = y.reshape(B, G, Na, FFT_INNER, C).transpose(0, 3, 2, 1, 4)
    return y.reshape(T, G * C)


def _attn_body(slope_ref, lamv_ref, g_ref, q_ref, k_ref, v_ref, o_ref, m_ref, l_ref, acc_ref,
               *, tq, tk, dh, lam_init):
    h = pl.program_id(1)
    qi = pl.program_id(2)
    ki = pl.program_id(3)

    @pl.when(ki == 0)
    def _():
        m_ref[...] = jnp.full(m_ref.shape, NEG_INF, F32)
        l_ref[...] = jnp.zeros(l_ref.shape, F32)
        acc_ref[...] = jnp.zeros(acc_ref.shape, F32)

    off = qi * tq - ki * tk
    row = lax.broadcasted_iota(jnp.int32, (tq, tk), 0)
    col = lax.broadcasted_iota(jnp.int32, (tq, tk), 1)
    bias = jnp.abs(row - col + off).astype(F32) * (-slope_ref[h])
    q = q_ref[...]
    k = k_ref[...]
    v = v_ref[...]
    for c in range(2):
        s = lax.dot_general(q[:, c * dh:(c + 1) * dh], k[:, c * dh:(c + 1) * dh],
                            (((1,), (1,)), ((), ())), preferred_element_type=F32) + bias
        m_prev = m_ref[c]
        m_new = jnp.maximum(m_prev, jnp.max(s, axis=-1, keepdims=True))
        alpha = jnp.exp(m_prev - m_new)
        p = jnp.exp(s - m_new)
        l_ref[c] = alpha * l_ref[c] + jnp.sum(p, axis=-1, keepdims=True)
        acc_ref[c] = alpha * acc_ref[c] + jnp.dot(p.astype(BF16), v, preferred_element_type=F32)
        m_ref[c] = m_new

    @pl.when(ki == pl.num_programs(3) - 1)
    def _():
        lv = lamv_ref[...]
        lam = (jnp.exp(jnp.sum(lv[0:1] * lv[1:2], axis=-1, keepdims=True))
               - jnp.exp(jnp.sum(lv[2:3] * lv[3:4], axis=-1, keepdims=True)) + lam_init)
        o = acc_ref[0] / l_ref[0] - lam * (acc_ref[1] / l_ref[1])
        ms = jnp.mean(o * o, axis=-1, keepdims=True)
        o_ref[...] = (o * lax.rsqrt(ms + EPS) * g_ref[...] * (1.0 - lam_init)).astype(o_ref.dtype)


def diff_attention(qk2d, v2d, lamv, subln_g, B, S, lam_init, tq=512, tk=512):
    T, W = qk2d.shape
    H = A_HEADS
    dh = W // (4 * H)
    dv = 2 * dh
    tq = min(tq, S)
    tk = min(tk, S)
    nq, nk = S // tq, S // tk
    slopes = jnp.asarray(2.0 ** (-8.0 * np.arange(1, H + 1) / H), F32)
    return pl.pallas_call(
        functools.partial(_attn_body, tq=tq, tk=tk, dh=dh, lam_init=lam_init),
        grid=(B, H, nq, nk),
        in_specs=[pl.BlockSpec(memory_space=pltpu.SMEM),
                  pl.BlockSpec((4, dh), lambda b, h, i, j: (0, 0)),
                  pl.BlockSpec((1, dv), lambda b, h, i, j: (0, 0)),
                  pl.BlockSpec((tq, dv), lambda b, h, i, j: (b * nq + i, h)),
                  pl.BlockSpec((tk, dv), lambda b, h, i, j: (b * nk + j, H + h)),
                  pl.BlockSpec((tk, dv), lambda b, h, i, j: (b * nk + j, h))],
        out_specs=pl.BlockSpec((tq, dv), lambda b, h, i, j: (b * nq + i, h)),
        out_shape=jax.ShapeDtypeStruct((T, H * dv), BF16),
        scratch_shapes=[pltpu.VMEM((2, tq, 1), F32), pltpu.VMEM((2, tq, 1), F32),
                        pltpu.VMEM((2, tq, dv), F32)],
        compiler_params=_params("parallel", "parallel", "parallel", "arbitrary"),
        name="diff_attention",
    )(slopes, lamv, subln_g.reshape(1, dv).astype(F32), qk2d, qk2d, v2d)


def _merge_body(fm_ref, on_ref, gf_ref, ga_ref, wf_ref, wa_ref, o_ref):
    yf = jnp.dot(fm_ref[...], wf_ref[...], preferred_element_type=F32)
    ya = jnp.dot(on_ref[...], wa_ref[...], preferred_element_type=F32)
    o_ref[...] = (gf_ref[...].astype(F32) * yf + ga_ref[...].astype(F32) * ya).astype(o_ref.dtype)


def gated_merge(fm, on, gates, wf, wa, tm=256):
    T, D = on.shape
    tm = min(tm, T)
    return pl.pallas_call(
        _merge_body,
        grid=(T // tm,),
        in_specs=[pl.BlockSpec((tm, fm.shape[1]), lambda i: (i, 0)),
                  pl.BlockSpec((tm, D), lambda i: (i, 0)),
                  pl.BlockSpec((tm, D), lambda i: (i, 0)),
                  pl.BlockSpec((tm, D), lambda i: (i, 1)),
                  pl.BlockSpec(wf.shape, lambda i: (0, 0)),
                  pl.BlockSpec(wa.shape, lambda i: (0, 0))],
        out_specs=pl.BlockSpec((tm, D), lambda i: (i, 0)),
        out_shape=jax.ShapeDtypeStruct((T, D), BF16),
        compiler_params=_params("parallel"),
        name="gated_merge",
    )(fm, on, gates, gates, wf, wa)


def _out_body(mx_ref, x_ref, wo_ref, g_ref, h_ref, hn_ref):
    h = x_ref[...] + jnp.dot(mx_ref[...], wo_ref[...], preferred_element_type=F32)
    h_ref[...] = h
    ms = jnp.mean(h * h, axis=-1, keepdims=True)
    hn_ref[...] = (h * lax.rsqrt(ms + EPS) * g_ref[...]).astype(hn_ref.dtype)


def out_projection(mixed, x2d, wo, g2, tm=512):
    T, D = x2d.shape
    tm = min(tm, T)
    return pl.pallas_call(
        _out_body,
        grid=(T // tm,),
        in_specs=[pl.BlockSpec((tm, D), lambda i: (i, 0)),
                  pl.BlockSpec((tm, D), lambda i: (i, 0)),
                  pl.BlockSpec((D, D), lambda i: (0, 0)),
                  pl.BlockSpec((1, D), lambda i: (0, 0))],
        out_specs=[pl.BlockSpec((tm, D), lambda i: (i, 0)),
                   pl.BlockSpec((tm, D), lambda i: (i, 0))],
        out_shape=[jax.ShapeDtypeStruct((T, D), F32), jax.ShapeDtypeStruct((T, D), BF16)],
        compiler_params=_params("parallel"),
        name="out_projection",
    )(mixed, x2d, wo, g2.reshape(1, D).astype(F32))


def _extract_top(x, dst_ref, n):
    for r in range(n):
        mx = jnp.max(x, axis=0, keepdims=True)
        dst_ref[pl.ds(r, 1), :] = mx
        x = jnp.where(x >= mx, NEG_INF, x)


def _peer_query_body(hn_ref, wq_ref, keys_ref, s1_ref, s2_ref, st_ref, sv_ref, cs_ref, *, nk):
    qp = jnp.dot(hn_ref[...], wq_ref[...], preferred_element_type=F32).astype(BF16)
    half = keys_ref.shape[-1]
    for h in range(P_HEADS):
        for c in range(2):
            lo = (2 * h + c) * half
            st = lax.dot_general(keys_ref[h, c], qp[:, lo:lo + half], (((1,), (1,)), ((), ())),
                                 preferred_element_type=F32)
            (s1_ref if c == 0 else s2_ref)[h] = st
            _extract_top(st, sv_ref.at[c], P_TOPK)
        sv1 = sv_ref[0]
        sv2 = sv_ref[1]
        parts = [sv1[0:1] + sv2]
        for a in range(1, 8):
            parts.append(sv1[a:a + 1] + sv2[0:8])
        parts.append(sv1[8:16] + sv2[0:1])
        _extract_top(jnp.concatenate(parts, axis=0), cs_ref, P_TOPK)
        cs = cs_ref[...]
        top = cs[0:1]
        z = jnp.sum(jnp.exp(cs - top), axis=0, keepdims=True)
        st_ref[0, pl.ds(h, 1), :] = cs[P_TOPK - 1:P_TOPK]
        st_ref[1, pl.ds(h, 1), :] = sv1[0:1]
        st_ref[2, pl.ds(h, 1), :] = sv2[0:1]
        st_ref[3, pl.ds(h, 1), :] = 1.0 / z


def peer_query(hn, wq, keys, tt=256):
    T, D = hn.shape
    H, _, nk, half = keys.shape
    tt = min(tt, T)
    return pl.pallas_call(
        functools.partial(_peer_query_body, nk=nk),
        grid=(T // tt,),
        in_specs=[pl.BlockSpec((tt, D), lambda i: (i, 0)),
                  pl.BlockSpec(wq.shape, lambda i: (0, 0)),
                  pl.BlockSpec(keys.shape, lambda i: (0, 0, 0, 0))],
        out_specs=[pl.BlockSpec((H, nk, tt), lambda i: (0, 0, i)),
                   pl.BlockSpec((H, nk, tt), lambda i: (0, 0, i)),
                   pl.BlockSpec((4, H, tt), lambda i: (0, 0, i))],
        out_shape=[jax.ShapeDtypeStruct((H, nk, T), F32), jax.ShapeDtypeStruct((H, nk, T), F32),
                   jax.ShapeDtypeStruct((4, H, T), F32)],
        scratch_shapes=[pltpu.VMEM((2, P_TOPK, tt), F32), pltpu.VMEM((P_TOPK, tt), F32)],
        compiler_params=_params("parallel"),
        name="peer_query",
    )(hn, wq, keys)


def _peer_dense_body(hn_ref, u_ref, v_ref, s1_ref, s2_ref, st_ref, h_ref, o_ref, e1_ref, e2_ref,
                     *, nk, rows_per_chunk):
    e = pl.program_id(1)

    @pl.when(e == 0)
    def _():
        o_ref[...] = h_ref[...]
        for h in range(P_HEADS):
            e1_ref[h] = jnp.exp(s1_ref[h] - st_ref[1, pl.ds(h, 1), :])
            e2_ref[h] = jnp.exp(s2_ref[h] - st_ref[2, pl.ds(h, 1), :]) * st_ref[3, pl.ds(h, 1), :]

    sc = lax.dot_general(u_ref[...], hn_ref[...], (((1,), (1,)), ((), ())),
                         preferred_element_type=F32)
    act = 0.5 * sc * (1.0 + lax.erf(sc * (1.0 / math.sqrt(2.0))))
    rows = []
    for ii in range(rows_per_chunk):
        i = e * rows_per_chunk + ii
        w = jnp.zeros((nk, sc.shape[1]), F32)
        for h in range(P_HEADS):
            cand = s2_ref[h] + s1_ref[h, pl.ds(i, 1), :]
            w = w + jnp.where(cand >= st_ref[0, pl.ds(h, 1), :],
                              e2_ref[h] * e1_ref[h, pl.ds(i, 1), :], 0.0)
        rows.append((act[ii * nk:(ii + 1) * nk] * w).astype(BF16))
    a = jnp.concatenate(rows, axis=0)
    o_ref[...] += lax.dot_general(a, v_ref[...], (((0,), (0,)), ((), ())), preferred_element_type=F32)


def peer_dense(hn, u, v, s1, s2, stats, h1, tt=512, ec=512):
    T, D = hn.shape
    E = u.shape[0]
    H, nk, _ = s1.shape
    tt = min(tt, T)
    return pl.pallas_call(
        functools.partial(_peer_dense_body, nk=nk, rows_per_chunk=ec // nk),
        grid=(T // tt, E // ec),
        in_specs=[pl.BlockSpec((tt, D), lambda t, e: (t, 0)),
                  pl.BlockSpec((ec, D), lambda t, e: (e, 0)),
                  pl.BlockSpec((ec, D), lambda t, e: (e, 0)),
                  pl.BlockSpec((H, nk, tt), lambda t, e: (0, 0, t)),
                  pl.BlockSpec((H, nk, tt), lambda t, e: (0, 0, t)),
                  pl.BlockSpec((4, H, tt), lambda t, e: (0, 0, t)),
                  pl.BlockSpec((tt, D), lambda t, e: (t, 0))],
        out_specs=pl.BlockSpec((tt, D), lambda t, e: (t, 0)),
        out_shape=jax.ShapeDtypeStruct((T, D), F32),
        scratch_shapes=[pltpu.VMEM((H, nk, tt), F32), pltpu.VMEM((H, nk, tt), F32)],
        compiler_params=_params("parallel", "arbitrary"),
        name="peer_dense",
    )(hn, u, v, s1, s2, stats, h1)


def kernel(x, norm1_g, w_in, w_fourier, w_attn, q_norm_g, k_norm_g, lambda_q1, lambda_k1,
           lambda_q2, lambda_k2, subln_g, w_out, norm2_g, w_query, sub_keys, expert_u, expert_v):
    B, S, D = x.shape
    T = B * S
    depth = w_in.shape[0]
    dh = D // (2 * A_HEADS)
    f_width = w_fourier.shape[1]
    qk_width = A_HEADS * 2 * dh
    v_width = w_attn.shape[1]
    o_q = f_width
    o_k = o_q + qk_width
    o_v = o_k + qk_width
    o_g = o_v + v_width
    h = x.reshape(T, D)
    for i in range(depth):
        lam_init = 0.8 - 0.6 * math.exp(-0.3 * i)
        xn = rmsnorm_rows(h, norm1_g[i])
        w_in_b = w_in[i].astype(BF16)
        qk_gain = jnp.concatenate([jnp.tile(q_norm_g[i].astype(F32) * dh ** -0.5, 2 * A_HEADS),
                                   jnp.tile(k_norm_g[i].astype(F32), 2 * A_HEADS)]).reshape(1, 2 * qk_width)
        z = project(xn, w_in_b, 0, f_width, "cast")
        qk = project(xn, w_in_b, o_q, 2 * qk_width, "headnorm", gain=qk_gain, group=dh)
        vv = project(xn, w_in_b, o_v, v_width, "cast")
        gates = project(xn, w_in_b, o_g, 2 * D, "sigmoid")

        fm = fourier_mix(z, B, S)
        lamv = jnp.stack([lambda_q1[i], lambda_k1[i], lambda_q2[i], lambda_k2[i]]).astype(F32)
        on = diff_attention(qk, vv, lamv, subln_g[i], B, S, lam_init)

        mixed = gated_merge(fm, on, gates, w_fourier[i].astype(BF16), w_attn[i].astype(BF16))
        h1, hn = out_projection(mixed, h, w_out[i].astype(BF16), norm2_g[i])

        s1, s2, stats = peer_query(hn, w_query[i].astype(BF16), sub_keys[i].astype(BF16))
        h = peer_dense(hn, expert_u[i].astype(BF16), expert_v[i].astype(BF16), s1, s2, stats, h1)
    return h.reshape(B, S, D)
```

```python
import functools
import math

import numpy as np
import jax
import jax.numpy as jnp
from jax import lax
from jax.experimental import pallas as pl
from jax.experimental.pallas import tpu as pltpu

EPS = 1e-6
F_GROUPS = 4
A_HEADS = 8
P_HEADS = 8
P_TOPK = 16
LANES = 128
FFT_INNER = 128
VMEM_LIMIT_BYTES = 56 * 1024 * 1024

F32 = jnp.float32
BF16 = jnp.bfloat16
NEG_INF = float("-inf")


def _params(*semantics):
    return pltpu.CompilerParams(dimension_semantics=semantics, vmem_limit_bytes=VMEM_LIMIT_BYTES)


def _rmsnorm_body(x_ref, g_ref, o_ref):
    x = x_ref[...]
    ms = jnp.mean(x * x, axis=-1, keepdims=True)
    o_ref[...] = (x * lax.rsqrt(ms + EPS) * g_ref[...]).astype(o_ref.dtype)


def rmsnorm_rows(x2d, g, tm=512):
    T, D = x2d.shape
    return pl.pallas_call(
        _rmsnorm_body,
        grid=(T // tm,),
        in_specs=[pl.BlockSpec((tm, D), lambda i: (i, 0)),
                  pl.BlockSpec((1, D), lambda i: (0, 0))],
        out_specs=pl.BlockSpec((tm, D), lambda i: (i, 0)),
        out_shape=jax.ShapeDtypeStruct((T, D), BF16),
        compiler_params=_params("parallel"),
        name="rmsnorm",
    )(x2d, g.reshape(1, D).astype(F32))


def _proj_body(x_ref, w_ref, g_ref, o_ref, *, mode, group):
    acc = jnp.dot(x_ref[...], w_ref[...], preferred_element_type=F32)
    if mode == "cast":
        o_ref[...] = acc.astype(o_ref.dtype)
    elif mode == "sigmoid":
        o_ref[...] = jax.nn.sigmoid(acc).astype(o_ref.dtype)
    else:
        for s in range(acc.shape[1] // group):
            blk = acc[:, s * group:(s + 1) * group]
            ms = jnp.mean(blk * blk, axis=-1, keepdims=True)
            o_ref[:, s * group:(s + 1) * group] = (
                blk * lax.rsqrt(ms + EPS) * g_ref[:, s * group:(s + 1) * group]).astype(o_ref.dtype)


def project(x, w, col0, ncols, mode, gain=None, group=LANES, tm=1024, tn=1024):
    T, K = x.shape
    tm = min(tm, T)
    if gain is None:
        gain = jnp.ones((1, ncols), F32)
    off = col0 // tn
    return pl.pallas_call(
        functools.partial(_proj_body, mode=mode, group=group),
        grid=(T // tm, ncols // tn),
        in_specs=[pl.BlockSpec((tm, K), lambda i, j: (i, 0)),
                  pl.BlockSpec((K, tn), lambda i, j: (0, off + j)),
                  pl.BlockSpec((1, tn), lambda i, j: (0, j))],
        out_specs=pl.BlockSpec((tm, tn), lambda i, j: (i, j)),
        out_shape=jax.ShapeDtypeStruct((T, ncols), BF16),
        compiler_params=_params("parallel", "arbitrary"),
        name="proj_" + mode,
    )(x, w, gain)


def _fourier_constants(S, C):
    Na = S // FFT_INNER
    n2 = np.arange(C, dtype=np.int64)
    ang = 2.0 * np.pi * ((n2[:, None] * n2[None, :]) % C) / C
    w0 = np.concatenate([np.cos(ang), -np.sin(ang)], axis=1)
    a = np.arange(Na, dtype=np.int64)
    phi = 2.0 * np.pi * ((a[:, None] * a[None, :]) % Na) / Na
    r = np.block([[np.cos(phi), np.sin(phi)], [-np.sin(phi), np.cos(phi)]])
    k = np.arange(S, dtype=np.int64)
    b = np.arange(FFT_INNER, dtype=np.int64)
    th = 2.0 * np.pi * ((k[:, None] * b[None, :]) % S) / S
    m = np.concatenate([np.cos(th), np.sin(th)], axis=1) / math.sqrt(S * C)
    m3 = m.reshape(FFT_INNER, Na, 2 * FFT_INNER).transpose(1, 0, 2)
    return (jnp.asarray(w0, BF16), jnp.asarray(r, BF16), jnp.asarray(m3, BF16))


def _mm_body(x_ref, w_ref, o_ref):
    o_ref[...] = jnp.dot(x_ref[...], w_ref[...], preferred_element_type=F32).astype(o_ref.dtype)


def _lmm_body(w_ref, x_ref, o_ref):
    o_ref[0] = jnp.dot(w_ref[...], x_ref[0], preferred_element_type=F32).astype(o_ref.dtype)


def _twiddle_body(m_ref, a_ref, o_ref):
    for ci in range(m_ref.shape[0]):
        o_ref[0, ci] = jnp.dot(m_ref[ci], a_ref[0, ci], preferred_element_type=F32).astype(o_ref.dtype)


def fourier_mix(z2d, B, S):
    T, W = z2d.shape
    G = F_GROUPS
    C = W // G
    Na = S // FFT_INNER
    w0, r, m3 = _fourier_constants(S, C)
    tm = min(1024, T)
    pq = pl.pallas_call(
        _mm_body,
        grid=(T // tm, G),
        in_specs=[pl.BlockSpec((tm, C), lambda i, g: (i, g)),
                  pl.BlockSpec((C, 2 * C), lambda i, g: (0, 0))],
        out_specs=pl.BlockSpec((tm, 2 * C), lambda i, g: (i, g)),
        out_shape=jax.ShapeDtypeStruct((T, G * 2 * C), BF16),
        compiler_params=_params("parallel", "arbitrary"),
        name="fourier_channel",
    )(z2d, w0)
    x1 = pq.reshape(B, Na, FFT_INNER, G, 2, C).transpose(0, 3, 4, 1, 2, 5)
    x1 = x1.reshape(B * G, 2 * Na, FFT_INNER * C)
    tn = min(8192, FFT_INNER * C)
    a1 = pl.pallas_call(
        _lmm_body,
        grid=(B * G, (FFT_INNER * C) // tn),
        in_specs=[pl.BlockSpec((2 * Na, 2 * Na), lambda s, j: (0, 0)),
                  pl.BlockSpec((1, 2 * Na, tn), lambda s, j: (s, 0, j))],
        out_specs=pl.BlockSpec((1, 2 * Na, tn), lambda s, j: (s, 0, j)),
        out_shape=jax.ShapeDtypeStruct((B * G, 2 * Na, FFT_INNER * C), BF16),
        compiler_params=_params("parallel", "arbitrary"),
        name="fourier_outer",
    )(r, x1)
    a2 = a1.reshape(B * G, 2, Na, FFT_INNER, C).transpose(0, 2, 1, 3, 4)
    a2 = a2.reshape(B * G, Na, 2 * FFT_INNER, C)
    cb = min(8, Na)
    y = pl.pallas_call(
        _twiddle_body,
        grid=(B * G, Na // cb),
        in_specs=[pl.BlockSpec((cb, FFT_INNER, 2 * FFT_INNER), lambda s, j: (j, 0, 0)),
                  pl.BlockSpec((1, cb, 2 * FFT_INNER, C), lambda s, j: (s, j, 0, 0))],
        out_specs=pl.BlockSpec((1, cb, FFT_INNER, C), lambda s, j: (s, j, 0, 0)),
        out_shape=jax.ShapeDtypeStruct((B * G, Na, FFT_INNER, C), BF16),
        compiler_params=_params("parallel", "arbitrary"),
        name="fourier_inner",
    )(m3, a2)
    y = y.reshape(B, G, Na, FFT_INNER, C).transpose(0, 3, 2, 1, 4)
    return y.reshape(T, G * C)


def _attn_body(slope_ref, lamv_ref, g_ref, qt_ref, ka_ref, q_ref, k_ref, v_ref, o_ref,
               m_ref, l_ref, acc_ref, *, tq, tk, dh, lam_init):
    h = pl.program_id(1)
    qi = pl.program_id(2)
    ki = pl.program_id(3)
    dv = 2 * dh

    @pl.when(ki == 0)
    def _():
        m_ref[...] = jnp.full(m_ref.shape, NEG_INF, F32)
        l_ref[...] = jnp.zeros(l_ref.shape, F32)
        acc_ref[...] = jnp.zeros(acc_ref.shape, F32)

    slope = slope_ref[h]

    def online_softmax_update(scores):
        v = v_ref[...]
        for c in range(2):
            s = scores[c]
            m_prev = m_ref[c]
            m_new = jnp.maximum(m_prev, jnp.max(s, axis=-1, keepdims=True))
            alpha = jnp.exp(m_prev - m_new)
            p = jnp.exp(s - jnp.tile(m_new, (1, tk // LANES)))
            l_ref[c] = alpha * l_ref[c] + jnp.sum(p, axis=-1, keepdims=True)
            acc_ref[c] = (jnp.tile(alpha, (1, dv // LANES)) * acc_ref[c]
                          + jnp.dot(p.astype(BF16), v, preferred_element_type=F32))
            m_ref[c] = m_new

    nt = (((1,), (1,)), ((), ()))

    @pl.when(qi != ki)
    def _():
        off = (qi - ki) * tq
        sgn = jnp.where(off > 0, slope, -slope)
        qa = ((qt_ref[0] + off.astype(F32) * qt_ref[1]) * sgn).astype(BF16)
        ka = ka_ref[...]
        q = q_ref[...]
        k = k_ref[...]
        scores = []
        for c in range(2):
            qc = jnp.concatenate([q[:, c * dh:(c + 1) * dh], qa], axis=1)
            kc = jnp.concatenate([k[:, c * dh:(c + 1) * dh], ka], axis=1)
            scores.append(lax.dot_general(qc, kc, nt, preferred_element_type=F32))
        online_softmax_update(scores)

    @pl.when(qi == ki)
    def _():
        row = lax.broadcasted_iota(jnp.int32, (tq, tk), 0)
        col = lax.broadcasted_iota(jnp.int32, (tq, tk), 1)
        bias = jnp.abs(row - col).astype(F32) * (-slope)
        q = q_ref[...]
        k = k_ref[...]
        scores = [lax.dot_general(q[:, c * dh:(c + 1) * dh], k[:, c * dh:(c + 1) * dh], nt,
                                  preferred_element_type=F32) + bias for c in range(2)]
        online_softmax_update(scores)

    @pl.when(ki == pl.num_programs(3) - 1)
    def _():
        lv = lamv_ref[...]
        lam = (jnp.exp(jnp.sum(lv[0:1] * lv[1:2], axis=-1, keepdims=True))
               - jnp.exp(jnp.sum(lv[2:3] * lv[3:4], axis=-1, keepdims=True)) + lam_init)
        o = (acc_ref[0] / jnp.tile(l_ref[0], (1, dv // LANES))
             - lam * (acc_ref[1] / jnp.tile(l_ref[1], (1, dv // LANES))))
        ms = jnp.mean(o * o, axis=-1, keepdims=True)
        o_ref[...] = (o * lax.rsqrt(ms + EPS) * g_ref[...] * (1.0 - lam_init)).astype(o_ref.dtype)


def _alibi_templates(tq, tk):
    r = np.arange(tq)
    qt = np.zeros((2, tq, LANES), np.float32)
    qt[0, :, 0] = -(r // 256) * 256
    qt[0, :, 1] = -(r % 256)
    qt[0, :, 2] = 1.0
    qt[0, :, 3] = 1.0
    qt[1, :, 0] = -1.0
    c = np.arange(tk)
    ka = np.zeros((tk, LANES), np.float32)
    ka[:, 0] = 1.0
    ka[:, 1] = 1.0
    ka[:, 2] = c % 256
    ka[:, 3] = (c // 256) * 256
    return jnp.asarray(qt, F32), jnp.asarray(ka, BF16)


def diff_attention(qk2d, v2d, lamv, subln_g, B, S, lam_init, tile=512):
    T, W = qk2d.shape
    H = A_HEADS
    dh = W // (4 * H)
    dv = 2 * dh
    tq = tk = min(tile, S)
    nq, nk = S // tq, S // tk
    slopes = 2.0 ** (-8.0 * np.arange(1, H + 1) / H)
    assert np.all(np.log2(slopes) == np.round(np.log2(slopes))) and S <= 8192
    qt, ka = _alibi_templates(tq, tk)
    return pl.pallas_call(
        functools.partial(_attn_body, tq=tq, tk=tk, dh=dh, lam_init=lam_init),
        grid=(B, H, nq, nk),
        in_specs=[pl.BlockSpec(memory_space=pltpu.SMEM),
                  pl.BlockSpec((4, dh), lambda b, h, i, j: (0, 0)),
                  pl.BlockSpec((1, dv), lambda b, h, i, j: (0, 0)),
                  pl.BlockSpec((2, tq, LANES), lambda b, h, i, j: (0, 0, 0)),
                  pl.BlockSpec((tk, LANES), lambda b, h, i, j: (0, 0)),
                  pl.BlockSpec((tq, dv), lambda b, h, i, j: (b * nq + i, h)),
                  pl.BlockSpec((tk, dv), lambda b, h, i, j: (b * nk + j, H + h)),
                  pl.BlockSpec((tk, dv), lambda b, h, i, j: (b * nk + j, h))],
        out_specs=pl.BlockSpec((tq, dv), lambda b, h, i, j: (b * nq + i, h)),
        out_shape=jax.ShapeDtypeStruct((T, H * dv), BF16),
        scratch_shapes=[pltpu.VMEM((2, tq, LANES), F32), pltpu.VMEM((2, tq, LANES), F32),
                        pltpu.VMEM((2, tq, dv), F32)],
        compiler_params=_params("parallel", "parallel", "parallel", "arbitrary"),
        name="diff_attention",
    )(jnp.asarray(slopes, F32), lamv, subln_g.reshape(1, dv).astype(F32), qt, ka, qk2d, qk2d, v2d)


def _merge_body(fm_ref, on_ref, gf_ref, ga_ref, wf_ref, wa_ref, o_ref):
    yf = jnp.dot(fm_ref[...], wf_ref[...], preferred_element_type=F32)
    ya = jnp.dot(on_ref[...], wa_ref[...], preferred_element_type=F32)
    o_ref[...] = (gf_ref[...].astype(F32) * yf + ga_ref[...].astype(F32) * ya).astype(o_ref.dtype)


def gated_merge(fm, on, gates, wf, wa, tm=256):
    T, D = on.shape
    tm = min(tm, T)
    return pl.pallas_call(
        _merge_body,
        grid=(T // tm,),
        in_specs=[pl.BlockSpec((tm, fm.shape[1]), lambda i: (i, 0)),
                  pl.BlockSpec((tm, D), lambda i: (i, 0)),
                  pl.BlockSpec((tm, D), lambda i: (i, 0)),
                  pl.BlockSpec((tm, D), lambda i: (i, 1)),
                  pl.BlockSpec(wf.shape, lambda i: (0, 0)),
                  pl.BlockSpec(wa.shape, lambda i: (0, 0))],
        out_specs=pl.BlockSpec((tm, D), lambda i: (i, 0)),
        out_shape=jax.ShapeDtypeStruct((T, D), BF16),
        compiler_params=_params("parallel"),
        name="gated_merge",
    )(fm, on, gates, gates, wf, wa)


def _out_body(mx_ref, x_ref, wo_ref, g_ref, h_ref, hn_ref):
    h = x_ref[...] + jnp.dot(mx_ref[...], wo_ref[...], preferred_element_type=F32)
    h_ref[...] = h
    ms = jnp.mean(h * h, axis=-1, keepdims=True)
    hn_ref[...] = (h * lax.rsqrt(ms + EPS) * g_ref[...]).astype(hn_ref.dtype)


def out_projection(mixed, x2d, wo, g2, tm=512):
    T, D = x2d.shape
    tm = min(tm, T)
    return pl.pallas_call(
        _out_body,
        grid=(T // tm,),
        in_specs=[pl.BlockSpec((tm, D), lambda i: (i, 0)),
                  pl.BlockSpec((tm, D), lambda i: (i, 0)),
                  pl.BlockSpec((D, D), lambda i: (0, 0)),
                  pl.BlockSpec((1, D), lambda i: (0, 0))],
        out_specs=[pl.BlockSpec((tm, D), lambda i: (i, 0)),
                   pl.BlockSpec((tm, D), lambda i: (i, 0))],
        out_shape=[jax.ShapeDtypeStruct((T, D), F32), jax.ShapeDtypeStruct((T, D), BF16)],
        compiler_params=_params("parallel"),
        name="out_projection",
    )(mixed, x2d, wo, g2.reshape(1, D).astype(F32))


def _extract_top(x, dst_ref, n):
    for r in range(n):
        mx = jnp.max(x, axis=0, keepdims=True)
        dst_ref[pl.ds(r, 1), :] = mx
        x = jnp.where(x >= mx, NEG_INF, x)


def _peer_query_body(hn_ref, wq_ref, keys_ref, s1_ref, s2_ref, st_ref, sv_ref, cs_ref, *, nk):
    qp = jnp.dot(hn_ref[...], wq_ref[...], preferred_element_type=F32).astype(BF16)
    half = keys_ref.shape[-1]
    for h in range(P_HEADS):
        for c in range(2):
            lo = (2 * h + c) * half
            st = lax.dot_general(keys_ref[h, c], qp[:, lo:lo + half], (((1,), (1,)), ((), ())),
                                 preferred_element_type=F32)
            (s1_ref if c == 0 else s2_ref)[h] = st
            _extract_top(st, sv_ref.at[c], P_TOPK)
        sv1 = sv_ref[0]
        sv2 = sv_ref[1]
        parts = [sv1[0:1] + sv2]
        for a in range(1, 8):
            parts.append(sv1[a:a + 1] + sv2[0:8])
        parts.append(sv1[8:16] + sv2[0:1])
        _extract_top(jnp.concatenate(parts, axis=0), cs_ref, P_TOPK)
        cs = cs_ref[...]
        top = cs[0:1]
        z = jnp.sum(jnp.exp(cs - top), axis=0, keepdims=True)
        st_ref[0, pl.ds(h, 1), :] = cs[P_TOPK - 1:P_TOPK]
        st_ref[1, pl.ds(h, 1), :] = sv1[0:1]
        st_ref[2, pl.ds(h, 1), :] = sv2[0:1]
        st_ref[3, pl.ds(h, 1), :] = 1.0 / z


def peer_query(hn, wq, keys, tt=256):
    T, D = hn.shape
    H, _, nk, half = keys.shape
    tt = min(tt, T)
    return pl.pallas_call(
        functools.partial(_peer_query_body, nk=nk),
        grid=(T // tt,),
        in_specs=[pl.BlockSpec((tt, D), lambda i: (i, 0)),
                  pl.BlockSpec(wq.shape, lambda i: (0, 0)),
                  pl.BlockSpec(keys.shape, lambda i: (0, 0, 0, 0))],
        out_specs=[pl.BlockSpec((H, nk, tt), lambda i: (0, 0, i)),
                   pl.BlockSpec((H, nk, tt), lambda i: (0, 0, i)),
                   pl.BlockSpec((4, H, tt), lambda i: (0, 0, i))],
        out_shape=[jax.ShapeDtypeStruct((H, nk, T), F32), jax.ShapeDtypeStruct((H, nk, T), F32),
                   jax.ShapeDtypeStruct((4, H, T), F32)],
        scratch_shapes=[pltpu.VMEM((2, P_TOPK, tt), F32), pltpu.VMEM((P_TOPK, tt), F32)],
        compiler_params=_params("parallel"),
        name="peer_query",
    )(hn, wq, keys)


def _peer_dense_body(hn_ref, u_ref, v_ref, s1_ref, s2_ref, st_ref, h_ref, o_ref, e1_ref, e2_ref,
                     *, nk, rows_per_chunk):
    e = pl.program_id(1)

    @pl.when(e == 0)
    def _():
        o_ref[...] = h_ref[...]
        for h in range(P_HEADS):
            e1_ref[h] = jnp.exp(s1_ref[h] - st_ref[1, pl.ds(h, 1), :])
            e2_ref[h] = jnp.exp(s2_ref[h] - st_ref[2, pl.ds(h, 1), :]) * st_ref[3, pl.ds(h, 1), :]

    sc = lax.dot_general(u_ref[...], hn_ref[...], (((1,), (1,)), ((), ())),
                         preferred_element_type=F32)
    act = 0.5 * sc * (1.0 + lax.erf(sc * (1.0 / math.sqrt(2.0))))
    rows = []
    for ii in range(rows_per_chunk):
        i = e * rows_per_chunk + ii
        w = jnp.zeros((nk, sc.shape[1]), F32)
        for h in range(P_HEADS):
            cand = s2_ref[h] + s1_ref[h, pl.ds(i, 1), :]
            w = w + jnp.where(cand >= st_ref[0, pl.ds(h, 1), :],
                              e2_ref[h] * e1_ref[h, pl.ds(i, 1), :], 0.0)
        rows.append((act[ii * nk:(ii + 1) * nk] * w).astype(BF16))
    a = jnp.concatenate(rows, axis=0)
    o_ref[...] += lax.dot_general(a, v_ref[...], (((0,), (0,)), ((), ())), preferred_element_type=F32)


def peer_dense(hn, u, v, s1, s2, stats, h1, tt=512, ec=512):
    T, D = hn.shape
    E = u.shape[0]
    H, nk, _ = s1.shape
    tt = min(tt, T)
    return pl.pallas_call(
        functools.partial(_peer_dense_body, nk=nk, rows_per_chunk=ec // nk),
        grid=(T // tt, E // ec),
        in_specs=[pl.BlockSpec((tt, D), lambda t, e: (t, 0)),
                  pl.BlockSpec((ec, D), lambda t, e: (e, 0)),
                  pl.BlockSpec((ec, D), lambda t, e: (e, 0)),
                  pl.BlockSpec((H, nk, tt), lambda t, e: (0, 0, t)),
                  pl.BlockSpec((H, nk, tt), lambda t, e: (0, 0, t)),
                  pl.BlockSpec((4, H, tt), lambda t, e: (0, 0, t)),
                  pl.BlockSpec((tt, D), lambda t, e: (t, 0))],
        out_specs=pl.BlockSpec((tt, D), lambda t, e: (t, 0)),
        out_shape=jax.ShapeDtypeStruct((T, D), F32),
        scratch_shapes=[pltpu.VMEM((H, nk, tt), F32), pltpu.VMEM((H, nk, tt), F32)],
        compiler_params=_params("parallel", "arbitrary"),
        name="peer_dense",
    )(hn, u, v, s1, s2, stats, h1)


def kernel(x, norm1_g, w_in, w_fourier, w_attn, q_norm_g, k_norm_g, lambda_q1, lambda_k1,
           lambda_q2, lambda_k2, subln_g, w_out, norm2_g, w_query, sub_keys, expert_u, expert_v):
    B, S, D = x.shape
    T = B * S
    depth = w_in.shape[0]
    dh = D // (2 * A_HEADS)
    f_width = w_fourier.shape[1]
    qk_width = A_HEADS * 2 * dh
    v_width = w_attn.shape[1]
    o_q = f_width
    o_k = o_q + qk_width
    o_v = o_k + qk_width
    o_g = o_v + v_width
    h = x.reshape(T, D)
    for i in range(depth):
        lam_init = 0.8 - 0.6 * math.exp(-0.3 * i)
        xn = rmsnorm_rows(h, norm1_g[i])
        w_in_b = w_in[i].astype(BF16)
        qk_gain = jnp.concatenate([jnp.tile(q_norm_g[i].astype(F32) * dh ** -0.5, 2 * A_HEADS),
                                   jnp.tile(k_norm_g[i].astype(F32), 2 * A_HEADS)]).reshape(1, 2 * qk_width)
        z = project(xn, w_in_b, 0, f_width, "cast")
        qk = project(xn, w_in_b, o_q, 2 * qk_width, "headnorm", gain=qk_gain, group=dh)
        vv = project(xn, w_in_b, o_v, v_width, "cast")
        gates = project(xn, w_in_b, o_g, 2 * D, "sigmoid")

        fm = fourier_mix(z, B, S)
        lamv = jnp.stack([lambda_q1[i], lambda_k1[i], lambda_q2[i], lambda_k2[i]]).astype(F32)
        on = diff_attention(qk, vv, lamv, subln_g[i], B, S, lam_init)

        mixed = gated_merge(fm, on, gates, w_fourier[i].astype(BF16), w_attn[i].astype(BF16))
        h1, hn = out_projection(mixed, h, w_out[i].astype(BF16), norm2_g[i])

        s1, s2, stats = peer_query(hn, w_query[i].astype(BF16), sub_keys[i].astype(BF16))
        h = peer_dense(hn, expert_u[i].astype(BF16), expert_v[i].astype(BF16), s1, s2, stats, h1)
    return h.reshape(B, S, D)
```

```python
import functools
import math

import numpy as np
import jax
import jax.numpy as jnp
from jax import lax
from jax.experimental import pallas as pl
from jax.experimental.pallas import tpu as pltpu

EPS = 1e-6
F_GROUPS = 4
A_HEADS = 8
P_HEADS = 8
P_TOPK = 16
LANES = 128
MXU_COLS = 256
FFT_INNER = 128
VMEM_LIMIT_BYTES = 56 * 1024 * 1024

F32 = jnp.float32
BF16 = jnp.bfloat16
NEG_INF = float("-inf")


def _params(*semantics):
    return pltpu.CompilerParams(dimension_semantics=semantics, vmem_limit_bytes=VMEM_LIMIT_BYTES)


def _rmsnorm_body(x_ref, g_ref, o_ref):
    x = x_ref[...]
    ms = jnp.mean(x * x, axis=-1, keepdims=True)
    o_ref[...] = (x * lax.rsqrt(ms + EPS) * g_ref[...]).astype(o_ref.dtype)


def rmsnorm_rows(x2d, g, tm=512):
    T, D = x2d.shape
    return pl.pallas_call(
        _rmsnorm_body,
        grid=(T // tm,),
        in_specs=[pl.BlockSpec((tm, D), lambda i: (i, 0)),
                  pl.BlockSpec((1, D), lambda i: (0, 0))],
        out_specs=pl.BlockSpec((tm, D), lambda i: (i, 0)),
        out_shape=jax.ShapeDtypeStruct((T, D), BF16),
        compiler_params=_params("parallel"),
        name="rmsnorm",
    )(x2d, g.reshape(1, D).astype(F32))


def _proj_body(x_ref, w_ref, g_ref, o_ref, *, mode, group):
    acc = jnp.dot(x_ref[...], w_ref[...], preferred_element_type=F32)
    if mode == "cast":
        o_ref[...] = acc.astype(o_ref.dtype)
    elif mode == "sigmoid":
        o_ref[...] = jax.nn.sigmoid(acc).astype(o_ref.dtype)
    else:
        for s in range(acc.shape[1] // group):
            blk = acc[:, s * group:(s + 1) * group]
            ms = jnp.mean(blk * blk, axis=-1, keepdims=True)
            o_ref[:, s * group:(s + 1) * group] = (
                blk * lax.rsqrt(ms + EPS) * g_ref[:, s * group:(s + 1) * group]).astype(o_ref.dtype)


def project(x, w, col0, ncols, mode, gain=None, group=LANES, tm=1024, tn=1024):
    T, K = x.shape
    tm = min(tm, T)
    if gain is None:
        gain = jnp.ones((1, ncols), F32)
    off = col0 // tn
    return pl.pallas_call(
        functools.partial(_proj_body, mode=mode, group=group),
        grid=(T // tm, ncols // tn),
        in_specs=[pl.BlockSpec((tm, K), lambda i, j: (i, 0)),
                  pl.BlockSpec((K, tn), lambda i, j: (0, off + j)),
                  pl.BlockSpec((1, tn), lambda i, j: (0, j))],
        out_specs=pl.BlockSpec((tm, tn), lambda i, j: (i, j)),
        out_shape=jax.ShapeDtypeStruct((T, ncols), BF16),
        compiler_params=_params("parallel", "arbitrary"),
        name="proj_" + mode,
    )(x, w, gain)


def _fourier_constants(S, C):
    Na = S // FFT_INNER
    n2 = np.arange(C, dtype=np.int64)
    ang = 2.0 * np.pi * ((n2[:, None] * n2[None, :]) % C) / C
    w0 = np.concatenate([np.cos(ang), -np.sin(ang)], axis=1)
    a = np.arange(Na, dtype=np.int64)
    phi = 2.0 * np.pi * ((a[:, None] * a[None, :]) % Na) / Na
    r = np.block([[np.cos(phi), np.sin(phi)], [-np.sin(phi), np.cos(phi)]])
    k = np.arange(S, dtype=np.int64)
    b = np.arange(FFT_INNER, dtype=np.int64)
    th = 2.0 * np.pi * ((k[:, None] * b[None, :]) % S) / S
    m = np.concatenate([np.cos(th), np.sin(th)], axis=1) / math.sqrt(S * C)
    m3 = m.reshape(FFT_INNER, Na, 2 * FFT_INNER).transpose(1, 0, 2)
    return (jnp.asarray(w0, BF16), jnp.asarray(r, BF16), jnp.asarray(m3, BF16))


def _mm_body(x_ref, w_ref, o_ref):
    o_ref[...] = jnp.dot(x_ref[...], w_ref[...], preferred_element_type=F32).astype(o_ref.dtype)


def _lmm_body(w_ref, x_ref, o_ref):
    o_ref[0] = jnp.dot(w_ref[...], x_ref[0], preferred_element_type=F32).astype(o_ref.dtype)


def _twiddle_body(m_ref, a_ref, o_ref):
    for ci in range(m_ref.shape[0]):
        o_ref[0, ci] = jnp.dot(m_ref[ci], a_ref[0, ci], preferred_element_type=F32).astype(o_ref.dtype)


def fourier_mix(z2d, B, S):
    T, W = z2d.shape
    G = F_GROUPS
    C = W // G
    Na = S // FFT_INNER
    w0, r, m3 = _fourier_constants(S, C)
    tm = min(1024, T)
    pq = pl.pallas_call(
        _mm_body,
        grid=(T // tm, G),
        in_specs=[pl.BlockSpec((tm, C), lambda i, g: (i, g)),
                  pl.BlockSpec((C, 2 * C), lambda i, g: (0, 0))],
        out_specs=pl.BlockSpec((tm, 2 * C), lambda i, g: (i, g)),
        out_shape=jax.ShapeDtypeStruct((T, G * 2 * C), BF16),
        compiler_params=_params("parallel", "arbitrary"),
        name="fourier_channel",
    )(z2d, w0)
    x1 = pq.reshape(B, Na, FFT_INNER, G, 2, C).transpose(0, 3, 4, 1, 2, 5)
    x1 = x1.reshape(B * G, 2 * Na, FFT_INNER * C)
    tn = min(8192, FFT_INNER * C)
    a1 = pl.pallas_call(
        _lmm_body,
        grid=(B * G, (FFT_INNER * C) // tn),
        in_specs=[pl.BlockSpec((2 * Na, 2 * Na), lambda s, j: (0, 0)),
                  pl.BlockSpec((1, 2 * Na, tn), lambda s, j: (s, 0, j))],
        out_specs=pl.BlockSpec((1, 2 * Na, tn), lambda s, j: (s, 0, j)),
        out_shape=jax.ShapeDtypeStruct((B * G, 2 * Na, FFT_INNER * C), BF16),
        compiler_params=_params("parallel", "arbitrary"),
        name="fourier_outer",
    )(r, x1)
    a2 = a1.reshape(B * G, 2, Na, FFT_INNER, C).transpose(0, 2, 1, 3, 4)
    a2 = a2.reshape(B * G, Na, 2 * FFT_INNER, C)
    cb = min(8, Na)
    y = pl.pallas_call(
        _twiddle_body,
        grid=(B * G, Na // cb),
        in_specs=[pl.BlockSpec((cb, FFT_INNER, 2 * FFT_INNER), lambda s, j: (j, 0, 0)),
                  pl.BlockSpec((1, cb, 2 * FFT_INNER, C), lambda s, j: (s, j, 0, 0))],
        out_specs=pl.BlockSpec((1, cb, FFT_INNER, C), lambda s, j: (s, j, 0, 0)),
        out_shape=jax.ShapeDtypeStruct((B * G, Na, FFT_INNER, C), BF16),
        compiler_params=_params("parallel", "arbitrary"),
        name="fourier_inner",
    )(m3, a2)
    y = y.reshape(B, G, Na, FFT_INNER, C).transpose(0, 3, 2, 1, 4)
    return y.reshape(T, G * C)


def _attn_body(slope_ref, lamv_ref, g_ref, qt_ref, ka_ref, q_ref, k_ref, v_ref, o_ref,
               m_ref, l_ref, acc_ref, *, tq, tk, dh, lam_init):
    h = pl.program_id(1)
    qi = pl.program_id(2)
    ki = pl.program_id(3)
    dv = 2 * dh

    @pl.when(ki == 0)
    def _():
        m_ref[...] = jnp.full(m_ref.shape, NEG_INF, F32)
        l_ref[...] = jnp.zeros(l_ref.shape, F32)
        acc_ref[...] = jnp.zeros(acc_ref.shape, F32)

    slope = slope_ref[h]

    def online_softmax_update(scores):
        v = v_ref[...]
        for c in range(2):
            s = scores[c]
            m_prev = m_ref[c]
            m_new = jnp.maximum(m_prev, jnp.max(s, axis=-1, keepdims=True))
            alpha = jnp.exp(m_prev - m_new)
            p = jnp.exp(s - jnp.tile(m_new, (1, tk // LANES)))
            l_ref[c] = alpha * l_ref[c] + jnp.sum(p, axis=-1, keepdims=True)
            acc_ref[c] = (jnp.tile(alpha, (1, dv // LANES)) * acc_ref[c]
                          + jnp.dot(p.astype(BF16), v, preferred_element_type=F32))
            m_ref[c] = m_new

    nt = (((1,), (1,)), ((), ()))

    @pl.when(qi != ki)
    def _():
        off = (qi - ki) * tq
        sgn = jnp.where(off > 0, slope, -slope)
        qa = ((qt_ref[0] + off.astype(F32) * qt_ref[1]) * sgn).astype(BF16)
        ka = ka_ref[...]
        q = q_ref[...]
        k = k_ref[...]
        scores = []
        for c in range(2):
            qc = jnp.concatenate([q[:, c * dh:(c + 1) * dh], qa], axis=1)
            kc = jnp.concatenate([k[:, c * dh:(c + 1) * dh], ka], axis=1)
            scores.append(lax.dot_general(qc, kc, nt, preferred_element_type=F32))
        online_softmax_update(scores)

    @pl.when(qi == ki)
    def _():
        row = lax.broadcasted_iota(jnp.int32, (tq, tk), 0)
        col = lax.broadcasted_iota(jnp.int32, (tq, tk), 1)
        bias = jnp.abs(row - col).astype(F32) * (-slope)
        q = q_ref[...]
        k = k_ref[...]
        scores = [lax.dot_general(q[:, c * dh:(c + 1) * dh], k[:, c * dh:(c + 1) * dh], nt,
                                  preferred_element_type=F32) + bias for c in range(2)]
        online_softmax_update(scores)

    @pl.when(ki == pl.num_programs(3) - 1)
    def _():
        lv = lamv_ref[...]
        lam = (jnp.exp(jnp.sum(lv[0:1] * lv[1:2], axis=-1, keepdims=True))
               - jnp.exp(jnp.sum(lv[2:3] * lv[3:4], axis=-1, keepdims=True)) + lam_init)
        o = (acc_ref[0] / jnp.tile(l_ref[0], (1, dv // LANES))
             - lam * (acc_ref[1] / jnp.tile(l_ref[1], (1, dv // LANES))))
        ms = jnp.mean(o * o, axis=-1, keepdims=True)
        o_ref[...] = (o * lax.rsqrt(ms + EPS) * g_ref[...] * (1.0 - lam_init)).astype(o_ref.dtype)


def _alibi_templates(tq, tk):
    r = np.arange(tq)
    qt = np.zeros((2, tq, LANES), np.float32)
    qt[0, :, 0] = -(r // 256) * 256
    qt[0, :, 1] = -(r % 256)
    qt[0, :, 2] = 1.0
    qt[0, :, 3] = 1.0
    qt[1, :, 0] = -1.0
    c = np.arange(tk)
    ka = np.zeros((tk, LANES), np.float32)
    ka[:, 0] = 1.0
    ka[:, 1] = 1.0
    ka[:, 2] = c % 256
    ka[:, 3] = (c // 256) * 256
    return jnp.asarray(qt, F32), jnp.asarray(ka, BF16)


def diff_attention(qk2d, v2d, lamv, subln_g, B, S, lam_init, tile=1024):
    T, W = qk2d.shape
    H = A_HEADS
    dh = W // (4 * H)
    dv = 2 * dh
    tq = tk = min(tile, S)
    nq, nk = S // tq, S // tk
    slopes = 2.0 ** (-8.0 * np.arange(1, H + 1) / H)
    assert np.all(np.log2(slopes) == np.round(np.log2(slopes))) and S <= 8192
    qt, ka = _alibi_templates(tq, tk)
    return pl.pallas_call(
        functools.partial(_attn_body, tq=tq, tk=tk, dh=dh, lam_init=lam_init),
        grid=(B, H, nq, nk),
        in_specs=[pl.BlockSpec(memory_space=pltpu.SMEM),
                  pl.BlockSpec((4, dh), lambda b, h, i, j: (0, 0)),
                  pl.BlockSpec((1, dv), lambda b, h, i, j: (0, 0)),
                  pl.BlockSpec((2, tq, LANES), lambda b, h, i, j: (0, 0, 0)),
                  pl.BlockSpec((tk, LANES), lambda b, h, i, j: (0, 0)),
                  pl.BlockSpec((tq, dv), lambda b, h, i, j: (b * nq + i, h)),
                  pl.BlockSpec((tk, dv), lambda b, h, i, j: (b * nk + j, H + h)),
                  pl.BlockSpec((tk, dv), lambda b, h, i, j: (b * nk + j, h))],
        out_specs=pl.BlockSpec((tq, dv), lambda b, h, i, j: (b * nq + i, h)),
        out_shape=jax.ShapeDtypeStruct((T, H * dv), BF16),
        scratch_shapes=[pltpu.VMEM((2, tq, LANES), F32), pltpu.VMEM((2, tq, LANES), F32),
                        pltpu.VMEM((2, tq, dv), F32)],
        compiler_params=_params("parallel", "parallel", "parallel", "arbitrary"),
        name="diff_attention",
    )(jnp.asarray(slopes, F32), lamv, subln_g.reshape(1, dv).astype(F32), qt, ka, qk2d, qk2d, v2d)


def _merge_body(fm_ref, on_ref, gf_ref, ga_ref, wf_ref, wa_ref, o_ref):
    yf = jnp.dot(fm_ref[...], wf_ref[...], preferred_element_type=F32)
    ya = jnp.dot(on_ref[...], wa_ref[...], preferred_element_type=F32)
    o_ref[...] = (gf_ref[...].astype(F32) * yf + ga_ref[...].astype(F32) * ya).astype(o_ref.dtype)


def gated_merge(fm, on, gates, wf, wa, tm=256):
    T, D = on.shape
    tm = min(tm, T)
    return pl.pallas_call(
        _merge_body,
        grid=(T // tm,),
        in_specs=[pl.BlockSpec((tm, fm.shape[1]), lambda i: (i, 0)),
                  pl.BlockSpec((tm, D), lambda i: (i, 0)),
                  pl.BlockSpec((tm, D), lambda i: (i, 0)),
                  pl.BlockSpec((tm, D), lambda i: (i, 1)),
                  pl.BlockSpec(wf.shape, lambda i: (0, 0)),
                  pl.BlockSpec(wa.shape, lambda i: (0, 0))],
        out_specs=pl.BlockSpec((tm, D), lambda i: (i, 0)),
        out_shape=jax.ShapeDtypeStruct((T, D), BF16),
        compiler_params=_params("parallel"),
        name="gated_merge",
    )(fm, on, gates, gates, wf, wa)


def _out_body(mx_ref, x_ref, wo_ref, g_ref, h_ref, hn_ref):
    h = x_ref[...] + jnp.dot(mx_ref[...], wo_ref[...], preferred_element_type=F32)
    h_ref[...] = h
    ms = jnp.mean(h * h, axis=-1, keepdims=True)
    hn_ref[...] = (h * lax.rsqrt(ms + EPS) * g_ref[...]).astype(hn_ref.dtype)


def out_projection(mixed, x2d, wo, g2, tm=512):
    T, D = x2d.shape
    tm = min(tm, T)
    return pl.pallas_call(
        _out_body,
        grid=(T // tm,),
        in_specs=[pl.BlockSpec((tm, D), lambda i: (i, 0)),
                  pl.BlockSpec((tm, D), lambda i: (i, 0)),
                  pl.BlockSpec((D, D), lambda i: (0, 0)),
                  pl.BlockSpec((1, D), lambda i: (0, 0))],
        out_specs=[pl.BlockSpec((tm, D), lambda i: (i, 0)),
                   pl.BlockSpec((tm, D), lambda i: (i, 0))],
        out_shape=[jax.ShapeDtypeStruct((T, D), F32), jax.ShapeDtypeStruct((T, D), BF16)],
        compiler_params=_params("parallel"),
        name="out_projection",
    )(mixed, x2d, wo, g2.reshape(1, D).astype(F32))


def _extract_top(x, dst_ref, n):
    for r in range(n):
        mx = jnp.max(x, axis=0, keepdims=True)
        dst_ref[pl.ds(r, 1), :] = mx
        x = jnp.where(x >= mx, NEG_INF, x)


def _peer_query_body(hn_ref, wq_ref, keys_ref, th_ref, e1_ref, s2_ref, e2_ref, sv_ref, cs_ref):
    qp = jnp.dot(hn_ref[...], wq_ref[...], preferred_element_type=F32).astype(BF16)
    half = keys_ref.shape[-1]
    nt = (((1,), (1,)), ((), ()))
    for h in range(P_HEADS):
        lo = 2 * h * half
        st1 = lax.dot_general(keys_ref[h, 0], qp[:, lo:lo + half], nt, preferred_element_type=F32)
        st2 = lax.dot_general(keys_ref[h, 1], qp[:, lo + half:lo + 2 * half], nt,
                              preferred_element_type=F32)
        _extract_top(st1, sv_ref.at[0], P_TOPK)
        _extract_top(st2, sv_ref.at[1], P_TOPK)
        sv1 = sv_ref[0]
        sv2 = sv_ref[1]
        parts = [sv1[0:1] + sv2]
        for a in range(1, 8):
            parts.append(sv1[a:a + 1] + sv2[0:8])
        parts.append(sv1[8:16] + sv2[0:1])
        _extract_top(jnp.concatenate(parts, axis=0), cs_ref, P_TOPK)
        cs = cs_ref[...]
        tau = cs[P_TOPK - 1:P_TOPK]
        z = jnp.sum(jnp.exp(cs - cs[0:1]), axis=0, keepdims=True)
        th = jnp.full(st1.shape, jnp.inf, F32)
        for a in range(P_TOPK):
            va = sv1[a:a + 1]
            th_a = jnp.min(jnp.where(va + sv2 >= tau, sv2, jnp.inf), axis=0, keepdims=True)
            th = jnp.where(st1 == va, th_a, th)
        e1 = jnp.exp(st1 - sv1[0:1])
        e2 = jnp.exp(st2 - sv2[0:1]) / z
        for tb in range(st1.shape[1] // LANES):
            cols = slice(tb * LANES, (tb + 1) * LANES)
            th_ref[h, tb] = th[:, cols]
            e1_ref[h, tb] = e1[:, cols]
            s2_ref[h, tb] = st2[:, cols]
            e2_ref[h, tb] = e2[:, cols]


def peer_query(hn, wq, keys, tt=256):
    T, D = hn.shape
    H, _, nk, half = keys.shape
    tt = min(tt, T)
    spec = pl.BlockSpec((H, tt // LANES, nk, LANES), lambda i: (0, i, 0, 0))
    shape = jax.ShapeDtypeStruct((H, T // LANES, nk, LANES), F32)
    return pl.pallas_call(
        _peer_query_body,
        grid=(T // tt,),
        in_specs=[pl.BlockSpec((tt, D), lambda i: (i, 0)),
                  pl.BlockSpec(wq.shape, lambda i: (0, 0)),
                  pl.BlockSpec(keys.shape, lambda i: (0, 0, 0, 0))],
        out_specs=[spec, spec, spec, spec],
        out_shape=[shape, shape, shape, shape],
        scratch_shapes=[pltpu.VMEM((2, P_TOPK, tt), F32), pltpu.VMEM((P_TOPK, tt), F32)],
        compiler_params=_params("parallel"),
        name="peer_query",
    )(hn, wq, keys)


def _peer_dense_body(hn_ref, u_ref, v_ref, th_ref, e1_ref, s2_ref, e2_ref, h_ref, o_ref,
                     sc0_ref, sc1_ref, a0_ref, a1_ref, *, nk, rows_per_chunk, n_chunks):
    e = pl.program_id(1)

    @pl.when(e == 0)
    def _():
        o_ref[...] = h_ref[...]
        sc1_ref[...] = jnp.zeros(sc1_ref.shape, F32)
        a0_ref[...] = jnp.zeros(a0_ref.shape, BF16)
        a1_ref[...] = jnp.zeros(a1_ref.shape, BF16)

    def stages(sc_w, sc_r, a_w, a_r):
        tt = sc_r.shape[1]
        chunk = jnp.clip(e - 1, 0, n_chunks - 1)

        def up(half):
            rows = slice(half * (tt // 2), (half + 1) * (tt // 2))
            sc_w[:, rows] = lax.dot_general(u_ref[...], hn_ref[rows, :], (((1,), (1,)), ((), ())),
                                            preferred_element_type=F32)

        def gate(tb, ii):
            i = chunk * rows_per_chunk + ii
            w = jnp.zeros((nk, LANES), F32)
            for h in range(P_HEADS):
                sel = s2_ref[h, tb] >= th_ref[h, tb, pl.ds(i, 1), :]
                w = w + jnp.where(sel, e2_ref[h, tb] * e1_ref[h, tb, pl.ds(i, 1), :], 0.0)
            x = sc_r[ii * nk:(ii + 1) * nk, tb * LANES:(tb + 1) * LANES]
            act = 0.5 * x * (1.0 + lax.erf(x * (1.0 / math.sqrt(2.0))))
            a_w[tb * LANES:(tb + 1) * LANES, ii * nk:(ii + 1) * nk] = (act * w).T.astype(BF16)

        def down(n):
            cols = slice(n * MXU_COLS, (n + 1) * MXU_COLS)
            o_ref[:, cols] += jnp.dot(a_r[...], v_ref[:, cols], preferred_element_type=F32)

        blocks = [(tb, ii) for tb in range(tt // LANES) for ii in range(rows_per_chunk)]
        n_down = o_ref.shape[1] // MXU_COLS
        per_up = (len(blocks) - n_down) // 2
        for half in range(2):
            up(half)
            for blk in blocks[half * per_up:(half + 1) * per_up]:
                gate(*blk)
        for n in range(n_down):
            down(n)
            gate(*blocks[2 * per_up + n])

    @pl.when(e % 2 == 0)
    def _():
        stages(sc0_ref, sc1_ref, a1_ref, a0_ref)

    @pl.when(e % 2 == 1)
    def _():
        stages(sc1_ref, sc0_ref, a0_ref, a1_ref)


def peer_dense(hn, u, v, th, e1, s2, e2, h1, tt=512, ec=512):
    T, D = hn.shape
    E = u.shape[0]
    H, _, nk, _ = th.shape
    tt = min(tt, T)
    n_chunks = E // ec
    tok = pl.BlockSpec((H, tt // LANES, nk, LANES), lambda t, e: (0, t, 0, 0))
    return pl.pallas_call(
        functools.partial(_peer_dense_body, nk=nk, rows_per_chunk=ec // nk, n_chunks=n_chunks),
        grid=(T // tt, n_chunks + 2),
        in_specs=[pl.BlockSpec((tt, D), lambda t, e: (t, 0)),
                  pl.BlockSpec((ec, D), lambda t, e: (jnp.minimum(e, n_chunks - 1), 0)),
                  pl.BlockSpec((ec, D), lambda t, e: (jnp.clip(e - 2, 0, n_chunks - 1), 0)),
                  tok, tok, tok, tok,
                  pl.BlockSpec((tt, D), lambda t, e: (t, 0))],
        out_specs=pl.BlockSpec((tt, D), lambda t, e: (t, 0)),
        out_shape=jax.ShapeDtypeStruct((T, D), F32),
        scratch_shapes=[pltpu.VMEM((ec, tt), F32), pltpu.VMEM((ec, tt), F32),
                        pltpu.VMEM((tt, ec), BF16), pltpu.VMEM((tt, ec), BF16)],
        compiler_params=_params("parallel", "arbitrary"),
        name="peer_dense",
    )(hn, u, v, th, e1, s2, e2, h1)


def kernel(x, norm1_g, w_in, w_fourier, w_attn, q_norm_g, k_norm_g, lambda_q1, lambda_k1,
           lambda_q2, lambda_k2, subln_g, w_out, norm2_g, w_query, sub_keys, expert_u, expert_v):
    B, S, D = x.shape
    T = B * S
    depth = w_in.shape[0]
    dh = D // (2 * A_HEADS)
    f_width = w_fourier.shape[1]
    qk_width = A_HEADS * 2 * dh
    v_width = w_attn.shape[1]
    o_q = f_width
    o_k = o_q + qk_width
    o_v = o_k + qk_width
    o_g = o_v + v_width
    h = x.reshape(T, D)
    for i in range(depth):
        lam_init = 0.8 - 0.6 * math.exp(-0.3 * i)
        xn = rmsnorm_rows(h, norm1_g[i])
        w_in_b = w_in[i].astype(BF16)
        qk_gain = jnp.concatenate([jnp.tile(q_norm_g[i].astype(F32) * dh ** -0.5, 2 * A_HEADS),
                                   jnp.tile(k_norm_g[i].astype(F32), 2 * A_HEADS)]).reshape(1, 2 * qk_width)
        z = project(xn, w_in_b, 0, f_width, "cast")
        qk = project(xn, w_in_b, o_q, 2 * qk_width, "headnorm", gain=qk_gain, group=dh)
        vv = project(xn, w_in_b, o_v, v_width, "cast")
        gates = project(xn, w_in_b, o_g, 2 * D, "sigmoid")

        fm = fourier_mix(z, B, S)
        lamv = jnp.stack([lambda_q1[i], lambda_k1[i], lambda_q2[i], lambda_k2[i]]).astype(F32)
        on = diff_attention(qk, vv, lamv, subln_g[i], B, S, lam_init)

        mixed = gated_merge(fm, on, gates, w_fourier[i].astype(BF16), w_attn[i].astype(BF16))
        h1, hn = out_projection(mixed, h, w_out[i].astype(BF16), norm2_g[i])

        th, e1, s2, e2 = peer_query(hn, w_query[i].astype(BF16), sub_keys[i].astype(BF16))
        h = peer_dense(hn, expert_u[i].astype(BF16), expert_v[i].astype(BF16), th, e1, s2, e2, h1)
    return h.reshape(B, S, D)
```

```python
import functools
import math

import numpy as np
import jax
import jax.numpy as jnp
from jax import lax
from jax.experimental import pallas as pl
from jax.experimental.pallas import tpu as pltpu

EPS = 1e-6
F_GROUPS = 4
A_HEADS = 8
P_HEADS = 8
P_TOPK = 16
LANES = 128
MXU_COLS = 256
BF16_ROWS = 16
MAX_SAFE_SHIFT = 40.0
FFT_INNER = 128
VMEM_LIMIT_BYTES = 56 * 1024 * 1024

F32 = jnp.float32
BF16 = jnp.bfloat16
NEG_INF = float("-inf")


def _params(*semantics):
    return pltpu.CompilerParams(dimension_semantics=semantics, vmem_limit_bytes=VMEM_LIMIT_BYTES)


def _rmsnorm_body(x_ref, g_ref, o_ref):
    x = x_ref[...]
    ms = jnp.mean(x * x, axis=-1, keepdims=True)
    o_ref[...] = (x * lax.rsqrt(ms + EPS) * g_ref[...]).astype(o_ref.dtype)


def rmsnorm_rows(x2d, g, tm=512):
    T, D = x2d.shape
    return pl.pallas_call(
        _rmsnorm_body,
        grid=(T // tm,),
        in_specs=[pl.BlockSpec((tm, D), lambda i: (i, 0)),
                  pl.BlockSpec((1, D), lambda i: (0, 0))],
        out_specs=pl.BlockSpec((tm, D), lambda i: (i, 0)),
        out_shape=jax.ShapeDtypeStruct((T, D), BF16),
        compiler_params=_params("parallel"),
        name="rmsnorm",
    )(x2d, g.reshape(1, D).astype(F32))


def _proj_body(x_ref, w_ref, g_ref, o_ref, *, mode, group):
    acc = jnp.dot(x_ref[...], w_ref[...], preferred_element_type=F32)
    if mode == "cast":
        o_ref[...] = acc.astype(o_ref.dtype)
    elif mode == "sigmoid":
        o_ref[...] = jax.nn.sigmoid(acc).astype(o_ref.dtype)
    else:
        for s in range(acc.shape[1] // group):
            blk = acc[:, s * group:(s + 1) * group]
            ms = jnp.mean(blk * blk, axis=-1, keepdims=True)
            o_ref[:, s * group:(s + 1) * group] = (
                blk * lax.rsqrt(ms + EPS) * g_ref[:, s * group:(s + 1) * group]).astype(o_ref.dtype)


def project(x, w, col0, ncols, mode, gain=None, group=LANES, tm=1024, tn=1024):
    T, K = x.shape
    tm = min(tm, T)
    if gain is None:
        gain = jnp.ones((1, ncols), F32)
    off = col0 // tn
    return pl.pallas_call(
        functools.partial(_proj_body, mode=mode, group=group),
        grid=(T // tm, ncols // tn),
        in_specs=[pl.BlockSpec((tm, K), lambda i, j: (i, 0)),
                  pl.BlockSpec((K, tn), lambda i, j: (0, off + j)),
                  pl.BlockSpec((1, tn), lambda i, j: (0, j))],
        out_specs=pl.BlockSpec((tm, tn), lambda i, j: (i, j)),
        out_shape=jax.ShapeDtypeStruct((T, ncols), BF16),
        compiler_params=_params("parallel", "arbitrary"),
        name="proj_" + mode,
    )(x, w, gain)


def _fourier_constants(S, C):
    Na = S // FFT_INNER
    n2 = np.arange(C, dtype=np.int64)
    ang = 2.0 * np.pi * ((n2[:, None] * n2[None, :]) % C) / C
    w0 = np.concatenate([np.cos(ang), -np.sin(ang)], axis=1)
    a = np.arange(Na, dtype=np.int64)
    phi = 2.0 * np.pi * ((a[:, None] * a[None, :]) % Na) / Na
    r = np.block([[np.cos(phi), np.sin(phi)], [-np.sin(phi), np.cos(phi)]])
    k = np.arange(S, dtype=np.int64)
    b = np.arange(FFT_INNER, dtype=np.int64)
    th = 2.0 * np.pi * ((k[:, None] * b[None, :]) % S) / S
    m = np.concatenate([np.cos(th), np.sin(th)], axis=1) / math.sqrt(S * C)
    m3 = m.reshape(FFT_INNER, Na, 2 * FFT_INNER).transpose(1, 0, 2)
    return (jnp.asarray(w0, BF16), jnp.asarray(r, BF16), jnp.asarray(m3, BF16))


def _mm_body(x_ref, w_ref, o_ref):
    o_ref[...] = jnp.dot(x_ref[...], w_ref[...], preferred_element_type=F32).astype(o_ref.dtype)


def _lmm_body(w_ref, x_ref, o_ref):
    o_ref[0] = jnp.dot(w_ref[...], x_ref[0], preferred_element_type=F32).astype(o_ref.dtype)


def _twiddle_body(m_ref, a_ref, o_ref):
    for ci in range(m_ref.shape[0]):
        o_ref[0, ci] = jnp.dot(m_ref[ci], a_ref[0, ci], preferred_element_type=F32).astype(o_ref.dtype)


def fourier_mix(z2d, B, S):
    T, W = z2d.shape
    G = F_GROUPS
    C = W // G
    Na = S // FFT_INNER
    w0, r, m3 = _fourier_constants(S, C)
    tm = min(1024, T)
    pq = pl.pallas_call(
        _mm_body,
        grid=(T // tm, G),
        in_specs=[pl.BlockSpec((tm, C), lambda i, g: (i, g)),
                  pl.BlockSpec((C, 2 * C), lambda i, g: (0, 0))],
        out_specs=pl.BlockSpec((tm, 2 * C), lambda i, g: (i, g)),
        out_shape=jax.ShapeDtypeStruct((T, G * 2 * C), BF16),
        compiler_params=_params("parallel", "arbitrary"),
        name="fourier_channel",
    )(z2d, w0)
    x1 = pq.reshape(B, Na, FFT_INNER, G, 2, C).transpose(0, 3, 4, 1, 2, 5)
    x1 = x1.reshape(B * G, 2 * Na, FFT_INNER * C)
    tn = min(8192, FFT_INNER * C)
    a1 = pl.pallas_call(
        _lmm_body,
        grid=(B * G, (FFT_INNER * C) // tn),
        in_specs=[pl.BlockSpec((2 * Na, 2 * Na), lambda s, j: (0, 0)),
                  pl.BlockSpec((1, 2 * Na, tn), lambda s, j: (s, 0, j))],
        out_specs=pl.BlockSpec((1, 2 * Na, tn), lambda s, j: (s, 0, j)),
        out_shape=jax.ShapeDtypeStruct((B * G, 2 * Na, FFT_INNER * C), BF16),
        compiler_params=_params("parallel", "arbitrary"),
        name="fourier_outer",
    )(r, x1)
    a2 = a1.reshape(B * G, 2, Na, FFT_INNER, C).transpose(0, 2, 1, 3, 4)
    a2 = a2.reshape(B * G, Na, 2 * FFT_INNER, C)
    cb = min(8, Na)
    y = pl.pallas_call(
        _twiddle_body,
        grid=(B * G, Na // cb),
        in_specs=[pl.BlockSpec((cb, FFT_INNER, 2 * FFT_INNER), lambda s, j: (j, 0, 0)),
                  pl.BlockSpec((1, cb, 2 * FFT_INNER, C), lambda s, j: (s, j, 0, 0))],
        out_specs=pl.BlockSpec((1, cb, FFT_INNER, C), lambda s, j: (s, j, 0, 0)),
        out_shape=jax.ShapeDtypeStruct((B * G, Na, FFT_INNER, C), BF16),
        compiler_params=_params("parallel", "arbitrary"),
        name="fourier_inner",
    )(m3, a2)
    y = y.reshape(B, G, Na, FFT_INNER, C).transpose(0, 3, 2, 1, 4)
    return y.reshape(T, G * C)


def _attn_body(slope_ref, shift_ref, lamv_ref, g_ref, qt_ref, ka_ref, q_ref, k_ref, v_ref, o_ref,
               m_ref, l_ref, acc_ref, *, tq, tk, dh, lam_init, fixed_shift):
    h = pl.program_id(1)
    qi = pl.program_id(2)
    ki = pl.program_id(3)
    dv = 2 * dh

    @pl.when(ki == 0)
    def _():
        m_ref[...] = jnp.full(m_ref.shape, NEG_INF, F32)
        l_ref[...] = jnp.zeros(l_ref.shape, F32)
        acc_ref[...] = jnp.zeros(acc_ref.shape, F32)

    slope = slope_ref[h]

    def online_softmax_update(scores):
        v = v_ref[...]
        for c in range(2):
            s = scores[c]
            m_prev = m_ref[c]
            m_new = jnp.maximum(m_prev, jnp.max(s, axis=-1, keepdims=True))
            alpha = jnp.exp(m_prev - m_new)
            p = jnp.exp(s - jnp.tile(m_new, (1, tk // LANES)))
            l_ref[c] = alpha * l_ref[c] + jnp.sum(p, axis=-1, keepdims=True)
            acc_ref[c] = (jnp.tile(alpha, (1, dv // LANES)) * acc_ref[c]
                          + jnp.dot(p.astype(BF16), v, preferred_element_type=F32))
            m_ref[c] = m_new

    def shifted_update(scores):
        v = v_ref[...]
        for c in range(2):
            p = jnp.exp(scores[c])
            part = p[:, 0:LANES]
            for t in range(1, tk // LANES):
                part = part + p[:, t * LANES:(t + 1) * LANES]
            l_ref[c] += part
            acc_ref[c] += jnp.dot(p.astype(BF16), v, preferred_element_type=F32)

    update = shifted_update if fixed_shift else online_softmax_update
    nt = (((1,), (1,)), ((), ()))

    @pl.when(qi != ki)
    def _():
        off = (qi - ki) * tq
        sgn = jnp.where(off > 0, slope, -slope)
        qa = ((qt_ref[0] + off.astype(F32) * qt_ref[1]) * sgn + qt_ref[2]).astype(BF16)
        ka = ka_ref[...]
        q = q_ref[...]
        k = k_ref[...]
        scores = []
        for c in range(2):
            qc = jnp.concatenate([q[:, c * dh:(c + 1) * dh], qa], axis=1)
            kc = jnp.concatenate([k[:, c * dh:(c + 1) * dh], ka], axis=1)
            scores.append(lax.dot_general(qc, kc, nt, preferred_element_type=F32))
        update(scores)

    @pl.when(qi == ki)
    def _():
        row = lax.broadcasted_iota(jnp.int32, (tq, tk), 0)
        col = lax.broadcasted_iota(jnp.int32, (tq, tk), 1)
        bias = jnp.abs(row - col).astype(F32) * (-slope) - shift_ref[0]
        q = q_ref[...]
        k = k_ref[...]
        scores = [lax.dot_general(q[:, c * dh:(c + 1) * dh], k[:, c * dh:(c + 1) * dh], nt,
                                  preferred_element_type=F32) + bias for c in range(2)]
        update(scores)

    @pl.when(ki == pl.num_programs(3) - 1)
    def _():
        lv = lamv_ref[...]
        lam = (jnp.exp(jnp.sum(lv[0:1] * lv[1:2], axis=-1, keepdims=True))
               - jnp.exp(jnp.sum(lv[2:3] * lv[3:4], axis=-1, keepdims=True)) + lam_init)
        if fixed_shift:
            l0 = jnp.sum(l_ref[0], axis=-1, keepdims=True)
            l1 = jnp.sum(l_ref[1], axis=-1, keepdims=True)
        else:
            l0 = jnp.tile(l_ref[0], (1, dv // LANES))
            l1 = jnp.tile(l_ref[1], (1, dv // LANES))
        o = acc_ref[0] / l0 - lam * (acc_ref[1] / l1)
        ms = jnp.mean(o * o, axis=-1, keepdims=True)
        o_ref[...] = (o * lax.rsqrt(ms + EPS) * g_ref[...] * (1.0 - lam_init)).astype(o_ref.dtype)


SHIFT_LANES = (4, 5, 6)


def _split_bf16(x, parts):
    out = []
    for _ in range(parts):
        p = x.astype(BF16).astype(F32)
        out.append(p)
        x = x - p
    return out


def _alibi_templates(tq, tk):
    r = np.arange(tq)
    qt = np.zeros((2, tq, LANES), np.float32)
    qt[0, :, 0] = -(r // 256) * 256
    qt[0, :, 1] = -(r % 256)
    qt[0, :, 2] = 1.0
    qt[0, :, 3] = 1.0
    qt[1, :, 0] = -1.0
    c = np.arange(tk)
    ka = np.zeros((tk, LANES), np.float32)
    ka[:, 0] = 1.0
    ka[:, 1] = 1.0
    ka[:, 2] = c % 256
    ka[:, 3] = (c // 256) * 256
    ka[:, SHIFT_LANES[0]:SHIFT_LANES[-1] + 1] = 1.0
    return jnp.asarray(qt, F32), jnp.asarray(ka, BF16)


def diff_attention(qk2d, v2d, lamv, subln_g, shift, B, S, lam_init, fixed_shift, tile=1024):
    T, W = qk2d.shape
    H = A_HEADS
    dh = W // (4 * H)
    dv = 2 * dh
    tq = tk = min(tile, S)
    nq, nk = S // tq, S // tk
    slopes = 2.0 ** (-8.0 * np.arange(1, H + 1) / H)
    assert np.all(np.log2(slopes) == np.round(np.log2(slopes))) and S <= 8192
    qt, ka = _alibi_templates(tq, tk)
    parts = _split_bf16(shift if fixed_shift else jnp.zeros((), F32), len(SHIFT_LANES))
    lane = lax.broadcasted_iota(jnp.int32, (1, tq, LANES), 2)
    shift_plane = sum(jnp.where(lane == ln, -p, 0.0) for ln, p in zip(SHIFT_LANES, parts))
    qt = jnp.concatenate([qt, shift_plane.astype(F32)], axis=0)
    applied = sum(parts).reshape(1)
    return pl.pallas_call(
        functools.partial(_attn_body, tq=tq, tk=tk, dh=dh, lam_init=lam_init, fixed_shift=fixed_shift),
        grid=(B, H, nq, nk),
        in_specs=[pl.BlockSpec(memory_space=pltpu.SMEM),
                  pl.BlockSpec(memory_space=pltpu.SMEM),
                  pl.BlockSpec((4, dh), lambda b, h, i, j: (0, 0)),
                  pl.BlockSpec((1, dv), lambda b, h, i, j: (0, 0)),
                  pl.BlockSpec((3, tq, LANES), lambda b, h, i, j: (0, 0, 0)),
                  pl.BlockSpec((tk, LANES), lambda b, h, i, j: (0, 0)),
                  pl.BlockSpec((tq, dv), lambda b, h, i, j: (b * nq + i, h)),
                  pl.BlockSpec((tk, dv), lambda b, h, i, j: (b * nk + j, H + h)),
                  pl.BlockSpec((tk, dv), lambda b, h, i, j: (b * nk + j, h))],
        out_specs=pl.BlockSpec((tq, dv), lambda b, h, i, j: (b * nq + i, h)),
        out_shape=jax.ShapeDtypeStruct((T, H * dv), BF16),
        scratch_shapes=[pltpu.VMEM((2, tq, LANES), F32), pltpu.VMEM((2, tq, LANES), F32),
                        pltpu.VMEM((2, tq, dv), F32)],
        compiler_params=_params("parallel", "parallel", "parallel", "arbitrary"),
        name="diff_attention_shifted" if fixed_shift else "diff_attention_online",
    )(jnp.asarray(slopes, F32), applied, lamv, subln_g.reshape(1, dv).astype(F32), qt, ka, qk2d, qk2d, v2d)


def _merge_body(fm_ref, on_ref, gf_ref, ga_ref, wf_ref, wa_ref, o_ref):
    yf = jnp.dot(fm_ref[...], wf_ref[...], preferred_element_type=F32)
    ya = jnp.dot(on_ref[...], wa_ref[...], preferred_element_type=F32)
    o_ref[...] = (gf_ref[...].astype(F32) * yf + ga_ref[...].astype(F32) * ya).astype(o_ref.dtype)


def gated_merge(fm, on, gates, wf, wa, tm=256):
    T, D = on.shape
    tm = min(tm, T)
    return pl.pallas_call(
        _merge_body,
        grid=(T // tm,),
        in_specs=[pl.BlockSpec((tm, fm.shape[1]), lambda i: (i, 0)),
                  pl.BlockSpec((tm, D), lambda i: (i, 0)),
                  pl.BlockSpec((tm, D), lambda i: (i, 0)),
                  pl.BlockSpec((tm, D), lambda i: (i, 1)),
                  pl.BlockSpec(wf.shape, lambda i: (0, 0)),
                  pl.BlockSpec(wa.shape, lambda i: (0, 0))],
        out_specs=pl.BlockSpec((tm, D), lambda i: (i, 0)),
        out_shape=jax.ShapeDtypeStruct((T, D), BF16),
        compiler_params=_params("parallel"),
        name="gated_merge",
    )(fm, on, gates, gates, wf, wa)


def _out_body(mx_ref, x_ref, wo_ref, g_ref, h_ref, hn_ref):
    h = x_ref[...] + jnp.dot(mx_ref[...], wo_ref[...], preferred_element_type=F32)
    h_ref[...] = h
    ms = jnp.mean(h * h, axis=-1, keepdims=True)
    hn_ref[...] = (h * lax.rsqrt(ms + EPS) * g_ref[...]).astype(hn_ref.dtype)


def out_projection(mixed, x2d, wo, g2, tm=512):
    T, D = x2d.shape
    tm = min(tm, T)
    return pl.pallas_call(
        _out_body,
        grid=(T // tm,),
        in_specs=[pl.BlockSpec((tm, D), lambda i: (i, 0)),
                  pl.BlockSpec((tm, D), lambda i: (i, 0)),
                  pl.BlockSpec((D, D), lambda i: (0, 0)),
                  pl.BlockSpec((1, D), lambda i: (0, 0))],
        out_specs=[pl.BlockSpec((tm, D), lambda i: (i, 0)),
                   pl.BlockSpec((tm, D), lambda i: (i, 0))],
        out_shape=[jax.ShapeDtypeStruct((T, D), F32), jax.ShapeDtypeStruct((T, D), BF16)],
        compiler_params=_params("parallel"),
        name="out_projection",
    )(mixed, x2d, wo, g2.reshape(1, D).astype(F32))


def _extract_top(x, dst_ref, n, want_rank=False):
    rank = jnp.full(x.shape, float(n), F32) if want_rank else None
    for r in range(n):
        mx = jnp.max(x, axis=0, keepdims=True)
        dst_ref[pl.ds(r, 1), :] = mx
        hit = x >= mx
        if want_rank:
            rank = jnp.where(hit, float(r), rank)
        x = jnp.where(hit, NEG_INF, x)
    return rank


def _peer_query_body(hn_ref, wq_ref, keys_ref, n1_ref, e1_ref, r2_ref, e2_ref, sv_ref, cs_ref):
    qp = jnp.dot(hn_ref[...], wq_ref[...], preferred_element_type=F32).astype(BF16)
    half = keys_ref.shape[-1]
    nt = (((1,), (1,)), ((), ()))
    for h in range(P_HEADS):
        lo = 2 * h * half
        st1 = lax.dot_general(keys_ref[h, 0], qp[:, lo:lo + half], nt, preferred_element_type=F32)
        st2 = lax.dot_general(keys_ref[h, 1], qp[:, lo + half:lo + 2 * half], nt,
                              preferred_element_type=F32)
        _extract_top(st1, sv_ref.at[0], P_TOPK)
        r2 = _extract_top(st2, sv_ref.at[1], P_TOPK, want_rank=True)
        sv1 = sv_ref[0]
        sv2 = sv_ref[1]
        parts = [sv1[0:1] + sv2]
        for a in range(1, 8):
            parts.append(sv1[a:a + 1] + sv2[0:8])
        parts.append(sv1[8:16] + sv2[0:1])
        _extract_top(jnp.concatenate(parts, axis=0), cs_ref, P_TOPK)
        cs = cs_ref[...]
        tau = cs[P_TOPK - 1:P_TOPK]
        z = jnp.sum(jnp.exp(cs - cs[0:1]), axis=0, keepdims=True)
        n1 = jnp.zeros(st1.shape, F32)
        for a in range(P_TOPK):
            va = sv1[a:a + 1]
            n_a = jnp.sum(jnp.where(va + sv2 >= tau, 1.0, 0.0), axis=0, keepdims=True)
            n1 = jnp.where(st1 == va, n_a, n1)
        e1 = jnp.exp(st1 - sv1[0:1])
        e2 = jnp.exp(st2 - sv2[0:1]) / z
        for tb in range(st1.shape[1] // LANES):
            cols = slice(tb * LANES, (tb + 1) * LANES)
            n1_ref[h, tb] = n1[:, cols]
            e1_ref[h, tb] = e1[:, cols]
            r2_ref[h, tb] = pltpu.bitcast(r2[:, cols].astype(BF16), jnp.uint32)
            e2_ref[h, tb] = pltpu.bitcast(e2[:, cols].astype(BF16), jnp.uint32)


def peer_query(hn, wq, keys, tt=256):
    T, D = hn.shape
    H, _, nk, half = keys.shape
    tt = min(tt, T)
    spec = pl.BlockSpec((H, tt // LANES, nk, LANES), lambda i: (0, i, 0, 0))
    rows = jax.ShapeDtypeStruct((H, T // LANES, nk, LANES), F32)
    pspec = pl.BlockSpec((H, tt // LANES, nk // 2, LANES), lambda i: (0, i, 0, 0))
    tiles = jax.ShapeDtypeStruct((H, T // LANES, nk // 2, LANES), jnp.uint32)
    return pl.pallas_call(
        _peer_query_body,
        grid=(T // tt,),
        in_specs=[pl.BlockSpec((tt, D), lambda i: (i, 0)),
                  pl.BlockSpec(wq.shape, lambda i: (0, 0)),
                  pl.BlockSpec(keys.shape, lambda i: (0, 0, 0, 0))],
        out_specs=[spec, spec, pspec, pspec],
        out_shape=[rows, rows, tiles, tiles],
        scratch_shapes=[pltpu.VMEM((2, P_TOPK, tt), F32), pltpu.VMEM((P_TOPK, tt), F32)],
        compiler_params=_params("parallel"),
        name="peer_query",
    )(hn, wq, keys)


def _peer_dense_body(hn_ref, u_ref, v_ref, n1_ref, e1_ref, r2_ref, e2_ref, h_ref, o_ref,
                     sc0_ref, sc1_ref, a0_ref, a1_ref, *, nk, rows_per_chunk, n_chunks):
    e = pl.program_id(1)

    @pl.when(e == 0)
    def _():
        o_ref[...] = h_ref[...]
        sc1_ref[...] = jnp.zeros(sc1_ref.shape, F32)
        a0_ref[...] = jnp.zeros(a0_ref.shape, BF16)
        a1_ref[...] = jnp.zeros(a1_ref.shape, BF16)

    def stages(sc_w, sc_r, a_w, a_r):
        tt = sc_r.shape[1]
        chunk = jnp.clip(e - 1, 0, n_chunks - 1)

        def up(half):
            rows = slice(half * (tt // 2), (half + 1) * (tt // 2))
            sc_w[:, rows] = lax.dot_general(u_ref[...], hn_ref[rows, :], (((1,), (1,)), ((), ())),
                                            preferred_element_type=F32)

        def gate(tb, ii):
            i = chunk * rows_per_chunk + ii
            def key_row(ref, h):
                row = jnp.broadcast_to(ref[h, tb, pl.ds(i, 1), :], (BF16_ROWS, LANES)).astype(BF16)
                return jnp.tile(row, (nk // BF16_ROWS, 1))

            w = jnp.zeros((nk, LANES), BF16)
            for h in range(P_HEADS):
                w = w + jnp.where(pltpu.bitcast(r2_ref[h, tb], BF16) < key_row(n1_ref, h),
                                  pltpu.bitcast(e2_ref[h, tb], BF16) * key_row(e1_ref, h), 0.0)
            x = sc_r[ii * nk:(ii + 1) * nk, tb * LANES:(tb + 1) * LANES]
            act = 0.5 * x * (1.0 + lax.erf(x * (1.0 / math.sqrt(2.0))))
            a_w[tb * LANES:(tb + 1) * LANES, ii * nk:(ii + 1) * nk] = (act * w.astype(F32)).T.astype(BF16)

        def down(n):
            cols = slice(n * MXU_COLS, (n + 1) * MXU_COLS)
            o_ref[:, cols] += jnp.dot(a_r[...], v_ref[:, cols], preferred_element_type=F32)

        blocks = [(tb, ii) for tb in range(tt // LANES) for ii in range(rows_per_chunk)]
        n_down = o_ref.shape[1] // MXU_COLS
        per_up = (len(blocks) - n_down) // 2
        for half in range(2):
            up(half)
            for blk in blocks[half * per_up:(half + 1) * per_up]:
                gate(*blk)
        for n in range(n_down):
            down(n)
            gate(*blocks[2 * per_up + n])

    @pl.when(e % 2 == 0)
    def _():
        stages(sc0_ref, sc1_ref, a1_ref, a0_ref)

    @pl.when(e % 2 == 1)
    def _():
        stages(sc1_ref, sc0_ref, a0_ref, a1_ref)


def peer_dense(hn, u, v, n1, e1, r2, e2, h1, tt=512, ec=512):
    T, D = hn.shape
    E = u.shape[0]
    H, _, nk, _ = n1.shape
    tt = min(tt, T)
    n_chunks = E // ec
    tok = pl.BlockSpec((H, tt // LANES, nk, LANES), lambda t, e: (0, t, 0, 0))
    ptok = pl.BlockSpec((H, tt // LANES, nk // 2, LANES), lambda t, e: (0, t, 0, 0))
    return pl.pallas_call(
        functools.partial(_peer_dense_body, nk=nk, rows_per_chunk=ec // nk, n_chunks=n_chunks),
        grid=(T // tt, n_chunks + 2),
        in_specs=[pl.BlockSpec((tt, D), lambda t, e: (t, 0)),
                  pl.BlockSpec((ec, D), lambda t, e: (jnp.minimum(e, n_chunks - 1), 0)),
                  pl.BlockSpec((ec, D), lambda t, e: (jnp.clip(e - 2, 0, n_chunks - 1), 0)),
                  tok, tok, ptok, ptok,
                  pl.BlockSpec((tt, D), lambda t, e: (t, 0))],
        out_specs=pl.BlockSpec((tt, D), lambda t, e: (t, 0)),
        out_shape=jax.ShapeDtypeStruct((T, D), F32),
        scratch_shapes=[pltpu.VMEM((ec, tt), F32), pltpu.VMEM((ec, tt), F32),
                        pltpu.VMEM((tt, ec), BF16), pltpu.VMEM((tt, ec), BF16)],
        compiler_params=_params("parallel", "arbitrary"),
        name="peer_dense",
    )(hn, u, v, n1, e1, r2, e2, h1)


def kernel(x, norm1_g, w_in, w_fourier, w_attn, q_norm_g, k_norm_g, lambda_q1, lambda_k1,
           lambda_q2, lambda_k2, subln_g, w_out, norm2_g, w_query, sub_keys, expert_u, expert_v):
    B, S, D = x.shape
    T = B * S
    depth = w_in.shape[0]
    dh = D // (2 * A_HEADS)
    f_width = w_fourier.shape[1]
    qk_width = A_HEADS * 2 * dh
    v_width = w_attn.shape[1]
    o_q = f_width
    o_k = o_q + qk_width
    o_v = o_k + qk_width
    o_g = o_v + v_width
    h = x.reshape(T, D)
    for i in range(depth):
        lam_init = 0.8 - 0.6 * math.exp(-0.3 * i)
        xn = rmsnorm_rows(h, norm1_g[i])
        w_in_b = w_in[i].astype(BF16)
        qk_gain = jnp.concatenate([jnp.tile(q_norm_g[i].astype(F32) * dh ** -0.5, 2 * A_HEADS),
                                   jnp.tile(k_norm_g[i].astype(F32), 2 * A_HEADS)]).reshape(1, 2 * qk_width)
        z = project(xn, w_in_b, 0, f_width, "cast")
        qk = project(xn, w_in_b, o_q, 2 * qk_width, "headnorm", gain=qk_gain, group=dh)
        vv = project(xn, w_in_b, o_v, v_width, "cast")
        gates = project(xn, w_in_b, o_g, 2 * D, "sigmoid")

        fm = fourier_mix(z, B, S)
        lamv = jnp.stack([lambda_q1[i], lambda_k1[i], lambda_q2[i], lambda_k2[i]]).astype(F32)
        bound = (jnp.max(jnp.abs(q_norm_g[i])) * jnp.max(jnp.abs(k_norm_g[i]))).astype(F32) * (1.01 * dh ** 0.5)
        attend = functools.partial(diff_attention, qk, vv, lamv, subln_g[i], bound, B, S, lam_init)
        on = lax.cond(bound <= MAX_SAFE_SHIFT, lambda: attend(True), lambda: attend(False))

        mixed = gated_merge(fm, on, gates, w_fourier[i].astype(BF16), w_attn[i].astype(BF16))
        h1, hn = out_projection(mixed, h, w_out[i].astype(BF16), norm2_g[i])

        n1, e1, r2, e2 = peer_query(hn, w_query[i].astype(BF16), sub_keys[i].astype(BF16))
        h = peer_dense(hn, expert_u[i].astype(BF16), expert_v[i].astype(BF16), n1, e1, r2, e2, h1)
    return h.reshape(B, S, D)
```

```python
import functools
import math

import numpy as np
import jax
import jax.numpy as jnp
from jax import lax
from jax.experimental import pallas as pl
from jax.experimental.pallas import tpu as pltpu

EPS = 1e-6
F_GROUPS = 4
A_HEADS = 8
P_HEADS = 8
P_TOPK = 16
LANES = 128
MXU_COLS = 256
BF16_ROWS = 16
UP_K = 256
MAX_SAFE_SHIFT = 40.0
FFT_INNER = 128
VMEM_LIMIT_BYTES = 56 * 1024 * 1024

F32 = jnp.float32
BF16 = jnp.bfloat16
NEG_INF = float("-inf")


def _params(*semantics):
    return pltpu.CompilerParams(dimension_semantics=semantics, vmem_limit_bytes=VMEM_LIMIT_BYTES)


def _rmsnorm_body(x_ref, g_ref, o_ref):
    x = x_ref[...]
    ms = jnp.mean(x * x, axis=-1, keepdims=True)
    o_ref[...] = (x * lax.rsqrt(ms + EPS) * g_ref[...]).astype(o_ref.dtype)


def rmsnorm_rows(x2d, g, tm=512):
    T, D = x2d.shape
    return pl.pallas_call(
        _rmsnorm_body,
        grid=(T // tm,),
        in_specs=[pl.BlockSpec((tm, D), lambda i: (i, 0)),
                  pl.BlockSpec((1, D), lambda i: (0, 0))],
        out_specs=pl.BlockSpec((tm, D), lambda i: (i, 0)),
        out_shape=jax.ShapeDtypeStruct((T, D), BF16),
        compiler_params=_params("parallel"),
        name="rmsnorm",
    )(x2d, g.reshape(1, D).astype(F32))


def _proj_body(x_ref, w_ref, g_ref, o_ref, *, mode, group):
    acc = jnp.dot(x_ref[...], w_ref[...], preferred_element_type=F32)
    if mode == "cast":
        o_ref[...] = acc.astype(o_ref.dtype)
    elif mode == "sigmoid":
        o_ref[...] = jax.nn.sigmoid(acc).astype(o_ref.dtype)
    else:
        for s in range(acc.shape[1] // group):
            blk = acc[:, s * group:(s + 1) * group]
            ms = jnp.mean(blk * blk, axis=-1, keepdims=True)
            o_ref[:, s * group:(s + 1) * group] = (
                blk * lax.rsqrt(ms + EPS) * g_ref[:, s * group:(s + 1) * group]).astype(o_ref.dtype)


def project(x, w, col0, ncols, mode, gain=None, group=LANES, tm=1024, tn=1024):
    T, K = x.shape
    tm = min(tm, T)
    if gain is None:
        gain = jnp.ones((1, ncols), F32)
    off = col0 // tn
    return pl.pallas_call(
        functools.partial(_proj_body, mode=mode, group=group),
        grid=(T // tm, ncols // tn),
        in_specs=[pl.BlockSpec((tm, K), lambda i, j: (i, 0)),
                  pl.BlockSpec((K, tn), lambda i, j: (0, off + j)),
                  pl.BlockSpec((1, tn), lambda i, j: (0, j))],
        out_specs=pl.BlockSpec((tm, tn), lambda i, j: (i, j)),
        out_shape=jax.ShapeDtypeStruct((T, ncols), BF16),
        compiler_params=_params("parallel", "arbitrary"),
        name="proj_" + mode,
    )(x, w, gain)


def _fourier_constants(S, C):
    Na = S // FFT_INNER
    n2 = np.arange(C, dtype=np.int64)
    ang = 2.0 * np.pi * ((n2[:, None] * n2[None, :]) % C) / C
    w0 = np.concatenate([np.cos(ang), -np.sin(ang)], axis=1)
    a = np.arange(Na, dtype=np.int64)
    phi = 2.0 * np.pi * ((a[:, None] * a[None, :]) % Na) / Na
    r = np.block([[np.cos(phi), np.sin(phi)], [-np.sin(phi), np.cos(phi)]])
    k = np.arange(S, dtype=np.int64)
    b = np.arange(FFT_INNER, dtype=np.int64)
    th = 2.0 * np.pi * ((k[:, None] * b[None, :]) % S) / S
    m = np.concatenate([np.cos(th), np.sin(th)], axis=1) / math.sqrt(S * C)
    m3 = m.reshape(FFT_INNER, Na, 2 * FFT_INNER).transpose(1, 0, 2)
    return (jnp.asarray(w0, BF16), jnp.asarray(r, BF16), jnp.asarray(m3, BF16))


def _mm_body(x_ref, w_ref, o_ref):
    o_ref[...] = jnp.dot(x_ref[...], w_ref[...], preferred_element_type=F32).astype(o_ref.dtype)


def _lmm_body(w_ref, x_ref, o_ref):
    o_ref[0] = jnp.dot(w_ref[...], x_ref[0], preferred_element_type=F32).astype(o_ref.dtype)


def _twiddle_body(m_ref, a_ref, o_ref):
    for ci in range(m_ref.shape[0]):
        o_ref[0, ci] = jnp.dot(m_ref[ci], a_ref[0, ci], preferred_element_type=F32).astype(o_ref.dtype)


def fourier_mix(z2d, B, S):
    T, W = z2d.shape
    G = F_GROUPS
    C = W // G
    Na = S // FFT_INNER
    w0, r, m3 = _fourier_constants(S, C)
    tm = min(1024, T)
    pq = pl.pallas_call(
        _mm_body,
        grid=(T // tm, G),
        in_specs=[pl.BlockSpec((tm, C), lambda i, g: (i, g)),
                  pl.BlockSpec((C, 2 * C), lambda i, g: (0, 0))],
        out_specs=pl.BlockSpec((tm, 2 * C), lambda i, g: (i, g)),
        out_shape=jax.ShapeDtypeStruct((T, G * 2 * C), BF16),
        compiler_params=_params("parallel", "arbitrary"),
        name="fourier_channel",
    )(z2d, w0)
    x1 = pq.reshape(B, Na, FFT_INNER, G, 2, C).transpose(0, 3, 4, 1, 2, 5)
    x1 = x1.reshape(B * G, 2 * Na, FFT_INNER * C)
    tn = min(8192, FFT_INNER * C)
    a1 = pl.pallas_call(
        _lmm_body,
        grid=(B * G, (FFT_INNER * C) // tn),
        in_specs=[pl.BlockSpec((2 * Na, 2 * Na), lambda s, j: (0, 0)),
                  pl.BlockSpec((1, 2 * Na, tn), lambda s, j: (s, 0, j))],
        out_specs=pl.BlockSpec((1, 2 * Na, tn), lambda s, j: (s, 0, j)),
        out_shape=jax.ShapeDtypeStruct((B * G, 2 * Na, FFT_INNER * C), BF16),
        compiler_params=_params("parallel", "arbitrary"),
        name="fourier_outer",
    )(r, x1)
    a2 = a1.reshape(B * G, 2, Na, FFT_INNER, C).transpose(0, 2, 1, 3, 4)
    a2 = a2.reshape(B * G, Na, 2 * FFT_INNER, C)
    cb = min(8, Na)
    y = pl.pallas_call(
        _twiddle_body,
        grid=(B * G, Na // cb),
        in_specs=[pl.BlockSpec((cb, FFT_INNER, 2 * FFT_INNER), lambda s, j: (j, 0, 0)),
                  pl.BlockSpec((1, cb, 2 * FFT_INNER, C), lambda s, j: (s, j, 0, 0))],
        out_specs=pl.BlockSpec((1, cb, FFT_INNER, C), lambda s, j: (s, j, 0, 0)),
        out_shape=jax.ShapeDtypeStruct((B * G, Na, FFT_INNER, C), BF16),
        compiler_params=_params("parallel", "arbitrary"),
        name="fourier_inner",
    )(m3, a2)
    y = y.reshape(B, G, Na, FFT_INNER, C).transpose(0, 3, 2, 1, 4)
    return y.reshape(T, G * C)


def _attn_body(slope_ref, shift_ref, lamv_ref, g_ref, qt_ref, ka_ref, q_ref, k_ref, v_ref, o_ref,
               m_ref, l_ref, acc_ref, *, tq, tk, dh, lam_init, fixed_shift):
    h = pl.program_id(1)
    qi = pl.program_id(2)
    ki = pl.program_id(3)
    dv = 2 * dh

    @pl.when(ki == 0)
    def _():
        m_ref[...] = jnp.full(m_ref.shape, NEG_INF, F32)
        l_ref[...] = jnp.zeros(l_ref.shape, F32)
        acc_ref[...] = jnp.zeros(acc_ref.shape, F32)

    slope = slope_ref[h]

    def online_softmax_update(scores):
        v = v_ref[...]
        for c in range(2):
            s = scores[c]
            m_prev = m_ref[c]
            m_new = jnp.maximum(m_prev, jnp.max(s, axis=-1, keepdims=True))
            alpha = jnp.exp(m_prev - m_new)
            p = jnp.exp(s - jnp.tile(m_new, (1, tk // LANES)))
            l_ref[c] = alpha * l_ref[c] + jnp.sum(p, axis=-1, keepdims=True)
            acc_ref[c] = (jnp.tile(alpha, (1, dv // LANES)) * acc_ref[c]
                          + jnp.dot(p.astype(BF16), v, preferred_element_type=F32))
            m_ref[c] = m_new

    def shifted_update(scores):
        v = v_ref[...]
        for c in range(2):
            p = jnp.exp(scores[c])
            part = p[:, 0:LANES]
            for t in range(1, tk // LANES):
                part = part + p[:, t * LANES:(t + 1) * LANES]
            l_ref[c] += part
            acc_ref[c] += jnp.dot(p.astype(BF16), v, preferred_element_type=F32)

    update = shifted_update if fixed_shift else online_softmax_update
    nt = (((1,), (1,)), ((), ()))

    @pl.when(qi != ki)
    def _():
        off = (qi - ki) * tq
        sgn = jnp.where(off > 0, slope, -slope)
        qa = ((qt_ref[0] + off.astype(F32) * qt_ref[1]) * sgn + qt_ref[2]).astype(BF16)
        ka = ka_ref[...]
        q = q_ref[...]
        k = k_ref[...]
        scores = []
        for c in range(2):
            qc = jnp.concatenate([q[:, c * dh:(c + 1) * dh], qa], axis=1)
            kc = jnp.concatenate([k[:, c * dh:(c + 1) * dh], ka], axis=1)
            scores.append(lax.dot_general(qc, kc, nt, preferred_element_type=F32))
        update(scores)

    @pl.when(qi == ki)
    def _():
        row = lax.broadcasted_iota(jnp.int32, (tq, tk), 0)
        col = lax.broadcasted_iota(jnp.int32, (tq, tk), 1)
        bias = jnp.abs(row - col).astype(F32) * (-slope) - shift_ref[0]
        q = q_ref[...]
        k = k_ref[...]
        scores = [lax.dot_general(q[:, c * dh:(c + 1) * dh], k[:, c * dh:(c + 1) * dh], nt,
                                  preferred_element_type=F32) + bias for c in range(2)]
        update(scores)

    @pl.when(ki == pl.num_programs(3) - 1)
    def _():
        lv = lamv_ref[...]
        lam = (jnp.exp(jnp.sum(lv[0:1] * lv[1:2], axis=-1, keepdims=True))
               - jnp.exp(jnp.sum(lv[2:3] * lv[3:4], axis=-1, keepdims=True)) + lam_init)
        if fixed_shift:
            l0 = jnp.sum(l_ref[0], axis=-1, keepdims=True)
            l1 = jnp.sum(l_ref[1], axis=-1, keepdims=True)
        else:
            l0 = jnp.tile(l_ref[0], (1, dv // LANES))
            l1 = jnp.tile(l_ref[1], (1, dv // LANES))
        o = acc_ref[0] / l0 - lam * (acc_ref[1] / l1)
        ms = jnp.mean(o * o, axis=-1, keepdims=True)
        o_ref[...] = (o * lax.rsqrt(ms + EPS) * g_ref[...] * (1.0 - lam_init)).astype(o_ref.dtype)


SHIFT_LANES = (4, 5, 6)


def _split_bf16(x, parts):
    out = []
    for _ in range(parts):
        p = x.astype(BF16).astype(F32)
        out.append(p)
        x = x - p
    return out


def _alibi_templates(tq, tk):
    r = np.arange(tq)
    qt = np.zeros((2, tq, LANES), np.float32)
    qt[0, :, 0] = -(r // 256) * 256
    qt[0, :, 1] = -(r % 256)
    qt[0, :, 2] = 1.0
    qt[0, :, 3] = 1.0
    qt[1, :, 0] = -1.0
    c = np.arange(tk)
    ka = np.zeros((tk, LANES), np.float32)
    ka[:, 0] = 1.0
    ka[:, 1] = 1.0
    ka[:, 2] = c % 256
    ka[:, 3] = (c // 256) * 256
    ka[:, SHIFT_LANES[0]:SHIFT_LANES[-1] + 1] = 1.0
    return jnp.asarray(qt, F32), jnp.asarray(ka, BF16)


def diff_attention(qk2d, v2d, lamv, subln_g, shift, B, S, lam_init, fixed_shift, tile=1024):
    T, W = qk2d.shape
    H = A_HEADS
    dh = W // (4 * H)
    dv = 2 * dh
    tq = tk = min(tile, S)
    nq, nk = S // tq, S // tk
    slopes = 2.0 ** (-8.0 * np.arange(1, H + 1) / H)
    assert np.all(np.log2(slopes) == np.round(np.log2(slopes))) and S <= 8192
    qt, ka = _alibi_templates(tq, tk)
    parts = _split_bf16(shift if fixed_shift else jnp.zeros((), F32), len(SHIFT_LANES))
    lane = lax.broadcasted_iota(jnp.int32, (1, tq, LANES), 2)
    shift_plane = sum(jnp.where(lane == ln, -p, 0.0) for ln, p in zip(SHIFT_LANES, parts))
    qt = jnp.concatenate([qt, shift_plane.astype(F32)], axis=0)
    applied = sum(parts).reshape(1)
    return pl.pallas_call(
        functools.partial(_attn_body, tq=tq, tk=tk, dh=dh, lam_init=lam_init, fixed_shift=fixed_shift),
        grid=(B, H, nq, nk),
        in_specs=[pl.BlockSpec(memory_space=pltpu.SMEM),
                  pl.BlockSpec(memory_space=pltpu.SMEM),
                  pl.BlockSpec((4, dh), lambda b, h, i, j: (0, 0)),
                  pl.BlockSpec((1, dv), lambda b, h, i, j: (0, 0)),
                  pl.BlockSpec((3, tq, LANES), lambda b, h, i, j: (0, 0, 0)),
                  pl.BlockSpec((tk, LANES), lambda b, h, i, j: (0, 0)),
                  pl.BlockSpec((tq, dv), lambda b, h, i, j: (b * nq + i, h)),
                  pl.BlockSpec((tk, dv), lambda b, h, i, j: (b * nk + j, H + h)),
                  pl.BlockSpec((tk, dv), lambda b, h, i, j: (b * nk + j, h))],
        out_specs=pl.BlockSpec((tq, dv), lambda b, h, i, j: (b * nq + i, h)),
        out_shape=jax.ShapeDtypeStruct((T, H * dv), BF16),
        scratch_shapes=[pltpu.VMEM((2, tq, LANES), F32), pltpu.VMEM((2, tq, LANES), F32),
                        pltpu.VMEM((2, tq, dv), F32)],
        compiler_params=_params("parallel", "parallel", "parallel", "arbitrary"),
        name="diff_attention_shifted" if fixed_shift else "diff_attention_online",
    )(jnp.asarray(slopes, F32), applied, lamv, subln_g.reshape(1, dv).astype(F32), qt, ka, qk2d, qk2d, v2d)


def _merge_body(fm_ref, on_ref, gf_ref, ga_ref, wf_ref, wa_ref, o_ref):
    yf = jnp.dot(fm_ref[...], wf_ref[...], preferred_element_type=F32)
    ya = jnp.dot(on_ref[...], wa_ref[...], preferred_element_type=F32)
    o_ref[...] = (gf_ref[...].astype(F32) * yf + ga_ref[...].astype(F32) * ya).astype(o_ref.dtype)


def gated_merge(fm, on, gates, wf, wa, tm=256):
    T, D = on.shape
    tm = min(tm, T)
    return pl.pallas_call(
        _merge_body,
        grid=(T // tm,),
        in_specs=[pl.BlockSpec((tm, fm.shape[1]), lambda i: (i, 0)),
                  pl.BlockSpec((tm, D), lambda i: (i, 0)),
                  pl.BlockSpec((tm, D), lambda i: (i, 0)),
                  pl.BlockSpec((tm, D), lambda i: (i, 1)),
                  pl.BlockSpec(wf.shape, lambda i: (0, 0)),
                  pl.BlockSpec(wa.shape, lambda i: (0, 0))],
        out_specs=pl.BlockSpec((tm, D), lambda i: (i, 0)),
        out_shape=jax.ShapeDtypeStruct((T, D), BF16),
        compiler_params=_params("parallel"),
        name="gated_merge",
    )(fm, on, gates, gates, wf, wa)


def _out_body(mx_ref, x_ref, wo_ref, g_ref, h_ref, hn_ref, hnt_ref):
    h = x_ref[...] + jnp.dot(mx_ref[...], wo_ref[...], preferred_element_type=F32)
    h_ref[...] = h
    ms = jnp.mean(h * h, axis=-1, keepdims=True)
    hn = h * lax.rsqrt(ms + EPS) * g_ref[...]
    hn_ref[...] = hn.astype(hn_ref.dtype)
    hnt_ref[...] = hn.T.astype(hnt_ref.dtype)


def out_projection(mixed, x2d, wo, g2, tm=512):
    T, D = x2d.shape
    tm = min(tm, T)
    return pl.pallas_call(
        _out_body,
        grid=(T // tm,),
        in_specs=[pl.BlockSpec((tm, D), lambda i: (i, 0)),
                  pl.BlockSpec((tm, D), lambda i: (i, 0)),
                  pl.BlockSpec((D, D), lambda i: (0, 0)),
                  pl.BlockSpec((1, D), lambda i: (0, 0))],
        out_specs=[pl.BlockSpec((tm, D), lambda i: (i, 0)),
                   pl.BlockSpec((tm, D), lambda i: (i, 0)),
                   pl.BlockSpec((D, tm), lambda i: (0, i))],
        out_shape=[jax.ShapeDtypeStruct((T, D), F32), jax.ShapeDtypeStruct((T, D), BF16),
                   jax.ShapeDtypeStruct((D, T), BF16)],
        compiler_params=_params("parallel"),
        name="out_projection",
    )(mixed, x2d, wo, g2.reshape(1, D).astype(F32))


def _extract_top(x, dst_ref, n, want_rank=False):
    rank = jnp.full(x.shape, float(n), F32) if want_rank else None
    for r in range(n):
        mx = jnp.max(x, axis=0, keepdims=True)
        dst_ref[pl.ds(r, 1), :] = mx
        hit = x >= mx
        if want_rank:
            rank = jnp.where(hit, float(r), rank)
        x = jnp.where(hit, NEG_INF, x)
    return rank


def _peer_query_body(hn_ref, wq_ref, keys_ref, n1_ref, e1_ref, r2_ref, e2_ref, sv_ref, cs_ref):
    qp = jnp.dot(hn_ref[...], wq_ref[...], preferred_element_type=F32).astype(BF16)
    half = keys_ref.shape[-1]
    nt = (((1,), (1,)), ((), ()))
    for h in range(P_HEADS):
        lo = 2 * h * half
        st1 = lax.dot_general(keys_ref[h, 0], qp[:, lo:lo + half], nt, preferred_element_type=F32)
        st2 = lax.dot_general(keys_ref[h, 1], qp[:, lo + half:lo + 2 * half], nt,
                              preferred_element_type=F32)
        _extract_top(st1, sv_ref.at[0], P_TOPK)
        r2 = _extract_top(st2, sv_ref.at[1], P_TOPK, want_rank=True)
        sv1 = sv_ref[0]
        sv2 = sv_ref[1]
        parts = [sv1[0:1] + sv2]
        for a in range(1, 8):
            parts.append(sv1[a:a + 1] + sv2[0:8])
        parts.append(sv1[8:16] + sv2[0:1])
        _extract_top(jnp.concatenate(parts, axis=0), cs_ref, P_TOPK)
        cs = cs_ref[...]
        tau = cs[P_TOPK - 1:P_TOPK]
        z = jnp.sum(jnp.exp(cs - cs[0:1]), axis=0, keepdims=True)
        n1 = jnp.zeros(st1.shape, F32)
        for a in range(P_TOPK):
            va = sv1[a:a + 1]
            n_a = jnp.sum(jnp.where(va + sv2 >= tau, 1.0, 0.0), axis=0, keepdims=True)
            n1 = jnp.where(st1 == va, n_a, n1)
        e1 = jnp.exp(st1 - sv1[0:1])
        e2 = jnp.exp(st2 - sv2[0:1]) / z
        for tb in range(st1.shape[1] // LANES):
            cols = slice(tb * LANES, (tb + 1) * LANES)
            n1_ref[h, tb] = n1[:, cols]
            e1_ref[h, tb] = e1[:, cols]
            r2_ref[h, tb] = pltpu.bitcast(r2[:, cols].astype(BF16), jnp.uint32)
            e2_ref[h, tb] = pltpu.bitcast(e2[:, cols].astype(BF16), jnp.uint32)


def peer_query(hn, wq, keys, tt=256):
    T, D = hn.shape
    H, _, nk, half = keys.shape
    tt = min(tt, T)
    spec = pl.BlockSpec((H, tt // LANES, nk, LANES), lambda i: (0, i, 0, 0))
    rows = jax.ShapeDtypeStruct((H, T // LANES, nk, LANES), F32)
    pspec = pl.BlockSpec((H, tt // LANES, nk // 2, LANES), lambda i: (0, i, 0, 0))
    tiles = jax.ShapeDtypeStruct((H, T // LANES, nk // 2, LANES), jnp.uint32)
    return pl.pallas_call(
        _peer_query_body,
        grid=(T // tt,),
        in_specs=[pl.BlockSpec((tt, D), lambda i: (i, 0)),
                  pl.BlockSpec(wq.shape, lambda i: (0, 0)),
                  pl.BlockSpec(keys.shape, lambda i: (0, 0, 0, 0))],
        out_specs=[spec, spec, pspec, pspec],
        out_shape=[rows, rows, tiles, tiles],
        scratch_shapes=[pltpu.VMEM((2, P_TOPK, tt), F32), pltpu.VMEM((P_TOPK, tt), F32)],
        compiler_params=_params("parallel"),
        name="peer_query",
    )(hn, wq, keys)


def _peer_dense_body(hnt_ref, u_ref, v_ref, n1_ref, e1_ref, r2_ref, e2_ref, h_ref, o_ref,
                     sc0_ref, sc1_ref, a0_ref, a1_ref, *, nk, rows_per_chunk, n_chunks):
    e = pl.program_id(1)

    @pl.when(e == 0)
    def _():
        o_ref[...] = h_ref[...]
        sc1_ref[...] = jnp.zeros(sc1_ref.shape, F32)
        a0_ref[...] = jnp.zeros(a0_ref.shape, BF16)
        a1_ref[...] = jnp.zeros(a1_ref.shape, BF16)

    def stages(sc_w, sc_r, a_w, a_r):
        tt = sc_r.shape[1]
        chunk = jnp.clip(e - 1, 0, n_chunks - 1)

        def anchored(lhs, dep):
            if dep is None:
                return lhs
            z = jnp.tile(dep, (BF16_ROWS // dep.shape[0], lhs.shape[1] // LANES)).astype(BF16)
            return jnp.concatenate([lhs[0:BF16_ROWS, :] + z, lhs[BF16_ROWS:, :]], axis=0)

        def up_piece(half, kb, dep):
            rows = slice(half * (tt // 2), (half + 1) * (tt // 2))
            ks = slice(kb * UP_K, (kb + 1) * UP_K)
            return jnp.dot(anchored(u_ref[:, ks], dep), hnt_ref[ks, rows], preferred_element_type=F32)

        def gate(tb, ii):
            i = chunk * rows_per_chunk + ii
            def key_row(ref, h):
                row = jnp.broadcast_to(ref[h, tb, pl.ds(i, 1), :], (BF16_ROWS, LANES)).astype(BF16)
                return jnp.tile(row, (nk // BF16_ROWS, 1))

            w = jnp.zeros((nk, LANES), BF16)
            for h in range(P_HEADS):
                w = w + jnp.where(pltpu.bitcast(r2_ref[h, tb], BF16) < key_row(n1_ref, h),
                                  pltpu.bitcast(e2_ref[h, tb], BF16) * key_row(e1_ref, h), 0.0)
            x = sc_r[ii * nk:(ii + 1) * nk, tb * LANES:(tb + 1) * LANES]
            act = 0.5 * x * (1.0 + lax.erf(x * (1.0 / math.sqrt(2.0))))
            prod = (act * w.astype(F32)).T
            a_w[tb * LANES:(tb + 1) * LANES, ii * nk:(ii + 1) * nk] = prod.astype(BF16)
            bits = pltpu.bitcast(prod[0:8, :], jnp.uint32)
            bits = lax.shift_right_logical(lax.shift_right_logical(bits, jnp.uint32(16)), jnp.uint32(16))
            return pltpu.bitcast(bits, F32)

        def down(n, dep):
            cols = slice(n * MXU_COLS, (n + 1) * MXU_COLS)
            o_ref[:, cols] += jnp.dot(anchored(a_r[...], dep), v_ref[:, cols], preferred_element_type=F32)

        blocks = [(tb, ii) for tb in range(tt // LANES) for ii in range(rows_per_chunk)]
        n_down = o_ref.shape[1] // MXU_COLS
        n_kb = u_ref.shape[1] // UP_K
        per_gate = 2 * n_kb // (len(blocks) - n_down)
        dep = None
        g = 0
        for half in range(2):
            acc = None
            for kb in range(n_kb):
                piece = up_piece(half, kb, dep)
                acc = piece if acc is None else acc + piece
                if (half * n_kb + kb + 1) % per_gate == 0:
                    dep = gate(*blocks[g])
                    g += 1
            sc_w[:, half * (tt // 2):(half + 1) * (tt // 2)] = acc
        for n in range(n_down):
            down(n, dep)
            dep = gate(*blocks[g + n])

    @pl.when(e % 2 == 0)
    def _():
        stages(sc0_ref, sc1_ref, a1_ref, a0_ref)

    @pl.when(e % 2 == 1)
    def _():
        stages(sc1_ref, sc0_ref, a0_ref, a1_ref)


def peer_dense(hnt, u, v, n1, e1, r2, e2, h1, tt=512, ec=512):
    D, T = hnt.shape
    E = u.shape[0]
    H, _, nk, _ = n1.shape
    tt = min(tt, T)
    n_chunks = E // ec
    tok = pl.BlockSpec((H, tt // LANES, nk, LANES), lambda t, e: (0, t, 0, 0))
    ptok = pl.BlockSpec((H, tt // LANES, nk // 2, LANES), lambda t, e: (0, t, 0, 0))
    return pl.pallas_call(
        functools.partial(_peer_dense_body, nk=nk, rows_per_chunk=ec // nk, n_chunks=n_chunks),
        grid=(T // tt, n_chunks + 2),
        in_specs=[pl.BlockSpec((D, tt), lambda t, e: (0, t)),
                  pl.BlockSpec((ec, D), lambda t, e: (jnp.minimum(e, n_chunks - 1), 0)),
                  pl.BlockSpec((ec, D), lambda t, e: (jnp.clip(e - 2, 0, n_chunks - 1), 0)),
                  tok, tok, ptok, ptok,
                  pl.BlockSpec((tt, D), lambda t, e: (t, 0))],
        out_specs=pl.BlockSpec((tt, D), lambda t, e: (t, 0)),
        out_shape=jax.ShapeDtypeStruct((T, D), F32),
        scratch_shapes=[pltpu.VMEM((ec, tt), F32), pltpu.VMEM((ec, tt), F32),
                        pltpu.VMEM((tt, ec), BF16), pltpu.VMEM((tt, ec), BF16)],
        compiler_params=_params("parallel", "arbitrary"),
        name="peer_dense",
    )(hnt, u, v, n1, e1, r2, e2, h1)


def kernel(x, norm1_g, w_in, w_fourier, w_attn, q_norm_g, k_norm_g, lambda_q1, lambda_k1,
           lambda_q2, lambda_k2, subln_g, w_out, norm2_g, w_query, sub_keys, expert_u, expert_v):
    B, S, D = x.shape
    T = B * S
    depth = w_in.shape[0]
    dh = D // (2 * A_HEADS)
    f_width = w_fourier.shape[1]
    qk_width = A_HEADS * 2 * dh
    v_width = w_attn.shape[1]
    o_q = f_width
    o_k = o_q + qk_width
    o_v = o_k + qk_width
    o_g = o_v + v_width
    h = x.reshape(T, D)
    for i in range(depth):
        lam_init = 0.8 - 0.6 * math.exp(-0.3 * i)
        xn = rmsnorm_rows(h, norm1_g[i])
        w_in_b = w_in[i].astype(BF16)
        qk_gain = jnp.concatenate([jnp.tile(q_norm_g[i].astype(F32) * dh ** -0.5, 2 * A_HEADS),
                                   jnp.tile(k_norm_g[i].astype(F32), 2 * A_HEADS)]).reshape(1, 2 * qk_width)
        z = project(xn, w_in_b, 0, f_width, "cast")
        qk = project(xn, w_in_b, o_q, 2 * qk_width, "headnorm", gain=qk_gain, group=dh)
        vv = project(xn, w_in_b, o_v, v_width, "cast")
        gates = project(xn, w_in_b, o_g, 2 * D, "sigmoid")

        fm = fourier_mix(z, B, S)
        lamv = jnp.stack([lambda_q1[i], lambda_k1[i], lambda_q2[i], lambda_k2[i]]).astype(F32)
        bound = (jnp.max(jnp.abs(q_norm_g[i])) * jnp.max(jnp.abs(k_norm_g[i]))).astype(F32) * (1.01 * dh ** 0.5)
        attend = functools.partial(diff_attention, qk, vv, lamv, subln_g[i], bound, B, S, lam_init)
        on = lax.cond(bound <= MAX_SAFE_SHIFT, lambda: attend(True), lambda: attend(False))

        mixed = gated_merge(fm, on, gates, w_fourier[i].astype(BF16), w_attn[i].astype(BF16))
        h1, hn, hnt = out_projection(mixed, h, w_out[i].astype(BF16), norm2_g[i])

        n1, e1, r2, e2 = peer_query(hn, w_query[i].astype(BF16), sub_keys[i].astype(BF16))
        h = peer_dense(hnt, expert_u[i].astype(BF16), expert_v[i].astype(BF16), n1, e1, r2, e2, h1)
    return h.reshape(B, S, D)
```

```python
import functools
import math

import numpy as np
import jax
import jax.numpy as jnp
from jax import lax
from jax.experimental import pallas as pl
from jax.experimental.pallas import tpu as pltpu

EPS = 1e-6
F_GROUPS = 4
A_HEADS = 8
P_HEADS = 8
P_TOPK = 16
LANES = 128
SUBLANES = 8
MXU_COLS = 256
BF16_ROWS = 16
UP_K = 256
MAX_SAFE_SHIFT = 40.0
FFT_INNER = 128
VMEM_LIMIT_BYTES = 56 * 1024 * 1024

F32 = jnp.float32
BF16 = jnp.bfloat16
NEG_INF = float("-inf")


def _params(*semantics):
    return pltpu.CompilerParams(dimension_semantics=semantics, vmem_limit_bytes=VMEM_LIMIT_BYTES)


def _rmsnorm_body(x_ref, g_ref, o_ref):
    x = x_ref[...]
    ms = jnp.mean(x * x, axis=-1, keepdims=True)
    o_ref[...] = (x * lax.rsqrt(ms + EPS) * g_ref[...]).astype(o_ref.dtype)


def rmsnorm_rows(x2d, g, tm=512):
    T, D = x2d.shape
    return pl.pallas_call(
        _rmsnorm_body,
        grid=(T // tm,),
        in_specs=[pl.BlockSpec((tm, D), lambda i: (i, 0)),
                  pl.BlockSpec((1, D), lambda i: (0, 0))],
        out_specs=pl.BlockSpec((tm, D), lambda i: (i, 0)),
        out_shape=jax.ShapeDtypeStruct((T, D), BF16),
        compiler_params=_params("parallel"),
        name="rmsnorm",
    )(x2d, g.reshape(1, D).astype(F32))


def _proj_body(x_ref, w_ref, g_ref, o_ref, *, mode, group):
    acc = jnp.dot(x_ref[...], w_ref[...], preferred_element_type=F32)
    if mode == "cast":
        o_ref[...] = acc.astype(o_ref.dtype)
    elif mode == "sigmoid":
        o_ref[...] = jax.nn.sigmoid(acc).astype(o_ref.dtype)
    else:
        for s in range(acc.shape[1] // group):
            blk = acc[:, s * group:(s + 1) * group]
            ms = jnp.mean(blk * blk, axis=-1, keepdims=True)
            o_ref[:, s * group:(s + 1) * group] = (
                blk * lax.rsqrt(ms + EPS) * g_ref[:, s * group:(s + 1) * group]).astype(o_ref.dtype)


def project(x, w, col0, ncols, mode, gain=None, group=LANES, out_dtype=BF16, tm=1024, tn=1024):
    T, K = x.shape
    tm = min(tm, T)
    if gain is None:
        gain = jnp.ones((1, ncols), F32)
    off = col0 // tn
    return pl.pallas_call(
        functools.partial(_proj_body, mode=mode, group=group),
        grid=(T // tm, ncols // tn),
        in_specs=[pl.BlockSpec((tm, K), lambda i, j: (i, 0)),
                  pl.BlockSpec((K, tn), lambda i, j: (0, off + j)),
                  pl.BlockSpec((1, tn), lambda i, j: (0, j))],
        out_specs=pl.BlockSpec((tm, tn), lambda i, j: (i, j)),
        out_shape=jax.ShapeDtypeStruct((T, ncols), out_dtype),
        compiler_params=_params("parallel", "arbitrary"),
        name="proj_" + mode,
    )(x, w, gain)


def _fourier_constants(S, C):
    Na = S // FFT_INNER
    eye = np.eye(SUBLANES)
    n2 = np.arange(C, dtype=np.int64)
    ang = 2.0 * np.pi * ((n2[:, None] * n2[None, :]) % C) / C
    w0 = np.concatenate([np.cos(ang), -np.sin(ang)], axis=1)
    a = np.arange(Na, dtype=np.int64)
    phi = 2.0 * np.pi * ((a[:, None] * a[None, :]) % Na) / Na
    r = np.block([[np.cos(phi), np.sin(phi)], [-np.sin(phi), np.cos(phi)]])
    r4 = r.reshape(2, Na, 2, Na)
    rk = np.einsum('qcpa,xy->qxcpay', r4, eye).reshape(2 * SUBLANES * Na, 2 * Na * SUBLANES)
    d = np.arange(FFT_INNER, dtype=np.int64)
    psi = 2.0 * np.pi * ((d[:, None] * d[None, :]) % FFT_INNER) / FFT_INNER
    f2 = np.stack([np.cos(psi), np.sin(psi)], axis=1) / math.sqrt(S * C)
    kf = np.einsum('drb,xy->dxrby', f2, eye).reshape(FFT_INNER * SUBLANES, 2 * FFT_INNER * SUBLANES)
    return (jnp.asarray(w0, BF16), jnp.asarray(rk, BF16), jnp.asarray(kf, BF16))


def _twiddles(S):
    Na = S // FFT_INNER
    b = lax.broadcasted_iota(jnp.int32, (FFT_INNER, Na), 0)
    c = lax.broadcasted_iota(jnp.int32, (FFT_INNER, Na), 1)
    th = ((b * c) % S).astype(F32) * (2.0 * math.pi / S)
    shape = (FFT_INNER, Na, LANES)
    return (jnp.broadcast_to(jnp.cos(th)[:, :, None], shape), jnp.broadcast_to(jnp.sin(th)[:, :, None], shape))


def _fourier_outer_body(z_ref, w0_ref, rk_ref, o_ref, *, groups):
    na, sub, w = z_ref.shape
    c = w // groups
    x = z_ref[...].reshape(na * sub, w).astype(BF16)
    ps, qs = [], []
    for g in range(groups):
        pq = jnp.dot(x[:, g * c:(g + 1) * c], w0_ref[...], preferred_element_type=F32)
        ps.append(pq[:, :c])
        qs.append(pq[:, c:])
    x1 = jnp.concatenate([jnp.concatenate(ps, axis=1), jnp.concatenate(qs, axis=1)], axis=0)
    a = jnp.dot(rk_ref[...], x1.astype(BF16), preferred_element_type=F32)
    o_ref[...] = a.reshape(2, sub, na, w)


def _fourier_inner_body(a_ref, tc_ref, ts_ref, kf_ref, o_ref):
    _, nb, sub, c = a_ref.shape
    ar = a_ref[0].reshape(nb * sub, c)
    ai = a_ref[1].reshape(nb * sub, c)
    tc = jnp.tile(tc_ref[...].reshape(nb * sub, LANES), (1, c // LANES))
    ts = jnp.tile(ts_ref[...].reshape(nb * sub, LANES), (1, c // LANES))
    x3 = jnp.concatenate([ar * tc + ai * ts, ai * tc - ar * ts], axis=0).astype(BF16)
    y = jnp.dot(kf_ref[...], x3, preferred_element_type=F32)
    o_ref[...] = y.reshape(nb, sub, c)


def fourier_mix(z2d, B, S):
    T, W = z2d.shape
    G = F_GROUPS
    C = W // G
    Na = S // FFT_INNER
    assert Na % SUBLANES == 0
    w0, rk, kf = _fourier_constants(S, C)
    tc, ts = _twiddles(S)
    a5 = pl.pallas_call(
        functools.partial(_fourier_outer_body, groups=G),
        grid=(B, FFT_INNER // SUBLANES),
        in_specs=[pl.BlockSpec((None, Na, SUBLANES, W), lambda b, j: (b, 0, j, 0)),
                  pl.BlockSpec(w0.shape, lambda b, j: (0, 0)),
                  pl.BlockSpec(rk.shape, lambda b, j: (0, 0))],
        out_specs=pl.BlockSpec((None, 2, SUBLANES, Na, W), lambda b, j: (b, 0, j, 0, 0)),
        out_shape=jax.ShapeDtypeStruct((B, 2, FFT_INNER, Na, W), F32),
        compiler_params=_params("parallel", "arbitrary"),
        name="fourier_outer",
    )(z2d.reshape(B, Na, FFT_INNER, W), w0, rk)
    y = pl.pallas_call(
        _fourier_inner_body,
        grid=(B, Na // SUBLANES, G),
        in_specs=[pl.BlockSpec((None, 2, FFT_INNER, SUBLANES, C), lambda b, j, g: (b, 0, 0, j, g)),
                  pl.BlockSpec((FFT_INNER, SUBLANES, LANES), lambda b, j, g: (0, j, 0)),
                  pl.BlockSpec((FFT_INNER, SUBLANES, LANES), lambda b, j, g: (0, j, 0)),
                  pl.BlockSpec(kf.shape, lambda b, j, g: (0, 0))],
        out_specs=pl.BlockSpec((None, FFT_INNER, SUBLANES, C), lambda b, j, g: (b, 0, j, g)),
        out_shape=jax.ShapeDtypeStruct((B, FFT_INNER, Na, W), F32),
        compiler_params=_params("parallel", "parallel", "arbitrary"),
        name="fourier_inner",
    )(a5, tc, ts, kf)
    return y.reshape(T, W)


def _attn_body(slope_ref, shift_ref, lamv_ref, g_ref, qt_ref, ka_ref, q_ref, k_ref, v_ref, o_ref,
               m_ref, l_ref, acc_ref, *, tq, tk, dh, lam_init, fixed_shift):
    h = pl.program_id(1)
    qi = pl.program_id(2)
    ki = pl.program_id(3)
    dv = 2 * dh

    @pl.when(ki == 0)
    def _():
        m_ref[...] = jnp.full(m_ref.shape, NEG_INF, F32)
        l_ref[...] = jnp.zeros(l_ref.shape, F32)
        acc_ref[...] = jnp.zeros(acc_ref.shape, F32)

    slope = slope_ref[h]

    def online_softmax_update(scores):
        v = v_ref[...]
        for c in range(2):
            s = scores[c]
            m_prev = m_ref[c]
            m_new = jnp.maximum(m_prev, jnp.max(s, axis=-1, keepdims=True))
            alpha = jnp.exp(m_prev - m_new)
            p = jnp.exp(s - jnp.tile(m_new, (1, tk // LANES)))
            l_ref[c] = alpha * l_ref[c] + jnp.sum(p, axis=-1, keepdims=True)
            acc_ref[c] = (jnp.tile(alpha, (1, dv // LANES)) * acc_ref[c]
                          + jnp.dot(p.astype(BF16), v, preferred_element_type=F32))
            m_ref[c] = m_new

    def shifted_update(scores):
        v = v_ref[...]
        for c in range(2):
            p = jnp.exp(scores[c])
            part = p[:, 0:LANES]
            for t in range(1, tk // LANES):
                part = part + p[:, t * LANES:(t + 1) * LANES]
            l_ref[c] += part
            acc_ref[c] += jnp.dot(p.astype(BF16), v, preferred_element_type=F32)

    update = shifted_update if fixed_shift else online_softmax_update
    nt = (((1,), (1,)), ((), ()))

    @pl.when(qi != ki)
    def _():
        off = (qi - ki) * tq
        sgn = jnp.where(off > 0, slope, -slope)
        qa = ((qt_ref[0] + off.astype(F32) * qt_ref[1]) * sgn + qt_ref[2]).astype(BF16)
        ka = ka_ref[...]
        q = q_ref[...]
        k = k_ref[...]
        scores = []
        for c in range(2):
            qc = jnp.concatenate([q[:, c * dh:(c + 1) * dh], qa], axis=1)
            kc = jnp.concatenate([k[:, c * dh:(c + 1) * dh], ka], axis=1)
            scores.append(lax.dot_general(qc, kc, nt, preferred_element_type=F32))
        update(scores)

    @pl.when(qi == ki)
    def _():
        row = lax.broadcasted_iota(jnp.int32, (tq, tk), 0)
        col = lax.broadcasted_iota(jnp.int32, (tq, tk), 1)
        bias = jnp.abs(row - col).astype(F32) * (-slope) - shift_ref[0]
        q = q_ref[...]
        k = k_ref[...]
        scores = [lax.dot_general(q[:, c * dh:(c + 1) * dh], k[:, c * dh:(c + 1) * dh], nt,
                                  preferred_element_type=F32) + bias for c in range(2)]
        update(scores)

    @pl.when(ki == pl.num_programs(3) - 1)
    def _():
        lv = lamv_ref[...]
        lam = (jnp.exp(jnp.sum(lv[0:1] * lv[1:2], axis=-1, keepdims=True))
               - jnp.exp(jnp.sum(lv[2:3] * lv[3:4], axis=-1, keepdims=True)) + lam_init)
        if fixed_shift:
            l0 = jnp.sum(l_ref[0], axis=-1, keepdims=True)
            l1 = jnp.sum(l_ref[1], axis=-1, keepdims=True)
        else:
            l0 = jnp.tile(l_ref[0], (1, dv // LANES))
            l1 = jnp.tile(l_ref[1], (1, dv // LANES))
        o = acc_ref[0] / l0 - lam * (acc_ref[1] / l1)
        ms = jnp.mean(o * o, axis=-1, keepdims=True)
        o_ref[...] = (o * lax.rsqrt(ms + EPS) * g_ref[...] * (1.0 - lam_init)).astype(o_ref.dtype)


SHIFT_LANES = (4, 5, 6)


def _split_bf16(x, parts):
    out = []
    for _ in range(parts):
        p = x.astype(BF16).astype(F32)
        out.append(p)
        x = x - p
    return out


def _alibi_templates(tq, tk):
    r = np.arange(tq)
    qt = np.zeros((2, tq, LANES), np.float32)
    qt[0, :, 0] = -(r // 256) * 256
    qt[0, :, 1] = -(r % 256)
    qt[0, :, 2] = 1.0
    qt[0, :, 3] = 1.0
    qt[1, :, 0] = -1.0
    c = np.arange(tk)
    ka = np.zeros((tk, LANES), np.float32)
    ka[:, 0] = 1.0
    ka[:, 1] = 1.0
    ka[:, 2] = c % 256
    ka[:, 3] = (c // 256) * 256
    ka[:, SHIFT_LANES[0]:SHIFT_LANES[-1] + 1] = 1.0
    return jnp.asarray(qt, F32), jnp.asarray(ka, BF16)


def diff_attention(qk2d, v2d, lamv, subln_g, shift, B, S, lam_init, fixed_shift, tile=1024):
    T, W = qk2d.shape
    H = A_HEADS
    dh = W // (4 * H)
    dv = 2 * dh
    tq = tk = min(tile, S)
    nq, nk = S // tq, S // tk
    slopes = 2.0 ** (-8.0 * np.arange(1, H + 1) / H)
    assert np.all(np.log2(slopes) == np.round(np.log2(slopes))) and S <= 8192
    qt, ka = _alibi_templates(tq, tk)
    parts = _split_bf16(shift if fixed_shift else jnp.zeros((), F32), len(SHIFT_LANES))
    lane = lax.broadcasted_iota(jnp.int32, (1, tq, LANES), 2)
    shift_plane = sum(jnp.where(lane == ln, -p, 0.0) for ln, p in zip(SHIFT_LANES, parts))
    qt = jnp.concatenate([qt, shift_plane.astype(F32)], axis=0)
    applied = sum(parts).reshape(1)
    return pl.pallas_call(
        functools.partial(_attn_body, tq=tq, tk=tk, dh=dh, lam_init=lam_init, fixed_shift=fixed_shift),
        grid=(B, H, nq, nk),
        in_specs=[pl.BlockSpec(memory_space=pltpu.SMEM),
                  pl.BlockSpec(memory_space=pltpu.SMEM),
                  pl.BlockSpec((4, dh), lambda b, h, i, j: (0, 0)),
                  pl.BlockSpec((1, dv), lambda b, h, i, j: (0, 0)),
                  pl.BlockSpec((3, tq, LANES), lambda b, h, i, j: (0, 0, 0)),
                  pl.BlockSpec((tk, LANES), lambda b, h, i, j: (0, 0)),
                  pl.BlockSpec((tq, dv), lambda b, h, i, j: (b * nq + i, h)),
                  pl.BlockSpec((tk, dv), lambda b, h, i, j: (b * nk + j, H + h)),
                  pl.BlockSpec((tk, dv), lambda b, h, i, j: (b * nk + j, h))],
        out_specs=pl.BlockSpec((tq, dv), lambda b, h, i, j: (b * nq + i, h)),
        out_shape=jax.ShapeDtypeStruct((T, H * dv), BF16),
        scratch_shapes=[pltpu.VMEM((2, tq, LANES), F32), pltpu.VMEM((2, tq, LANES), F32),
                        pltpu.VMEM((2, tq, dv), F32)],
        compiler_params=_params("parallel", "parallel", "parallel", "arbitrary"),
        name="diff_attention_shifted" if fixed_shift else "diff_attention_online",
    )(jnp.asarray(slopes, F32), applied, lamv, subln_g.reshape(1, dv).astype(F32), qt, ka, qk2d, qk2d, v2d)


def _merge_body(fm_ref, on_ref, gf_ref, ga_ref, wf_ref, wa_ref, o_ref):
    yf = jnp.dot(fm_ref[...].astype(BF16), wf_ref[...], preferred_element_type=F32)
    ya = jnp.dot(on_ref[...], wa_ref[...], preferred_element_type=F32)
    o_ref[...] = (gf_ref[...].astype(F32) * yf + ga_ref[...].astype(F32) * ya).astype(o_ref.dtype)


def gated_merge(fm, on, gates, wf, wa, tm=256):
    T, D = on.shape
    tm = min(tm, T)
    return pl.pallas_call(
        _merge_body,
        grid=(T // tm,),
        in_specs=[pl.BlockSpec((tm, fm.shape[1]), lambda i: (i, 0)),
                  pl.BlockSpec((tm, D), lambda i: (i, 0)),
                  pl.BlockSpec((tm, D), lambda i: (i, 0)),
                  pl.BlockSpec((tm, D), lambda i: (i, 1)),
                  pl.BlockSpec(wf.shape, lambda i: (0, 0)),
                  pl.BlockSpec(wa.shape, lambda i: (0, 0))],
        out_specs=pl.BlockSpec((tm, D), lambda i: (i, 0)),
        out_shape=jax.ShapeDtypeStruct((T, D), BF16),
        compiler_params=_params("parallel"),
        name="gated_merge",
    )(fm, on, gates, gates, wf, wa)


def _out_body(mx_ref, x_ref, wo_ref, g_ref, h_ref, hn_ref, hnt_ref):
    h = x_ref[...] + jnp.dot(mx_ref[...], wo_ref[...], preferred_element_type=F32)
    h_ref[...] = h
    ms = jnp.mean(h * h, axis=-1, keepdims=True)
    hn = h * lax.rsqrt(ms + EPS) * g_ref[...]
    hn_ref[...] = hn.astype(hn_ref.dtype)
    hnt_ref[...] = hn.T.astype(hnt_ref.dtype)


def out_projection(mixed, x2d, wo, g2, tm=512):
    T, D = x2d.shape
    tm = min(tm, T)
    return pl.pallas_call(
        _out_body,
        grid=(T // tm,),
        in_specs=[pl.BlockSpec((tm, D), lambda i: (i, 0)),
                  pl.BlockSpec((tm, D), lambda i: (i, 0)),
                  pl.BlockSpec((D, D), lambda i: (0, 0)),
                  pl.BlockSpec((1, D), lambda i: (0, 0))],
        out_specs=[pl.BlockSpec((tm, D), lambda i: (i, 0)),
                   pl.BlockSpec((tm, D), lambda i: (i, 0)),
                   pl.BlockSpec((D, tm), lambda i: (0, i))],
        out_shape=[jax.ShapeDtypeStruct((T, D), F32), jax.ShapeDtypeStruct((T, D), BF16),
                   jax.ShapeDtypeStruct((D, T), BF16)],
        compiler_params=_params("parallel"),
        name="out_projection",
    )(mixed, x2d, wo, g2.reshape(1, D).astype(F32))


def _extract_top(x, dst_ref, n, want_rank=False):
    rank = jnp.full(x.shape, float(n), F32) if want_rank else None
    for r in range(n):
        mx = jnp.max(x, axis=0, keepdims=True)
        dst_ref[pl.ds(r, 1), :] = mx
        hit = x >= mx
        if want_rank:
            rank = jnp.where(hit, float(r), rank)
        x = jnp.where(hit, NEG_INF, x)
    return rank


def _peer_query_body(hn_ref, wq_ref, keys_ref, n1_ref, e1_ref, r2_ref, e2_ref, sv_ref, cs_ref):
    qp = jnp.dot(hn_ref[...], wq_ref[...], preferred_element_type=F32).astype(BF16)
    half = keys_ref.shape[-1]
    nt = (((1,), (1,)), ((), ()))
    for h in range(P_HEADS):
        lo = 2 * h * half
        st1 = lax.dot_general(keys_ref[h, 0], qp[:, lo:lo + half], nt, preferred_element_type=F32)
        st2 = lax.dot_general(keys_ref[h, 1], qp[:, lo + half:lo + 2 * half], nt,
                              preferred_element_type=F32)
        _extract_top(st1, sv_ref.at[0], P_TOPK)
        r2 = _extract_top(st2, sv_ref.at[1], P_TOPK, want_rank=True)
        sv1 = sv_ref[0]
        sv2 = sv_ref[1]
        parts = [sv1[0:1] + sv2]
        for a in range(1, 8):
            parts.append(sv1[a:a + 1] + sv2[0:8])
        parts.append(sv1[8:16] + sv2[0:1])
        _extract_top(jnp.concatenate(parts, axis=0), cs_ref, P_TOPK)
        cs = cs_ref[...]
        tau = cs[P_TOPK - 1:P_TOPK]
        z = jnp.sum(jnp.exp(cs - cs[0:1]), axis=0, keepdims=True)
        n1 = jnp.zeros(st1.shape, F32)
        for a in range(P_TOPK):
            va = sv1[a:a + 1]
            n_a = jnp.sum(jnp.where(va + sv2 >= tau, 1.0, 0.0), axis=0, keepdims=True)
            n1 = jnp.where(st1 == va, n_a, n1)
        e1 = jnp.exp(st1 - sv1[0:1])
        e2 = jnp.exp(st2 - sv2[0:1]) / z
        for tb in range(st1.shape[1] // LANES):
            cols = slice(tb * LANES, (tb + 1) * LANES)
            n1_ref[h, tb] = n1[:, cols]
            e1_ref[h, tb] = e1[:, cols]
            r2_ref[h, tb] = pltpu.bitcast(r2[:, cols].astype(BF16), jnp.uint32)
            e2_ref[h, tb] = pltpu.bitcast(e2[:, cols].astype(BF16), jnp.uint32)


def peer_query(hn, wq, keys, tt=256):
    T, D = hn.shape
    H, _, nk, half = keys.shape
    tt = min(tt, T)
    spec = pl.BlockSpec((H, tt // LANES, nk, LANES), lambda i: (0, i, 0, 0))
    rows = jax.ShapeDtypeStruct((H, T // LANES, nk, LANES), F32)
    pspec = pl.BlockSpec((H, tt // LANES, nk // 2, LANES), lambda i: (0, i, 0, 0))
    tiles = jax.ShapeDtypeStruct((H, T // LANES, nk // 2, LANES), jnp.uint32)
    return pl.pallas_call(
        _peer_query_body,
        grid=(T // tt,),
        in_specs=[pl.BlockSpec((tt, D), lambda i: (i, 0)),
                  pl.BlockSpec(wq.shape, lambda i: (0, 0)),
                  pl.BlockSpec(keys.shape, lambda i: (0, 0, 0, 0))],
        out_specs=[spec, spec, pspec, pspec],
        out_shape=[rows, rows, tiles, tiles],
        scratch_shapes=[pltpu.VMEM((2, P_TOPK, tt), F32), pltpu.VMEM((P_TOPK, tt), F32)],
        compiler_params=_params("parallel"),
        name="peer_query",
    )(hn, wq, keys)


def _peer_dense_body(hnt_ref, u_ref, v_ref, n1_ref, e1_ref, r2_ref, e2_ref, h_ref, o_ref,
                     sc0_ref, sc1_ref, a0_ref, a1_ref, *, nk, rows_per_chunk, n_chunks):
    e = pl.program_id(1)

    @pl.when(e == 0)
    def _():
        o_ref[...] = h_ref[...]
        sc1_ref[...] = jnp.zeros(sc1_ref.shape, F32)
        a0_ref[...] = jnp.zeros(a0_ref.shape, BF16)
        a1_ref[...] = jnp.zeros(a1_ref.shape, BF16)

    def stages(sc_w, sc_r, a_w, a_r):
        tt = sc_r.shape[1]
        chunk = jnp.clip(e - 1, 0, n_chunks - 1)

        def anchored(lhs, dep):
            if dep is None:
                return lhs
            z = jnp.tile(dep, (BF16_ROWS // dep.shape[0], lhs.shape[1] // LANES)).astype(BF16)
            return jnp.concatenate([lhs[0:BF16_ROWS, :] + z, lhs[BF16_ROWS:, :]], axis=0)

        def up_piece(half, kb, dep):
            rows = slice(half * (tt // 2), (half + 1) * (tt // 2))
            ks = slice(kb * UP_K, (kb + 1) * UP_K)
            return jnp.dot(anchored(u_ref[:, ks], dep), hnt_ref[ks, rows], preferred_element_type=F32)

        def gate(tb, ii):
            i = chunk * rows_per_chunk + ii
            def key_row(ref, h):
                row = jnp.broadcast_to(ref[h, tb, pl.ds(i, 1), :], (BF16_ROWS, LANES)).astype(BF16)
                return jnp.tile(row, (nk // BF16_ROWS, 1))

            w = jnp.zeros((nk, LANES), BF16)
            for h in range(P_HEADS):
                w = w + jnp.where(pltpu.bitcast(r2_ref[h, tb], BF16) < key_row(n1_ref, h),
                                  pltpu.bitcast(e2_ref[h, tb], BF16) * key_row(e1_ref, h), 0.0)
            x = sc_r[ii * nk:(ii + 1) * nk, tb * LANES:(tb + 1) * LANES]
            act = 0.5 * x * (1.0 + lax.erf(x * (1.0 / math.sqrt(2.0))))
            prod = (act * w.astype(F32)).T
            a_w[tb * LANES:(tb + 1) * LANES, ii * nk:(ii + 1) * nk] = prod.astype(BF16)
            bits = pltpu.bitcast(prod[0:8, :], jnp.uint32)
            bits = lax.shift_right_logical(lax.shift_right_logical(bits, jnp.uint32(16)), jnp.uint32(16))
            return pltpu.bitcast(bits, F32)

        def down(n, dep):
            cols = slice(n * MXU_COLS, (n + 1) * MXU_COLS)
            o_ref[:, cols] += jnp.dot(anchored(a_r[...], dep), v_ref[:, cols], preferred_element_type=F32)

        blocks = [(tb, ii) for tb in range(tt // LANES) for ii in range(rows_per_chunk)]
        n_down = o_ref.shape[1] // MXU_COLS
        n_kb = u_ref.shape[1] // UP_K
        per_gate = 2 * n_kb // (len(blocks) - n_down)
        dep = None
        g = 0
        for half in range(2):
            acc = None
            for kb in range(n_kb):
                piece = up_piece(half, kb, dep)
                acc = piece if acc is None else acc + piece
                if (half * n_kb + kb + 1) % per_gate == 0:
                    dep = gate(*blocks[g])
                    g += 1
            sc_w[:, half * (tt // 2):(half + 1) * (tt // 2)] = acc
        for n in range(n_down):
            down(n, dep)
            dep = gate(*blocks[g + n])

    @pl.when(e % 2 == 0)
    def _():
        stages(sc0_ref, sc1_ref, a1_ref, a0_ref)

    @pl.when(e % 2 == 1)
    def _():
        stages(sc1_ref, sc0_ref, a0_ref, a1_ref)


def peer_dense(hnt, u, v, n1, e1, r2, e2, h1, tt=512, ec=512):
    D, T = hnt.shape
    E = u.shape[0]
    H, _, nk, _ = n1.shape
    tt = min(tt, T)
    n_chunks = E // ec
    tok = pl.BlockSpec((H, tt // LANES, nk, LANES), lambda t, e: (0, t, 0, 0))
    ptok = pl.BlockSpec((H, tt // LANES, nk // 2, LANES), lambda t, e: (0, t, 0, 0))
    return pl.pallas_call(
        functools.partial(_peer_dense_body, nk=nk, rows_per_chunk=ec // nk, n_chunks=n_chunks),
        grid=(T // tt, n_chunks + 2),
        in_specs=[pl.BlockSpec((D, tt), lambda t, e: (0, t)),
                  pl.BlockSpec((ec, D), lambda t, e: (jnp.minimum(e, n_chunks - 1), 0)),
                  pl.BlockSpec((ec, D), lambda t, e: (jnp.clip(e - 2, 0, n_chunks - 1), 0)),
                  tok, tok, ptok, ptok,
                  pl.BlockSpec((tt, D), lambda t, e: (t, 0))],
        out_specs=pl.BlockSpec((tt, D), lambda t, e: (t, 0)),
        out_shape=jax.ShapeDtypeStruct((T, D), F32),
        scratch_shapes=[pltpu.VMEM((ec, tt), F32), pltpu.VMEM((ec, tt), F32),
                        pltpu.VMEM((tt, ec), BF16), pltpu.VMEM((tt, ec), BF16)],
        compiler_params=_params("parallel", "arbitrary"),
        name="peer_dense",
    )(hnt, u, v, n1, e1, r2, e2, h1)


def kernel(x, norm1_g, w_in, w_fourier, w_attn, q_norm_g, k_norm_g, lambda_q1, lambda_k1,
           lambda_q2, lambda_k2, subln_g, w_out, norm2_g, w_query, sub_keys, expert_u, expert_v):
    B, S, D = x.shape
    T = B * S
    depth = w_in.shape[0]
    dh = D // (2 * A_HEADS)
    f_width = w_fourier.shape[1]
    qk_width = A_HEADS * 2 * dh
    v_width = w_attn.shape[1]
    o_q = f_width
    o_k = o_q + qk_width
    o_v = o_k + qk_width
    o_g = o_v + v_width
    h = x.reshape(T, D)
    for i in range(depth):
        lam_init = 0.8 - 0.6 * math.exp(-0.3 * i)
        xn = rmsnorm_rows(h, norm1_g[i])
        w_in_b = w_in[i].astype(BF16)
        qk_gain = jnp.concatenate([jnp.tile(q_norm_g[i].astype(F32) * dh ** -0.5, 2 * A_HEADS),
                                   jnp.tile(k_norm_g[i].astype(F32), 2 * A_HEADS)]).reshape(1, 2 * qk_width)
        z = project(xn, w_in_b, 0, f_width, "cast", out_dtype=F32)
        qk = project(xn, w_in_b, o_q, 2 * qk_width, "headnorm", gain=qk_gain, group=dh)
        vv = project(xn, w_in_b, o_v, v_width, "cast")
        gates = project(xn, w_in_b, o_g, 2 * D, "sigmoid")

        fm = fourier_mix(z, B, S)
        lamv = jnp.stack([lambda_q1[i], lambda_k1[i], lambda_q2[i], lambda_k2[i]]).astype(F32)
        bound = (jnp.max(jnp.abs(q_norm_g[i])) * jnp.max(jnp.abs(k_norm_g[i]))).astype(F32) * (1.01 * dh ** 0.5)
        attend = functools.partial(diff_attention, qk, vv, lamv, subln_g[i], bound, B, S, lam_init)
        on = lax.cond(bound <= MAX_SAFE_SHIFT, lambda: attend(True), lambda: attend(False))

        mixed = gated_merge(fm, on, gates, w_fourier[i].astype(BF16), w_attn[i].astype(BF16))
        h1, hn, hnt = out_projection(mixed, h, w_out[i].astype(BF16), norm2_g[i])

        n1, e1, r2, e2 = peer_query(hn, w_query[i].astype(BF16), sub_keys[i].astype(BF16))
        h = peer_dense(hnt, expert_u[i].astype(BF16), expert_v[i].astype(BF16), n1, e1, r2, e2, h1)
    return h.reshape(B, S, D)
```

```python
import functools
import math

import numpy as np
import jax
import jax.numpy as jnp
from jax import lax
from jax.experimental import pallas as pl
from jax.experimental.pallas import tpu as pltpu

EPS = 1e-6
F_GROUPS = 4
A_HEADS = 8
P_HEADS = 8
P_TOPK = 16
LANES = 128
SUBLANES = 8
MXU_COLS = 256
BF16_ROWS = 16
UP_K = 256
MAX_SAFE_SHIFT = 40.0
FFT_INNER = 128
VMEM_LIMIT_BYTES = 56 * 1024 * 1024

F32 = jnp.float32
BF16 = jnp.bfloat16
NEG_INF = float("-inf")


def _params(*semantics):
    return pltpu.CompilerParams(dimension_semantics=semantics, vmem_limit_bytes=VMEM_LIMIT_BYTES)


def _rmsnorm_body(x_ref, g_ref, o_ref):
    x = x_ref[...]
    ms = jnp.mean(x * x, axis=-1, keepdims=True)
    o_ref[...] = (x * lax.rsqrt(ms + EPS) * g_ref[...]).astype(o_ref.dtype)


def rmsnorm_rows(x2d, g, tm=512):
    T, D = x2d.shape
    return pl.pallas_call(
        _rmsnorm_body,
        grid=(T // tm,),
        in_specs=[pl.BlockSpec((tm, D), lambda i: (i, 0)),
                  pl.BlockSpec((1, D), lambda i: (0, 0))],
        out_specs=pl.BlockSpec((tm, D), lambda i: (i, 0)),
        out_shape=jax.ShapeDtypeStruct((T, D), BF16),
        compiler_params=_params("parallel"),
        name="rmsnorm",
    )(x2d, g.reshape(1, D).astype(F32))


def _proj_body(x_ref, w_ref, g_ref, o_ref, *, mode, group):
    acc = jnp.dot(x_ref[...], w_ref[...], preferred_element_type=F32)
    if mode == "cast":
        o_ref[...] = acc.astype(o_ref.dtype)
    elif mode == "sigmoid":
        o_ref[...] = jax.nn.sigmoid(acc).astype(o_ref.dtype)
    else:
        for s in range(acc.shape[1] // group):
            blk = acc[:, s * group:(s + 1) * group]
            ms = jnp.mean(blk * blk, axis=-1, keepdims=True)
            o_ref[:, s * group:(s + 1) * group] = (
                blk * lax.rsqrt(ms + EPS) * g_ref[:, s * group:(s + 1) * group]).astype(o_ref.dtype)


def project(x, w, col0, ncols, mode, gain=None, group=LANES, out_dtype=BF16, tm=1024, tn=1024):
    T, K = x.shape
    tm = min(tm, T)
    if gain is None:
        gain = jnp.ones((1, ncols), F32)
    off = col0 // tn
    return pl.pallas_call(
        functools.partial(_proj_body, mode=mode, group=group),
        grid=(T // tm, ncols // tn),
        in_specs=[pl.BlockSpec((tm, K), lambda i, j: (i, 0)),
                  pl.BlockSpec((K, tn), lambda i, j: (0, off + j)),
                  pl.BlockSpec((1, tn), lambda i, j: (0, j))],
        out_specs=pl.BlockSpec((tm, tn), lambda i, j: (i, j)),
        out_shape=jax.ShapeDtypeStruct((T, ncols), out_dtype),
        compiler_params=_params("parallel", "arbitrary"),
        name="proj_" + mode,
    )(x, w, gain)


def _fourier_constants(S, C):
    Na = S // FFT_INNER
    eye = np.eye(SUBLANES)
    n2 = np.arange(C, dtype=np.int64)
    ang = 2.0 * np.pi * ((n2[:, None] * n2[None, :]) % C) / C
    w0 = np.concatenate([np.cos(ang), -np.sin(ang)], axis=1)
    a = np.arange(Na, dtype=np.int64)
    phi = 2.0 * np.pi * ((a[:, None] * a[None, :]) % Na) / Na
    r = np.block([[np.cos(phi), np.sin(phi)], [-np.sin(phi), np.cos(phi)]])
    r4 = r.reshape(2, Na, 2, Na)
    rk = np.einsum('qcpa,xy->qxcpay', r4, eye).reshape(2 * SUBLANES * Na, 2 * Na * SUBLANES)
    d = np.arange(FFT_INNER, dtype=np.int64)
    psi = 2.0 * np.pi * ((d[:, None] * d[None, :]) % FFT_INNER) / FFT_INNER
    f2 = np.stack([np.cos(psi), np.sin(psi)], axis=1) / math.sqrt(S * C)
    kf = np.einsum('drb,xy->dxrby', f2, eye).reshape(FFT_INNER * SUBLANES, 2 * FFT_INNER * SUBLANES)
    return (jnp.asarray(w0, BF16), jnp.asarray(rk, BF16), jnp.asarray(kf, BF16))


def _twiddles(S):
    Na = S // FFT_INNER
    b = lax.broadcasted_iota(jnp.int32, (FFT_INNER, Na), 0)
    c = lax.broadcasted_iota(jnp.int32, (FFT_INNER, Na), 1)
    th = ((b * c) % S).astype(F32) * (2.0 * math.pi / S)
    shape = (FFT_INNER, Na, LANES)
    return (jnp.broadcast_to(jnp.cos(th)[:, :, None], shape), jnp.broadcast_to(jnp.sin(th)[:, :, None], shape))


def _fourier_outer_body(z_ref, w0_ref, rk_ref, o_ref, *, groups):
    na, sub, w = z_ref.shape
    c = w // groups
    x = z_ref[...].reshape(na * sub, w).astype(BF16)
    ps, qs = [], []
    for g in range(groups):
        pq = jnp.dot(x[:, g * c:(g + 1) * c], w0_ref[...], preferred_element_type=F32)
        ps.append(pq[:, :c])
        qs.append(pq[:, c:])
    x1 = jnp.concatenate([jnp.concatenate(ps, axis=1), jnp.concatenate(qs, axis=1)], axis=0)
    a = jnp.dot(rk_ref[...], x1.astype(BF16), preferred_element_type=F32)
    o_ref[...] = a.reshape(2, sub, na, w)


def _fourier_inner_body(a_ref, tc_ref, ts_ref, kf_ref, o_ref):
    _, nb, sub, c = a_ref.shape
    ar = a_ref[0].reshape(nb * sub, c)
    ai = a_ref[1].reshape(nb * sub, c)
    tc = jnp.tile(tc_ref[...].reshape(nb * sub, LANES), (1, c // LANES))
    ts = jnp.tile(ts_ref[...].reshape(nb * sub, LANES), (1, c // LANES))
    x3 = jnp.concatenate([ar * tc + ai * ts, ai * tc - ar * ts], axis=0).astype(BF16)
    y = jnp.dot(kf_ref[...], x3, preferred_element_type=F32)
    o_ref[...] = y.reshape(nb, sub, c)


def fourier_mix(z2d, B, S):
    T, W = z2d.shape
    G = F_GROUPS
    C = W // G
    Na = S // FFT_INNER
    assert Na % SUBLANES == 0
    w0, rk, kf = _fourier_constants(S, C)
    tc, ts = _twiddles(S)
    a5 = pl.pallas_call(
        functools.partial(_fourier_outer_body, groups=G),
        grid=(B, FFT_INNER // SUBLANES),
        in_specs=[pl.BlockSpec((None, Na, SUBLANES, W), lambda b, j: (b, 0, j, 0)),
                  pl.BlockSpec(w0.shape, lambda b, j: (0, 0)),
                  pl.BlockSpec(rk.shape, lambda b, j: (0, 0))],
        out_specs=pl.BlockSpec((None, 2, SUBLANES, Na, W), lambda b, j: (b, 0, j, 0, 0)),
        out_shape=jax.ShapeDtypeStruct((B, 2, FFT_INNER, Na, W), F32),
        compiler_params=_params("parallel", "arbitrary"),
        name="fourier_outer",
    )(z2d.reshape(B, Na, FFT_INNER, W), w0, rk)
    y = pl.pallas_call(
        _fourier_inner_body,
        grid=(B, Na // SUBLANES, G),
        in_specs=[pl.BlockSpec((None, 2, FFT_INNER, SUBLANES, C), lambda b, j, g: (b, 0, 0, j, g)),
                  pl.BlockSpec((FFT_INNER, SUBLANES, LANES), lambda b, j, g: (0, j, 0)),
                  pl.BlockSpec((FFT_INNER, SUBLANES, LANES), lambda b, j, g: (0, j, 0)),
                  pl.BlockSpec(kf.shape, lambda b, j, g: (0, 0))],
        out_specs=pl.BlockSpec((None, FFT_INNER, SUBLANES, C), lambda b, j, g: (b, 0, j, g)),
        out_shape=jax.ShapeDtypeStruct((B, FFT_INNER, Na, W), F32),
        compiler_params=_params("parallel", "parallel", "arbitrary"),
        name="fourier_inner",
    )(a5, tc, ts, kf)
    return y.reshape(T, W)


def _attn_body(slope_ref, shift_ref, lamv_ref, g_ref, qt_ref, ka_ref, q_ref, k_ref, v_ref, o_ref,
               m_ref, l_ref, acc_ref, *, tq, tk, dh, lam_init, fixed_shift):
    h = pl.program_id(1)
    qi = pl.program_id(2)
    ki = pl.program_id(3)
    dv = 2 * dh

    @pl.when(ki == 0)
    def _():
        m_ref[...] = jnp.full(m_ref.shape, NEG_INF, F32)
        l_ref[...] = jnp.zeros(l_ref.shape, F32)
        acc_ref[...] = jnp.zeros(acc_ref.shape, F32)

    slope = slope_ref[h]

    def online_softmax_update(scores):
        v = v_ref[...]
        for c in range(2):
            s = scores[c]
            m_prev = m_ref[c]
            m_new = jnp.maximum(m_prev, jnp.max(s, axis=-1, keepdims=True))
            alpha = jnp.exp(m_prev - m_new)
            p = jnp.exp(s - jnp.tile(m_new, (1, tk // LANES)))
            l_ref[c] = alpha * l_ref[c] + jnp.sum(p, axis=-1, keepdims=True)
            acc_ref[c] = (jnp.tile(alpha, (1, dv // LANES)) * acc_ref[c]
                          + jnp.dot(p.astype(BF16), v, preferred_element_type=F32))
            m_ref[c] = m_new

    def shifted_update(scores):
        v = v_ref[...]
        for c in range(2):
            p = jnp.exp(scores[c])
            part = p[:, 0:LANES]
            for t in range(1, tk // LANES):
                part = part + p[:, t * LANES:(t + 1) * LANES]
            l_ref[c] += part
            acc_ref[c] += jnp.dot(p.astype(BF16), v, preferred_element_type=F32)

    update = shifted_update if fixed_shift else online_softmax_update
    nt = (((1,), (1,)), ((), ()))

    @pl.when(qi != ki)
    def _():
        off = (qi - ki) * tq
        sgn = jnp.where(off > 0, slope, -slope)
        qa = ((qt_ref[0] + off.astype(F32) * qt_ref[1]) * sgn + qt_ref[2]).astype(BF16)
        ka = ka_ref[...]
        q = q_ref[...]
        k = k_ref[...]
        scores = []
        for c in range(2):
            qc = jnp.concatenate([q[:, c * dh:(c + 1) * dh], qa], axis=1)
            kc = jnp.concatenate([k[:, c * dh:(c + 1) * dh], ka], axis=1)
            scores.append(lax.dot_general(qc, kc, nt, preferred_element_type=F32))
        update(scores)

    @pl.when(qi == ki)
    def _():
        row = lax.broadcasted_iota(jnp.int32, (tq, tk), 0)
        col = lax.broadcasted_iota(jnp.int32, (tq, tk), 1)
        bias = jnp.abs(row - col).astype(F32) * (-slope) - shift_ref[0]
        q = q_ref[...]
        k = k_ref[...]
        scores = [lax.dot_general(q[:, c * dh:(c + 1) * dh], k[:, c * dh:(c + 1) * dh], nt,
                                  preferred_element_type=F32) + bias for c in range(2)]
        update(scores)

    @pl.when(ki == pl.num_programs(3) - 1)
    def _():
        lv = lamv_ref[...]
        lam = (jnp.exp(jnp.sum(lv[0:1] * lv[1:2], axis=-1, keepdims=True))
               - jnp.exp(jnp.sum(lv[2:3] * lv[3:4], axis=-1, keepdims=True)) + lam_init)
        if fixed_shift:
            l0 = jnp.sum(l_ref[0], axis=-1, keepdims=True)
            l1 = jnp.sum(l_ref[1], axis=-1, keepdims=True)
        else:
            l0 = jnp.tile(l_ref[0], (1, dv // LANES))
            l1 = jnp.tile(l_ref[1], (1, dv // LANES))
        o = acc_ref[0] / l0 - lam * (acc_ref[1] / l1)
        ms = jnp.mean(o * o, axis=-1, keepdims=True)
        o_ref[...] = (o * lax.rsqrt(ms + EPS) * g_ref[...] * (1.0 - lam_init)).astype(o_ref.dtype)


SHIFT_LANES = (4, 5, 6)


def _split_bf16(x, parts):
    out = []
    for _ in range(parts):
        p = x.astype(BF16).astype(F32)
        out.append(p)
        x = x - p
    return out


def _alibi_templates(tq, tk):
    r = np.arange(tq)
    qt = np.zeros((2, tq, LANES), np.float32)
    qt[0, :, 0] = -(r // 256) * 256
    qt[0, :, 1] = -(r % 256)
    qt[0, :, 2] = 1.0
    qt[0, :, 3] = 1.0
    qt[1, :, 0] = -1.0
    c = np.arange(tk)
    ka = np.zeros((tk, LANES), np.float32)
    ka[:, 0] = 1.0
    ka[:, 1] = 1.0
    ka[:, 2] = c % 256
    ka[:, 3] = (c // 256) * 256
    ka[:, SHIFT_LANES[0]:SHIFT_LANES[-1] + 1] = 1.0
    return jnp.asarray(qt, F32), jnp.asarray(ka, BF16)


def diff_attention(qk2d, v2d, lamv, subln_g, shift, B, S, lam_init, fixed_shift, tile=1024):
    T, W = qk2d.shape
    H = A_HEADS
    dh = W // (4 * H)
    dv = 2 * dh
    tq = tk = min(tile, S)
    nq, nk = S // tq, S // tk
    slopes = 2.0 ** (-8.0 * np.arange(1, H + 1) / H)
    assert np.all(np.log2(slopes) == np.round(np.log2(slopes))) and S <= 8192
    qt, ka = _alibi_templates(tq, tk)
    parts = _split_bf16(shift if fixed_shift else jnp.zeros((), F32), len(SHIFT_LANES))
    lane = lax.broadcasted_iota(jnp.int32, (1, tq, LANES), 2)
    shift_plane = sum(jnp.where(lane == ln, -p, 0.0) for ln, p in zip(SHIFT_LANES, parts))
    qt = jnp.concatenate([qt, shift_plane.astype(F32)], axis=0)
    applied = sum(parts).reshape(1)
    return pl.pallas_call(
        functools.partial(_attn_body, tq=tq, tk=tk, dh=dh, lam_init=lam_init, fixed_shift=fixed_shift),
        grid=(B, H, nq, nk),
        in_specs=[pl.BlockSpec(memory_space=pltpu.SMEM),
                  pl.BlockSpec(memory_space=pltpu.SMEM),
                  pl.BlockSpec((4, dh), lambda b, h, i, j: (0, 0)),
                  pl.BlockSpec((1, dv), lambda b, h, i, j: (0, 0)),
                  pl.BlockSpec((3, tq, LANES), lambda b, h, i, j: (0, 0, 0)),
                  pl.BlockSpec((tk, LANES), lambda b, h, i, j: (0, 0)),
                  pl.BlockSpec((tq, dv), lambda b, h, i, j: (b * nq + i, h)),
                  pl.BlockSpec((tk, dv), lambda b, h, i, j: (b * nk + j, H + h)),
                  pl.BlockSpec((tk, dv), lambda b, h, i, j: (b * nk + j, h))],
        out_specs=pl.BlockSpec((tq, dv), lambda b, h, i, j: (b * nq + i, h)),
        out_shape=jax.ShapeDtypeStruct((T, H * dv), BF16),
        scratch_shapes=[pltpu.VMEM((2, tq, LANES), F32), pltpu.VMEM((2, tq, LANES), F32),
                        pltpu.VMEM((2, tq, dv), F32)],
        compiler_params=_params("parallel", "parallel", "parallel", "arbitrary"),
        name="diff_attention_shifted" if fixed_shift else "diff_attention_online",
    )(jnp.asarray(slopes, F32), applied, lamv, subln_g.reshape(1, dv).astype(F32), qt, ka, qk2d, qk2d, v2d)


def _merge_body(fm_ref, on_ref, gf_ref, ga_ref, wf_ref, wa_ref, o_ref):
    yf = jnp.dot(fm_ref[...].astype(BF16), wf_ref[...], preferred_element_type=F32)
    ya = jnp.dot(on_ref[...], wa_ref[...], preferred_element_type=F32)
    o_ref[...] = (gf_ref[...].astype(F32) * yf + ga_ref[...].astype(F32) * ya).astype(o_ref.dtype)


def gated_merge(fm, on, gates, wf, wa, tm=256):
    T, D = on.shape
    tm = min(tm, T)
    return pl.pallas_call(
        _merge_body,
        grid=(T // tm,),
        in_specs=[pl.BlockSpec((tm, fm.shape[1]), lambda i: (i, 0)),
                  pl.BlockSpec((tm, D), lambda i: (i, 0)),
                  pl.BlockSpec((tm, D), lambda i: (i, 0)),
                  pl.BlockSpec((tm, D), lambda i: (i, 1)),
                  pl.BlockSpec(wf.shape, lambda i: (0, 0)),
                  pl.BlockSpec(wa.shape, lambda i: (0, 0))],
        out_specs=pl.BlockSpec((tm, D), lambda i: (i, 0)),
        out_shape=jax.ShapeDtypeStruct((T, D), BF16),
        compiler_params=_params("parallel"),
        name="gated_merge",
    )(fm, on, gates, gates, wf, wa)


def _out_body(mx_ref, x_ref, wo_ref, g_ref, h_ref, hn_ref, hnt_ref):
    h = x_ref[...] + jnp.dot(mx_ref[...], wo_ref[...], preferred_element_type=F32)
    h_ref[...] = h
    ms = jnp.mean(h * h, axis=-1, keepdims=True)
    hn = h * lax.rsqrt(ms + EPS) * g_ref[...]
    hn_ref[...] = hn.astype(hn_ref.dtype)
    hnt_ref[...] = hn.T.astype(hnt_ref.dtype)


def out_projection(mixed, x2d, wo, g2, tm=512):
    T, D = x2d.shape
    tm = min(tm, T)
    return pl.pallas_call(
        _out_body,
        grid=(T // tm,),
        in_specs=[pl.BlockSpec((tm, D), lambda i: (i, 0)),
                  pl.BlockSpec((tm, D), lambda i: (i, 0)),
                  pl.BlockSpec((D, D), lambda i: (0, 0)),
                  pl.BlockSpec((1, D), lambda i: (0, 0))],
        out_specs=[pl.BlockSpec((tm, D), lambda i: (i, 0)),
                   pl.BlockSpec((tm, D), lambda i: (i, 0)),
                   pl.BlockSpec((D, tm), lambda i: (0, i))],
        out_shape=[jax.ShapeDtypeStruct((T, D), F32), jax.ShapeDtypeStruct((T, D), BF16),
                   jax.ShapeDtypeStruct((D, T), BF16)],
        compiler_params=_params("parallel"),
        name="out_projection",
    )(mixed, x2d, wo, g2.reshape(1, D).astype(F32))


def _extract_top(x, dst_ref, n, want_rank=False):
    rank = jnp.full(x.shape, float(n), F32) if want_rank else None
    for r in range(n):
        mx = jnp.max(x, axis=0, keepdims=True)
        dst_ref[pl.ds(r, 1), :] = mx
        hit = x >= mx
        if want_rank:
            rank = jnp.where(hit, float(r), rank)
        x = jnp.where(hit, NEG_INF, x)
    return rank


def _peer_query_body(hn_ref, wq_ref, keys_ref, n1_ref, e1_ref, r2_ref, e2_ref, sv_ref, cs_ref):
    qp = jnp.dot(hn_ref[...], wq_ref[...], preferred_element_type=F32).astype(BF16)
    half = keys_ref.shape[-1]
    nt = (((1,), (1,)), ((), ()))
    for h in range(P_HEADS):
        lo = 2 * h * half
        st1 = lax.dot_general(keys_ref[h, 0], qp[:, lo:lo + half], nt, preferred_element_type=F32)
        st2 = lax.dot_general(keys_ref[h, 1], qp[:, lo + half:lo + 2 * half], nt,
                              preferred_element_type=F32)
        _extract_top(st1, sv_ref.at[0], P_TOPK)
        r2 = _extract_top(st2, sv_ref.at[1], P_TOPK, want_rank=True)
        sv1 = sv_ref[0]
        sv2 = sv_ref[1]
        parts = [sv1[0:1] + sv2]
        for a in range(1, 8):
            parts.append(sv1[a:a + 1] + sv2[0:8])
        parts.append(sv1[8:16] + sv2[0:1])
        _extract_top(jnp.concatenate(parts, axis=0), cs_ref, P_TOPK)
        cs = cs_ref[...]
        tau = cs[P_TOPK - 1:P_TOPK]
        z = jnp.sum(jnp.exp(cs - cs[0:1]), axis=0, keepdims=True)
        n1 = jnp.zeros(st1.shape, F32)
        for a in range(P_TOPK):
            va = sv1[a:a + 1]
            n_a = jnp.sum(jnp.where(va + sv2 >= tau, 1.0, 0.0), axis=0, keepdims=True)
            n1 = jnp.where(st1 == va, n_a, n1)
        e1 = jnp.exp(st1 - sv1[0:1])
        e2 = jnp.exp(st2 - sv2[0:1]) / z
        for tb in range(st1.shape[1] // LANES):
            cols = slice(tb * LANES, (tb + 1) * LANES)
            n1_ref[h, tb] = n1[:, cols]
            e1_ref[h, tb] = e1[:, cols]
            r2_ref[h, tb] = pltpu.bitcast(r2[:, cols].astype(BF16), jnp.uint32)
            e2_ref[h, tb] = pltpu.bitcast(e2[:, cols].astype(BF16), jnp.uint32)


def peer_query(hn, wq, keys, tt=256):
    T, D = hn.shape
    H, _, nk, half = keys.shape
    tt = min(tt, T)
    spec = pl.BlockSpec((H, tt // LANES, nk, LANES), lambda i: (0, i, 0, 0))
    rows = jax.ShapeDtypeStruct((H, T // LANES, nk, LANES), F32)
    pspec = pl.BlockSpec((H, tt // LANES, nk // 2, LANES), lambda i: (0, i, 0, 0))
    tiles = jax.ShapeDtypeStruct((H, T // LANES, nk // 2, LANES), jnp.uint32)
    return pl.pallas_call(
        _peer_query_body,
        grid=(T // tt,),
        in_specs=[pl.BlockSpec((tt, D), lambda i: (i, 0)),
                  pl.BlockSpec(wq.shape, lambda i: (0, 0)),
                  pl.BlockSpec(keys.shape, lambda i: (0, 0, 0, 0))],
        out_specs=[spec, spec, pspec, pspec],
        out_shape=[rows, rows, tiles, tiles],
        scratch_shapes=[pltpu.VMEM((2, P_TOPK, tt), F32), pltpu.VMEM((P_TOPK, tt), F32)],
        compiler_params=_params("parallel"),
        name="peer_query",
    )(hn, wq, keys)


def _peer_dense_body(hnt_ref, u_ref, v_ref, n1_ref, e1_ref, r2_ref, e2_ref, h_ref, o_ref,
                     sc0_ref, sc1_ref, a0_ref, a1_ref, *, nk, rows_per_chunk, n_chunks):
    e = pl.program_id(1)

    @pl.when(e == 0)
    def _():
        o_ref[...] = h_ref[...]

    def stages(sc_w, sc_r, a_w, a_r, do_up=True, do_gate=True, do_down=True):
        tt = o_ref.shape[0]
        chunk = e - 1

        def anchored(lhs, dep):
            if dep is None:
                return lhs
            z = jnp.tile(dep, (BF16_ROWS // dep.shape[0], lhs.shape[1] // LANES)).astype(BF16)
            return jnp.concatenate([lhs[0:BF16_ROWS, :] + z, lhs[BF16_ROWS:, :]], axis=0)

        def up_piece(half, kb, dep):
            rows = slice(half * (tt // 2), (half + 1) * (tt // 2))
            ks = slice(kb * UP_K, (kb + 1) * UP_K)
            return jnp.dot(anchored(u_ref[:, ks], dep), hnt_ref[ks, rows], preferred_element_type=F32)

        def gate(tb, ii):
            i = chunk * rows_per_chunk + ii
            def key_row(ref, h):
                row = jnp.broadcast_to(ref[h, tb, pl.ds(i, 1), :], (BF16_ROWS, LANES)).astype(BF16)
                return jnp.tile(row, (nk // BF16_ROWS, 1))

            w = jnp.zeros((nk, LANES), BF16)
            for h in range(P_HEADS):
                w = w + jnp.where(pltpu.bitcast(r2_ref[h, tb], BF16) < key_row(n1_ref, h),
                                  pltpu.bitcast(e2_ref[h, tb], BF16) * key_row(e1_ref, h), 0.0)
            x = sc_r[ii * nk:(ii + 1) * nk, tb * LANES:(tb + 1) * LANES]
            act = 0.5 * x * (1.0 + lax.erf(x * (1.0 / math.sqrt(2.0))))
            prod = (act * w.astype(F32)).T
            a_w[tb * LANES:(tb + 1) * LANES, ii * nk:(ii + 1) * nk] = prod.astype(BF16)
            bits = pltpu.bitcast(prod[0:8, :], jnp.uint32)
            bits = lax.shift_right_logical(lax.shift_right_logical(bits, jnp.uint32(16)), jnp.uint32(16))
            return pltpu.bitcast(bits, F32)

        def down(n, dep):
            cols = slice(n * MXU_COLS, (n + 1) * MXU_COLS)
            o_ref[:, cols] += jnp.dot(anchored(a_r[...], dep), v_ref[:, cols], preferred_element_type=F32)

        blocks = [(tb, ii) for tb in range(tt // LANES) for ii in range(rows_per_chunk)]
        n_down = o_ref.shape[1] // MXU_COLS
        n_kb = u_ref.shape[1] // UP_K
        per_gate = 2 * n_kb // (len(blocks) - n_down)
        dep = None
        g = 0
        for half in range(2 if do_up else 0):
            acc = None
            for kb in range(n_kb):
                piece = up_piece(half, kb, dep)
                acc = piece if acc is None else acc + piece
                if do_gate and (half * n_kb + kb + 1) % per_gate == 0:
                    dep = gate(*blocks[g])
                    g += 1
            sc_w[:, half * (tt // 2):(half + 1) * (tt // 2)] = acc
        if do_gate and not do_up:
            for blk in blocks[:len(blocks) - n_down]:
                dep = gate(*blk)
            g = len(blocks) - n_down
        for n in range(n_down):
            if do_down:
                down(n, dep)
            if do_gate:
                dep = gate(*blocks[g + n])

    steady = jnp.logical_and(e >= 2, e < n_chunks)

    @pl.when(e == 0)
    def _():
        stages(sc0_ref, None, None, None, do_gate=False, do_down=False)

    @pl.when(e == 1)
    def _():
        stages(sc1_ref, sc0_ref, a0_ref, None, do_down=False)

    @pl.when(jnp.logical_and(steady, e % 2 == 0))
    def _():
        stages(sc0_ref, sc1_ref, a1_ref, a0_ref)

    @pl.when(jnp.logical_and(steady, e % 2 == 1))
    def _():
        stages(sc1_ref, sc0_ref, a0_ref, a1_ref)

    @pl.when(e == n_chunks)
    def _():
        stages(None, sc1_ref, a1_ref, a0_ref, do_up=False)

    @pl.when(e == n_chunks + 1)
    def _():
        stages(None, None, None, a1_ref, do_up=False, do_gate=False)


def peer_dense(hnt, u, v, n1, e1, r2, e2, h1, tt=512, ec=512):
    D, T = hnt.shape
    E = u.shape[0]
    H, _, nk, _ = n1.shape
    tt = min(tt, T)
    n_chunks = E // ec
    assert n_chunks % 2 == 0 and n_chunks >= 2
    tok = pl.BlockSpec((H, tt // LANES, nk, LANES), lambda t, e: (0, t, 0, 0))
    ptok = pl.BlockSpec((H, tt // LANES, nk // 2, LANES), lambda t, e: (0, t, 0, 0))
    return pl.pallas_call(
        functools.partial(_peer_dense_body, nk=nk, rows_per_chunk=ec // nk, n_chunks=n_chunks),
        grid=(T // tt, n_chunks + 2),
        in_specs=[pl.BlockSpec((D, tt), lambda t, e: (0, t)),
                  pl.BlockSpec((ec, D), lambda t, e: (jnp.minimum(e, n_chunks - 1), 0)),
                  pl.BlockSpec((ec, D), lambda t, e: (jnp.clip(e - 2, 0, n_chunks - 1), 0)),
                  tok, tok, ptok, ptok,
                  pl.BlockSpec((tt, D), lambda t, e: (t, 0))],
        out_specs=pl.BlockSpec((tt, D), lambda t, e: (t, 0)),
        out_shape=jax.ShapeDtypeStruct((T, D), F32),
        scratch_shapes=[pltpu.VMEM((ec, tt), F32), pltpu.VMEM((ec, tt), F32),
                        pltpu.VMEM((tt, ec), BF16), pltpu.VMEM((tt, ec), BF16)],
        compiler_params=_params("parallel", "arbitrary"),
        name="peer_dense",
    )(hnt, u, v, n1, e1, r2, e2, h1)


def kernel(x, norm1_g, w_in, w_fourier, w_attn, q_norm_g, k_norm_g, lambda_q1, lambda_k1,
           lambda_q2, lambda_k2, subln_g, w_out, norm2_g, w_query, sub_keys, expert_u, expert_v):
    B, S, D = x.shape
    T = B * S
    depth = w_in.shape[0]
    dh = D // (2 * A_HEADS)
    f_width = w_fourier.shape[1]
    qk_width = A_HEADS * 2 * dh
    v_width = w_attn.shape[1]
    o_q = f_width
    o_k = o_q + qk_width
    o_v = o_k + qk_width
    o_g = o_v + v_width
    h = x.reshape(T, D)
    for i in range(depth):
        lam_init = 0.8 - 0.6 * math.exp(-0.3 * i)
        xn = rmsnorm_rows(h, norm1_g[i])
        w_in_b = w_in[i].astype(BF16)
        qk_gain = jnp.concatenate([jnp.tile(q_norm_g[i].astype(F32) * dh ** -0.5, 2 * A_HEADS),
                                   jnp.tile(k_norm_g[i].astype(F32), 2 * A_HEADS)]).reshape(1, 2 * qk_width)
        z = project(xn, w_in_b, 0, f_width, "cast", out_dtype=F32)
        qk = project(xn, w_in_b, o_q, 2 * qk_width, "headnorm", gain=qk_gain, group=dh)
        vv = project(xn, w_in_b, o_v, v_width, "cast")
        gates = project(xn, w_in_b, o_g, 2 * D, "sigmoid")

        fm = fourier_mix(z, B, S)
        lamv = jnp.stack([lambda_q1[i], lambda_k1[i], lambda_q2[i], lambda_k2[i]]).astype(F32)
        bound = (jnp.max(jnp.abs(q_norm_g[i])) * jnp.max(jnp.abs(k_norm_g[i]))).astype(F32) * (1.01 * dh ** 0.5)
        attend = functools.partial(diff_attention, qk, vv, lamv, subln_g[i], bound, B, S, lam_init)
        on = lax.cond(bound <= MAX_SAFE_SHIFT, lambda: attend(True), lambda: attend(False))

        mixed = gated_merge(fm, on, gates, w_fourier[i].astype(BF16), w_attn[i].astype(BF16))
        h1, hn, hnt = out_projection(mixed, h, w_out[i].astype(BF16), norm2_g[i])

        n1, e1, r2, e2 = peer_query(hn, w_query[i].astype(BF16), sub_keys[i].astype(BF16))
        h = peer_dense(hnt, expert_u[i].astype(BF16), expert_v[i].astype(BF16), n1, e1, r2, e2, h1)
    return h.reshape(B, S, D)
```

```python
import functools
import math

import numpy as np
import jax
import jax.numpy as jnp
from jax import lax
from jax.experimental import pallas as pl
from jax.experimental.pallas import tpu as pltpu

EPS = 1e-6
F_GROUPS = 4
A_HEADS = 8
P_HEADS = 8
P_TOPK = 16
LANES = 128
SUBLANES = 8
MXU_COLS = 256
BF16_ROWS = 16
UP_K = 256
MAX_SAFE_SHIFT = 40.0
FFT_INNER = 128
VMEM_LIMIT_BYTES = 56 * 1024 * 1024

F32 = jnp.float32
BF16 = jnp.bfloat16
NEG_INF = float("-inf")


def _params(*semantics):
    return pltpu.CompilerParams(dimension_semantics=semantics, vmem_limit_bytes=VMEM_LIMIT_BYTES)


def _rmsnorm_body(x_ref, g_ref, o_ref):
    x = x_ref[...]
    ms = jnp.mean(x * x, axis=-1, keepdims=True)
    o_ref[...] = (x * lax.rsqrt(ms + EPS) * g_ref[...]).astype(o_ref.dtype)


def rmsnorm_rows(x2d, g, tm=512):
    T, D = x2d.shape
    return pl.pallas_call(
        _rmsnorm_body,
        grid=(T // tm,),
        in_specs=[pl.BlockSpec((tm, D), lambda i: (i, 0)),
                  pl.BlockSpec((1, D), lambda i: (0, 0))],
        out_specs=pl.BlockSpec((tm, D), lambda i: (i, 0)),
        out_shape=jax.ShapeDtypeStruct((T, D), BF16),
        compiler_params=_params("parallel"),
        name="rmsnorm",
    )(x2d, g.reshape(1, D).astype(F32))


def _proj_body(x_ref, w_ref, g_ref, o_ref, wb_ref, *, mode, group):
    @pl.when(pl.program_id(1) == 0)
    def _():
        wb_ref[...] = w_ref[...].astype(BF16)

    acc = jnp.dot(x_ref[...], wb_ref[...], preferred_element_type=F32)
    if mode == "cast":
        o_ref[...] = acc.astype(o_ref.dtype)
    elif mode == "sigmoid":
        o_ref[...] = jax.nn.sigmoid(acc).astype(o_ref.dtype)
    else:
        for s in range(acc.shape[1] // group):
            blk = acc[:, s * group:(s + 1) * group]
            ms = jnp.mean(blk * blk, axis=-1, keepdims=True)
            o_ref[:, s * group:(s + 1) * group] = (
                blk * lax.rsqrt(ms + EPS) * g_ref[:, s * group:(s + 1) * group]).astype(o_ref.dtype)


def project(x, w, col0, ncols, mode, gain=None, group=LANES, out_dtype=BF16, tm=1024, tn=1024):
    T, K = x.shape
    tm = min(tm, T)
    if gain is None:
        gain = jnp.ones((1, ncols), F32)
    off = col0 // tn
    return pl.pallas_call(
        functools.partial(_proj_body, mode=mode, group=group),
        grid=(ncols // tn, T // tm),
        in_specs=[pl.BlockSpec((tm, K), lambda j, i: (i, 0)),
                  pl.BlockSpec((K, tn), lambda j, i: (0, off + j)),
                  pl.BlockSpec((1, tn), lambda j, i: (0, j))],
        out_specs=pl.BlockSpec((tm, tn), lambda j, i: (i, j)),
        out_shape=jax.ShapeDtypeStruct((T, ncols), out_dtype),
        scratch_shapes=[pltpu.VMEM((K, tn), BF16)],
        compiler_params=_params("parallel", "arbitrary"),
        name="proj_" + mode,
    )(x, w, gain)


def _fourier_constants(S, C):
    Na = S // FFT_INNER
    eye = np.eye(SUBLANES)
    n2 = np.arange(C, dtype=np.int64)
    ang = 2.0 * np.pi * ((n2[:, None] * n2[None, :]) % C) / C
    w0 = np.concatenate([np.cos(ang), -np.sin(ang)], axis=1)
    a = np.arange(Na, dtype=np.int64)
    phi = 2.0 * np.pi * ((a[:, None] * a[None, :]) % Na) / Na
    r = np.block([[np.cos(phi), np.sin(phi)], [-np.sin(phi), np.cos(phi)]])
    r4 = r.reshape(2, Na, 2, Na)
    rk = np.einsum('qcpa,xy->qxcpay', r4, eye).reshape(2 * SUBLANES * Na, 2 * Na * SUBLANES)
    d = np.arange(FFT_INNER, dtype=np.int64)
    psi = 2.0 * np.pi * ((d[:, None] * d[None, :]) % FFT_INNER) / FFT_INNER
    f2 = np.stack([np.cos(psi), np.sin(psi)], axis=1) / math.sqrt(S * C)
    kf = np.einsum('drb,xy->dxrby', f2, eye).reshape(FFT_INNER * SUBLANES, 2 * FFT_INNER * SUBLANES)
    return (jnp.asarray(w0, BF16), jnp.asarray(rk, BF16), jnp.asarray(kf, BF16))


def _twiddles(S):
    Na = S // FFT_INNER
    b = lax.broadcasted_iota(jnp.int32, (FFT_INNER, Na), 0)
    c = lax.broadcasted_iota(jnp.int32, (FFT_INNER, Na), 1)
    th = ((b * c) % S).astype(F32) * (2.0 * math.pi / S)
    shape = (FFT_INNER, Na, LANES)
    return (jnp.broadcast_to(jnp.cos(th)[:, :, None], shape), jnp.broadcast_to(jnp.sin(th)[:, :, None], shape))


def _fourier_outer_body(z_ref, w0_ref, rk_ref, o_ref, *, groups):
    na, sub, w = z_ref.shape
    c = w // groups
    x = z_ref[...].reshape(na * sub, w).astype(BF16)
    ps, qs = [], []
    for g in range(groups):
        pq = jnp.dot(x[:, g * c:(g + 1) * c], w0_ref[...], preferred_element_type=F32)
        ps.append(pq[:, :c])
        qs.append(pq[:, c:])
    x1 = jnp.concatenate([jnp.concatenate(ps, axis=1), jnp.concatenate(qs, axis=1)], axis=0)
    a = jnp.dot(rk_ref[...], x1.astype(BF16), preferred_element_type=F32)
    o_ref[...] = a.reshape(2, sub, na, w)


def _fourier_inner_body(a_ref, tc_ref, ts_ref, kf_ref, o_ref):
    _, nb, sub, c = a_ref.shape
    ar = a_ref[0].reshape(nb * sub, c)
    ai = a_ref[1].reshape(nb * sub, c)
    tc = jnp.tile(tc_ref[...].reshape(nb * sub, LANES), (1, c // LANES))
    ts = jnp.tile(ts_ref[...].reshape(nb * sub, LANES), (1, c // LANES))
    x3 = jnp.concatenate([ar * tc + ai * ts, ai * tc - ar * ts], axis=0).astype(BF16)
    y = jnp.dot(kf_ref[...], x3, preferred_element_type=F32)
    o_ref[...] = y.reshape(nb, sub, c)


def fourier_mix(z2d, B, S):
    T, W = z2d.shape
    G = F_GROUPS
    C = W // G
    Na = S // FFT_INNER
    assert Na % SUBLANES == 0
    w0, rk, kf = _fourier_constants(S, C)
    tc, ts = _twiddles(S)
    a5 = pl.pallas_call(
        functools.partial(_fourier_outer_body, groups=G),
        grid=(B, FFT_INNER // SUBLANES),
        in_specs=[pl.BlockSpec((None, Na, SUBLANES, W), lambda b, j: (b, 0, j, 0)),
                  pl.BlockSpec(w0.shape, lambda b, j: (0, 0)),
                  pl.BlockSpec(rk.shape, lambda b, j: (0, 0))],
        out_specs=pl.BlockSpec((None, 2, SUBLANES, Na, W), lambda b, j: (b, 0, j, 0, 0)),
        out_shape=jax.ShapeDtypeStruct((B, 2, FFT_INNER, Na, W), F32),
        compiler_params=_params("parallel", "arbitrary"),
        name="fourier_outer",
    )(z2d.reshape(B, Na, FFT_INNER, W), w0, rk)
    y = pl.pallas_call(
        _fourier_inner_body,
        grid=(B, Na // SUBLANES, G),
        in_specs=[pl.BlockSpec((None, 2, FFT_INNER, SUBLANES, C), lambda b, j, g: (b, 0, 0, j, g)),
                  pl.BlockSpec((FFT_INNER, SUBLANES, LANES), lambda b, j, g: (0, j, 0)),
                  pl.BlockSpec((FFT_INNER, SUBLANES, LANES), lambda b, j, g: (0, j, 0)),
                  pl.BlockSpec(kf.shape, lambda b, j, g: (0, 0))],
        out_specs=pl.BlockSpec((None, FFT_INNER, SUBLANES, C), lambda b, j, g: (b, 0, j, g)),
        out_shape=jax.ShapeDtypeStruct((B, FFT_INNER, Na, W), F32),
        compiler_params=_params("parallel", "parallel", "arbitrary"),
        name="fourier_inner",
    )(a5, tc, ts, kf)
    return y.reshape(T, W)


def _attn_body(slope_ref, shift_ref, lamv_ref, g_ref, qt_ref, ka_ref, q_ref, k_ref, v_ref, o_ref,
               m_ref, l_ref, acc_ref, *, tq, tk, dh, lam_init, fixed_shift):
    h = pl.program_id(1)
    qi = pl.program_id(2)
    ki = pl.program_id(3)
    dv = 2 * dh

    @pl.when(ki == 0)
    def _():
        m_ref[...] = jnp.full(m_ref.shape, NEG_INF, F32)
        l_ref[...] = jnp.zeros(l_ref.shape, F32)
        acc_ref[...] = jnp.zeros(acc_ref.shape, F32)

    slope = slope_ref[h]

    def online_softmax_update(scores):
        v = v_ref[...]
        for c in range(2):
            s = scores[c]
            m_prev = m_ref[c]
            m_new = jnp.maximum(m_prev, jnp.max(s, axis=-1, keepdims=True))
            alpha = jnp.exp(m_prev - m_new)
            p = jnp.exp(s - jnp.tile(m_new, (1, tk // LANES)))
            l_ref[c] = alpha * l_ref[c] + jnp.sum(p, axis=-1, keepdims=True)
            acc_ref[c] = (jnp.tile(alpha, (1, dv // LANES)) * acc_ref[c]
                          + jnp.dot(p.astype(BF16), v, preferred_element_type=F32))
            m_ref[c] = m_new

    def shifted_update(scores):
        v = v_ref[...]
        for c in range(2):
            p = jnp.exp(scores[c])
            part = p[:, 0:LANES]
            for t in range(1, tk // LANES):
                part = part + p[:, t * LANES:(t + 1) * LANES]
            l_ref[c] += part
            acc_ref[c] += jnp.dot(p.astype(BF16), v, preferred_element_type=F32)

    update = shifted_update if fixed_shift else online_softmax_update
    nt = (((1,), (1,)), ((), ()))

    @pl.when(qi != ki)
    def _():
        off = (qi - ki) * tq
        sgn = jnp.where(off > 0, slope, -slope)
        qa = ((qt_ref[0] + off.astype(F32) * qt_ref[1]) * sgn + qt_ref[2]).astype(BF16)
        ka = ka_ref[...]
        q = q_ref[...]
        k = k_ref[...]
        scores = []
        for c in range(2):
            qc = jnp.concatenate([q[:, c * dh:(c + 1) * dh], qa], axis=1)
            kc = jnp.concatenate([k[:, c * dh:(c + 1) * dh], ka], axis=1)
            scores.append(lax.dot_general(qc, kc, nt, preferred_element_type=F32))
        update(scores)

    @pl.when(qi == ki)
    def _():
        row = lax.broadcasted_iota(jnp.int32, (tq, tk), 0)
        col = lax.broadcasted_iota(jnp.int32, (tq, tk), 1)
        bias = jnp.abs(row - col).astype(F32) * (-slope) - shift_ref[0]
        q = q_ref[...]
        k = k_ref[...]
        scores = [lax.dot_general(q[:, c * dh:(c + 1) * dh], k[:, c * dh:(c + 1) * dh], nt,
                                  preferred_element_type=F32) + bias for c in range(2)]
        update(scores)

    @pl.when(ki == pl.num_programs(3) - 1)
    def _():
        lv = lamv_ref[...]
        lam = (jnp.exp(jnp.sum(lv[0:1] * lv[1:2], axis=-1, keepdims=True))
               - jnp.exp(jnp.sum(lv[2:3] * lv[3:4], axis=-1, keepdims=True)) + lam_init)
        if fixed_shift:
            l0 = jnp.sum(l_ref[0], axis=-1, keepdims=True)
            l1 = jnp.sum(l_ref[1], axis=-1, keepdims=True)
        else:
            l0 = jnp.tile(l_ref[0], (1, dv // LANES))
            l1 = jnp.tile(l_ref[1], (1, dv // LANES))
        o = acc_ref[0] / l0 - lam * (acc_ref[1] / l1)
        ms = jnp.mean(o * o, axis=-1, keepdims=True)
        o_ref[...] = (o * lax.rsqrt(ms + EPS) * g_ref[...] * (1.0 - lam_init)).astype(o_ref.dtype)


SHIFT_LANES = (4, 5, 6)


def _split_bf16(x, parts):
    out = []
    for _ in range(parts):
        p = x.astype(BF16).astype(F32)
        out.append(p)
        x = x - p
    return out


def _alibi_templates(tq, tk):
    r = np.arange(tq)
    qt = np.zeros((2, tq, LANES), np.float32)
    qt[0, :, 0] = -(r // 256) * 256
    qt[0, :, 1] = -(r % 256)
    qt[0, :, 2] = 1.0
    qt[0, :, 3] = 1.0
    qt[1, :, 0] = -1.0
    c = np.arange(tk)
    ka = np.zeros((tk, LANES), np.float32)
    ka[:, 0] = 1.0
    ka[:, 1] = 1.0
    ka[:, 2] = c % 256
    ka[:, 3] = (c // 256) * 256
    ka[:, SHIFT_LANES[0]:SHIFT_LANES[-1] + 1] = 1.0
    return jnp.asarray(qt, F32), jnp.asarray(ka, BF16)


def diff_attention(qk2d, v2d, lamv, subln_g, shift, B, S, lam_init, fixed_shift, tile=1024):
    T, W = qk2d.shape
    H = A_HEADS
    dh = W // (4 * H)
    dv = 2 * dh
    tq = tk = min(tile, S)
    nq, nk = S // tq, S // tk
    slopes = 2.0 ** (-8.0 * np.arange(1, H + 1) / H)
    assert np.all(np.log2(slopes) == np.round(np.log2(slopes))) and S <= 8192
    qt, ka = _alibi_templates(tq, tk)
    parts = _split_bf16(shift if fixed_shift else jnp.zeros((), F32), len(SHIFT_LANES))
    lane = lax.broadcasted_iota(jnp.int32, (1, tq, LANES), 2)
    shift_plane = sum(jnp.where(lane == ln, -p, 0.0) for ln, p in zip(SHIFT_LANES, parts))
    qt = jnp.concatenate([qt, shift_plane.astype(F32)], axis=0)
    applied = sum(parts).reshape(1)
    return pl.pallas_call(
        functools.partial(_attn_body, tq=tq, tk=tk, dh=dh, lam_init=lam_init, fixed_shift=fixed_shift),
        grid=(B, H, nq, nk),
        in_specs=[pl.BlockSpec(memory_space=pltpu.SMEM),
                  pl.BlockSpec(memory_space=pltpu.SMEM),
                  pl.BlockSpec((4, dh), lambda b, h, i, j: (0, 0)),
                  pl.BlockSpec((1, dv), lambda b, h, i, j: (0, 0)),
                  pl.BlockSpec((3, tq, LANES), lambda b, h, i, j: (0, 0, 0)),
                  pl.BlockSpec((tk, LANES), lambda b, h, i, j: (0, 0)),
                  pl.BlockSpec((tq, dv), lambda b, h, i, j: (b * nq + i, h)),
                  pl.BlockSpec((tk, dv), lambda b, h, i, j: (b * nk + j, H + h)),
                  pl.BlockSpec((tk, dv), lambda b, h, i, j: (b * nk + j, h))],
        out_specs=pl.BlockSpec((tq, dv), lambda b, h, i, j: (b * nq + i, h)),
        out_shape=jax.ShapeDtypeStruct((T, H * dv), BF16),
        scratch_shapes=[pltpu.VMEM((2, tq, LANES), F32), pltpu.VMEM((2, tq, LANES), F32),
                        pltpu.VMEM((2, tq, dv), F32)],
        compiler_params=_params("parallel", "parallel", "parallel", "arbitrary"),
        name="diff_attention_shifted" if fixed_shift else "diff_attention_online",
    )(jnp.asarray(slopes, F32), applied, lamv, subln_g.reshape(1, dv).astype(F32), qt, ka, qk2d, qk2d, v2d)


def _merge_body(fm_ref, on_ref, gf_ref, ga_ref, wf_ref, wa_ref, o_ref):
    yf = jnp.dot(fm_ref[...].astype(BF16), wf_ref[...], preferred_element_type=F32)
    ya = jnp.dot(on_ref[...], wa_ref[...], preferred_element_type=F32)
    o_ref[...] = (gf_ref[...].astype(F32) * yf + ga_ref[...].astype(F32) * ya).astype(o_ref.dtype)


def gated_merge(fm, on, gates, wf, wa, tm=256):
    T, D = on.shape
    tm = min(tm, T)
    return pl.pallas_call(
        _merge_body,
        grid=(T // tm,),
        in_specs=[pl.BlockSpec((tm, fm.shape[1]), lambda i: (i, 0)),
                  pl.BlockSpec((tm, D), lambda i: (i, 0)),
                  pl.BlockSpec((tm, D), lambda i: (i, 0)),
                  pl.BlockSpec((tm, D), lambda i: (i, 1)),
                  pl.BlockSpec(wf.shape, lambda i: (0, 0)),
                  pl.BlockSpec(wa.shape, lambda i: (0, 0))],
        out_specs=pl.BlockSpec((tm, D), lambda i: (i, 0)),
        out_shape=jax.ShapeDtypeStruct((T, D), BF16),
        compiler_params=_params("parallel"),
        name="gated_merge",
    )(fm, on, gates, gates, wf, wa)


def _out_body(mx_ref, x_ref, wo_ref, g_ref, h_ref, hn_ref, hnt_ref):
    h = x_ref[...] + jnp.dot(mx_ref[...], wo_ref[...], preferred_element_type=F32)
    h_ref[...] = h
    ms = jnp.mean(h * h, axis=-1, keepdims=True)
    hn = h * lax.rsqrt(ms + EPS) * g_ref[...]
    hn_ref[...] = hn.astype(hn_ref.dtype)
    hnt_ref[...] = hn.T.astype(hnt_ref.dtype)


def out_projection(mixed, x2d, wo, g2, tm=512):
    T, D = x2d.shape
    tm = min(tm, T)
    return pl.pallas_call(
        _out_body,
        grid=(T // tm,),
        in_specs=[pl.BlockSpec((tm, D), lambda i: (i, 0)),
                  pl.BlockSpec((tm, D), lambda i: (i, 0)),
                  pl.BlockSpec((D, D), lambda i: (0, 0)),
                  pl.BlockSpec((1, D), lambda i: (0, 0))],
        out_specs=[pl.BlockSpec((tm, D), lambda i: (i, 0)),
                   pl.BlockSpec((tm, D), lambda i: (i, 0)),
                   pl.BlockSpec((D, tm), lambda i: (0, i))],
        out_shape=[jax.ShapeDtypeStruct((T, D), F32), jax.ShapeDtypeStruct((T, D), BF16),
                   jax.ShapeDtypeStruct((D, T), BF16)],
        compiler_params=_params("parallel"),
        name="out_projection",
    )(mixed, x2d, wo, g2.reshape(1, D).astype(F32))


def _extract_top(x, dst_ref, n, want_rank=False):
    rank = jnp.full(x.shape, float(n), F32) if want_rank else None
    for r in range(n):
        mx = jnp.max(x, axis=0, keepdims=True)
        dst_ref[pl.ds(r, 1), :] = mx
        hit = x >= mx
        if want_rank:
            rank = jnp.where(hit, float(r), rank)
        x = jnp.where(hit, NEG_INF, x)
    return rank


def _peer_query_body(hn_ref, wq_ref, keys_ref, n1_ref, e1_ref, r2_ref, e2_ref, sv_ref, cs_ref):
    qp = jnp.dot(hn_ref[...], wq_ref[...], preferred_element_type=F32).astype(BF16)
    half = keys_ref.shape[-1]
    nt = (((1,), (1,)), ((), ()))
    for h in range(P_HEADS):
        lo = 2 * h * half
        st1 = lax.dot_general(keys_ref[h, 0], qp[:, lo:lo + half], nt, preferred_element_type=F32)
        st2 = lax.dot_general(keys_ref[h, 1], qp[:, lo + half:lo + 2 * half], nt,
                              preferred_element_type=F32)
        _extract_top(st1, sv_ref.at[0], P_TOPK)
        r2 = _extract_top(st2, sv_ref.at[1], P_TOPK, want_rank=True)
        sv1 = sv_ref[0]
        sv2 = sv_ref[1]
        parts = [sv1[0:1] + sv2]
        for a in range(1, 8):
            parts.append(sv1[a:a + 1] + sv2[0:8])
        parts.append(sv1[8:16] + sv2[0:1])
        _extract_top(jnp.concatenate(parts, axis=0), cs_ref, P_TOPK)
        cs = cs_ref[...]
        tau = cs[P_TOPK - 1:P_TOPK]
        z = jnp.sum(jnp.exp(cs - cs[0:1]), axis=0, keepdims=True)
        n1 = jnp.zeros(st1.shape, F32)
        for a in range(P_TOPK):
            va = sv1[a:a + 1]
            n_a = jnp.sum(jnp.where(va + sv2 >= tau, 1.0, 0.0), axis=0, keepdims=True)
            n1 = jnp.where(st1 == va, n_a, n1)
        e1 = jnp.exp(st1 - sv1[0:1])
        e2 = jnp.exp(st2 - sv2[0:1]) / z
        for tb in range(st1.shape[1] // LANES):
            cols = slice(tb * LANES, (tb + 1) * LANES)
            n1_ref[h, tb] = n1[:, cols]
            e1_ref[h, tb] = e1[:, cols]
            r2_ref[h, tb] = pltpu.bitcast(r2[:, cols].astype(BF16), jnp.uint32)
            e2_ref[h, tb] = pltpu.bitcast(e2[:, cols].astype(BF16), jnp.uint32)


def peer_query(hn, wq, keys, tt=256):
    T, D = hn.shape
    H, _, nk, half = keys.shape
    tt = min(tt, T)
    spec = pl.BlockSpec((H, tt // LANES, nk, LANES), lambda i: (0, i, 0, 0))
    rows = jax.ShapeDtypeStruct((H, T // LANES, nk, LANES), F32)
    pspec = pl.BlockSpec((H, tt // LANES, nk // 2, LANES), lambda i: (0, i, 0, 0))
    tiles = jax.ShapeDtypeStruct((H, T // LANES, nk // 2, LANES), jnp.uint32)
    return pl.pallas_call(
        _peer_query_body,
        grid=(T // tt,),
        in_specs=[pl.BlockSpec((tt, D), lambda i: (i, 0)),
                  pl.BlockSpec(wq.shape, lambda i: (0, 0)),
                  pl.BlockSpec(keys.shape, lambda i: (0, 0, 0, 0))],
        out_specs=[spec, spec, pspec, pspec],
        out_shape=[rows, rows, tiles, tiles],
        scratch_shapes=[pltpu.VMEM((2, P_TOPK, tt), F32), pltpu.VMEM((P_TOPK, tt), F32)],
        compiler_params=_params("parallel"),
        name="peer_query",
    )(hn, wq, keys)


def _peer_dense_body(hnt_ref, u_ref, v_ref, n1_ref, e1_ref, r2_ref, e2_ref, h_ref, o_ref,
                     sc0_ref, sc1_ref, a0_ref, a1_ref, *, nk, rows_per_chunk, n_chunks):
    e = pl.program_id(1)

    @pl.when(e == 0)
    def _():
        o_ref[...] = h_ref[...]

    def stages(sc_w, sc_r, a_w, a_r, do_up=True, do_gate=True, do_down=True):
        tt = o_ref.shape[0]
        chunk = e - 1

        def anchored(lhs, dep):
            if dep is None:
                return lhs
            z = jnp.tile(dep, (BF16_ROWS // dep.shape[0], lhs.shape[1] // LANES)).astype(BF16)
            return jnp.concatenate([lhs[0:BF16_ROWS, :] + z, lhs[BF16_ROWS:, :]], axis=0)

        def up_piece(half, kb, dep):
            rows = slice(half * (tt // 2), (half + 1) * (tt // 2))
            ks = slice(kb * UP_K, (kb + 1) * UP_K)
            return jnp.dot(anchored(u_ref[:, ks], dep), hnt_ref[ks, rows], preferred_element_type=F32)

        def gate(tb, ii):
            i = chunk * rows_per_chunk + ii
            def key_row(ref, h):
                row = jnp.broadcast_to(ref[h, tb, pl.ds(i, 1), :], (BF16_ROWS, LANES)).astype(BF16)
                return jnp.tile(row, (nk // BF16_ROWS, 1))

            w = jnp.zeros((nk, LANES), BF16)
            for h in range(P_HEADS):
                w = w + jnp.where(pltpu.bitcast(r2_ref[h, tb], BF16) < key_row(n1_ref, h),
                                  pltpu.bitcast(e2_ref[h, tb], BF16) * key_row(e1_ref, h), 0.0)
            x = sc_r[ii * nk:(ii + 1) * nk, tb * LANES:(tb + 1) * LANES]
            act = 0.5 * x * (1.0 + lax.erf(x * (1.0 / math.sqrt(2.0))))
            prod = (act * w.astype(F32)).T
            a_w[tb * LANES:(tb + 1) * LANES, ii * nk:(ii + 1) * nk] = prod.astype(BF16)
            bits = pltpu.bitcast(prod[0:8, :], jnp.uint32)
            bits = lax.shift_right_logical(lax.shift_right_logical(bits, jnp.uint32(16)), jnp.uint32(16))
            return pltpu.bitcast(bits, F32)

        def down(n, dep):
            cols = slice(n * MXU_COLS, (n + 1) * MXU_COLS)
            o_ref[:, cols] += jnp.dot(anchored(a_r[...], dep), v_ref[:, cols], preferred_element_type=F32)

        blocks = [(tb, ii) for tb in range(tt // LANES) for ii in range(rows_per_chunk)]
        n_down = o_ref.shape[1] // MXU_COLS
        n_kb = u_ref.shape[1] // UP_K
        per_gate = 2 * n_kb // (len(blocks) - n_down)
        dep = None
        g = 0
        for half in range(2 if do_up else 0):
            acc = None
            for kb in range(n_kb):
                piece = up_piece(half, kb, dep)
                acc = piece if acc is None else acc + piece
                if do_gate and (half * n_kb + kb + 1) % per_gate == 0:
                    dep = gate(*blocks[g])
                    g += 1
            sc_w[:, half * (tt // 2):(half + 1) * (tt // 2)] = acc
        if do_gate and not do_up:
            for blk in blocks[:len(blocks) - n_down]:
                dep = gate(*blk)
            g = len(blocks) - n_down
        for n in range(n_down):
            if do_down:
                down(n, dep)
            if do_gate:
                dep = gate(*blocks[g + n])

    steady = jnp.logical_and(e >= 2, e < n_chunks)

    @pl.when(e == 0)
    def _():
        stages(sc0_ref, None, None, None, do_gate=False, do_down=False)

    @pl.when(e == 1)
    def _():
        stages(sc1_ref, sc0_ref, a0_ref, None, do_down=False)

    @pl.when(jnp.logical_and(steady, e % 2 == 0))
    def _():
        stages(sc0_ref, sc1_ref, a1_ref, a0_ref)

    @pl.when(jnp.logical_and(steady, e % 2 == 1))
    def _():
        stages(sc1_ref, sc0_ref, a0_ref, a1_ref)

    @pl.when(e == n_chunks)
    def _():
        stages(None, sc1_ref, a1_ref, a0_ref, do_up=False)

    @pl.when(e == n_chunks + 1)
    def _():
        stages(None, None, None, a1_ref, do_up=False, do_gate=False)


def peer_dense(hnt, u, v, n1, e1, r2, e2, h1, tt=512, ec=512):
    D, T = hnt.shape
    E = u.shape[0]
    H, _, nk, _ = n1.shape
    tt = min(tt, T)
    n_chunks = E // ec
    assert n_chunks % 2 == 0 and n_chunks >= 2
    tok = pl.BlockSpec((H, tt // LANES, nk, LANES), lambda t, e: (0, t, 0, 0))
    ptok = pl.BlockSpec((H, tt // LANES, nk // 2, LANES), lambda t, e: (0, t, 0, 0))
    return pl.pallas_call(
        functools.partial(_peer_dense_body, nk=nk, rows_per_chunk=ec // nk, n_chunks=n_chunks),
        grid=(T // tt, n_chunks + 2),
        in_specs=[pl.BlockSpec((D, tt), lambda t, e: (0, t)),
                  pl.BlockSpec((ec, D), lambda t, e: (jnp.minimum(e, n_chunks - 1), 0)),
                  pl.BlockSpec((ec, D), lambda t, e: (jnp.clip(e - 2, 0, n_chunks - 1), 0)),
                  tok, tok, ptok, ptok,
                  pl.BlockSpec((tt, D), lambda t, e: (t, 0))],
        out_specs=pl.BlockSpec((tt, D), lambda t, e: (t, 0)),
        out_shape=jax.ShapeDtypeStruct((T, D), F32),
        scratch_shapes=[pltpu.VMEM((ec, tt), F32), pltpu.VMEM((ec, tt), F32),
                        pltpu.VMEM((tt, ec), BF16), pltpu.VMEM((tt, ec), BF16)],
        compiler_params=_params("parallel", "arbitrary"),
        name="peer_dense",
    )(hnt, u, v, n1, e1, r2, e2, h1)


def kernel(x, norm1_g, w_in, w_fourier, w_attn, q_norm_g, k_norm_g, lambda_q1, lambda_k1,
           lambda_q2, lambda_k2, subln_g, w_out, norm2_g, w_query, sub_keys, expert_u, expert_v):
    B, S, D = x.shape
    T = B * S
    depth = w_in.shape[0]
    dh = D // (2 * A_HEADS)
    f_width = w_fourier.shape[1]
    qk_width = A_HEADS * 2 * dh
    v_width = w_attn.shape[1]
    o_q = f_width
    o_k = o_q + qk_width
    o_v = o_k + qk_width
    o_g = o_v + v_width
    h = x.reshape(T, D)
    for i in range(depth):
        lam_init = 0.8 - 0.6 * math.exp(-0.3 * i)
        xn = rmsnorm_rows(h, norm1_g[i])
        w_in_b = w_in[i]
        qk_gain = jnp.concatenate([jnp.tile(q_norm_g[i].astype(F32) * dh ** -0.5, 2 * A_HEADS),
                                   jnp.tile(k_norm_g[i].astype(F32), 2 * A_HEADS)]).reshape(1, 2 * qk_width)
        z = project(xn, w_in_b, 0, f_width, "cast", out_dtype=F32)
        qk = project(xn, w_in_b, o_q, 2 * qk_width, "headnorm", gain=qk_gain, group=dh)
        vv = project(xn, w_in_b, o_v, v_width, "cast")
        gates = project(xn, w_in_b, o_g, 2 * D, "sigmoid")

        fm = fourier_mix(z, B, S)
        lamv = jnp.stack([lambda_q1[i], lambda_k1[i], lambda_q2[i], lambda_k2[i]]).astype(F32)
        bound = (jnp.max(jnp.abs(q_norm_g[i])) * jnp.max(jnp.abs(k_norm_g[i]))).astype(F32) * (1.01 * dh ** 0.5)
        attend = functools.partial(diff_attention, qk, vv, lamv, subln_g[i], bound, B, S, lam_init)
        on = lax.cond(bound <= MAX_SAFE_SHIFT, lambda: attend(True), lambda: attend(False))

        mixed = gated_merge(fm, on, gates, w_fourier[i].astype(BF16), w_attn[i].astype(BF16))
        h1, hn, hnt = out_projection(mixed, h, w_out[i].astype(BF16), norm2_g[i])

        n1, e1, r2, e2 = peer_query(hn, w_query[i].astype(BF16), sub_keys[i].astype(BF16))
        h = peer_dense(hnt, expert_u[i].astype(BF16), expert_v[i].astype(BF16), n1, e1, r2, e2, h1)
    return h.reshape(B, S, D)
```

```python
import functools
import math

import numpy as np
import jax
import jax.numpy as jnp
from jax import lax
from jax.experimental import pallas as pl
from jax.experimental.pallas import tpu as pltpu

EPS = 1e-6
F_GROUPS = 4
A_HEADS = 8
P_HEADS = 8
P_TOPK = 16
LANES = 128
SUBLANES = 8
MXU_COLS = 256
BF16_ROWS = 16
UP_K = 256
MAX_SAFE_SHIFT = 40.0
FFT_INNER = 128
VMEM_LIMIT_BYTES = 56 * 1024 * 1024

F32 = jnp.float32
BF16 = jnp.bfloat16
NEG_INF = float("-inf")


def _params(*semantics):
    return pltpu.CompilerParams(dimension_semantics=semantics, vmem_limit_bytes=VMEM_LIMIT_BYTES)


def _rmsnorm_body(x_ref, g_ref, o_ref):
    x = x_ref[...]
    ms = jnp.mean(x * x, axis=-1, keepdims=True)
    o_ref[...] = (x * lax.rsqrt(ms + EPS) * g_ref[...]).astype(o_ref.dtype)


def rmsnorm_rows(x2d, g, tm=512):
    T, D = x2d.shape
    return pl.pallas_call(
        _rmsnorm_body,
        grid=(T // tm,),
        in_specs=[pl.BlockSpec((tm, D), lambda i: (i, 0)),
                  pl.BlockSpec((1, D), lambda i: (0, 0))],
        out_specs=pl.BlockSpec((tm, D), lambda i: (i, 0)),
        out_shape=jax.ShapeDtypeStruct((T, D), BF16),
        compiler_params=_params("parallel"),
        name="rmsnorm",
    )(x2d, g.reshape(1, D).astype(F32))


def _proj_body(x_ref, w_ref, g_ref, o_ref, wb_ref, *, mode, group):
    @pl.when(pl.program_id(1) == 0)
    def _():
        wb_ref[...] = w_ref[...].astype(BF16)

    acc = jnp.dot(x_ref[...], wb_ref[...], preferred_element_type=F32)
    if mode == "cast":
        o_ref[...] = acc.astype(o_ref.dtype)
    elif mode == "sigmoid":
        o_ref[...] = jax.nn.sigmoid(acc).astype(o_ref.dtype)
    else:
        for s in range(acc.shape[1] // group):
            blk = acc[:, s * group:(s + 1) * group]
            ms = jnp.mean(blk * blk, axis=-1, keepdims=True)
            o_ref[:, s * group:(s + 1) * group] = (
                blk * lax.rsqrt(ms + EPS) * g_ref[:, s * group:(s + 1) * group]).astype(o_ref.dtype)


def project(x, w, col0, ncols, mode, gain=None, group=LANES, out_dtype=BF16, tm=1024, tn=1024):
    T, K = x.shape
    tm = min(tm, T)
    if gain is None:
        gain = jnp.ones((1, ncols), F32)
    off = col0 // tn
    return pl.pallas_call(
        functools.partial(_proj_body, mode=mode, group=group),
        grid=(ncols // tn, T // tm),
        in_specs=[pl.BlockSpec((tm, K), lambda j, i: (i, 0)),
                  pl.BlockSpec((K, tn), lambda j, i: (0, off + j)),
                  pl.BlockSpec((1, tn), lambda j, i: (0, j))],
        out_specs=pl.BlockSpec((tm, tn), lambda j, i: (i, j)),
        out_shape=jax.ShapeDtypeStruct((T, ncols), out_dtype),
        scratch_shapes=[pltpu.VMEM((K, tn), BF16)],
        compiler_params=_params("parallel", "arbitrary"),
        name="proj_" + mode,
    )(x, w, gain)


def _fourier_constants(S, C):
    Na = S // FFT_INNER
    eye = np.eye(SUBLANES)
    n2 = np.arange(C, dtype=np.int64)
    ang = 2.0 * np.pi * ((n2[:, None] * n2[None, :]) % C) / C
    w0 = np.concatenate([np.cos(ang), -np.sin(ang)], axis=1)
    a = np.arange(Na, dtype=np.int64)
    phi = 2.0 * np.pi * ((a[:, None] * a[None, :]) % Na) / Na
    r = np.block([[np.cos(phi), np.sin(phi)], [-np.sin(phi), np.cos(phi)]])
    r4 = r.reshape(2, Na, 2, Na)
    rk = np.einsum('qcpa,xy->qxcpay', r4, eye).reshape(2 * SUBLANES * Na, 2 * Na * SUBLANES)
    d = np.arange(FFT_INNER, dtype=np.int64)
    psi = 2.0 * np.pi * ((d[:, None] * d[None, :]) % FFT_INNER) / FFT_INNER
    f2 = np.stack([np.cos(psi), np.sin(psi)], axis=1) / math.sqrt(S * C)
    kf = np.einsum('drb,xy->dxrby', f2, eye).reshape(FFT_INNER * SUBLANES, 2 * FFT_INNER * SUBLANES)
    return (jnp.asarray(w0, BF16), jnp.asarray(rk, BF16), jnp.asarray(kf, BF16))


def _twiddles(S):
    Na = S // FFT_INNER
    b = lax.broadcasted_iota(jnp.int32, (FFT_INNER, Na), 0)
    c = lax.broadcasted_iota(jnp.int32, (FFT_INNER, Na), 1)
    th = ((b * c) % S).astype(F32) * (2.0 * math.pi / S)
    shape = (FFT_INNER, Na, LANES)
    return (jnp.broadcast_to(jnp.cos(th)[:, :, None], shape), jnp.broadcast_to(jnp.sin(th)[:, :, None], shape))


def _fourier_outer_body(z_ref, w0_ref, rk_ref, o_ref, *, groups):
    na, sub, w = z_ref.shape
    c = w // groups
    x = z_ref[...].reshape(na * sub, w).astype(BF16)
    ps, qs = [], []
    for g in range(groups):
        pq = jnp.dot(x[:, g * c:(g + 1) * c], w0_ref[...], preferred_element_type=F32)
        ps.append(pq[:, :c])
        qs.append(pq[:, c:])
    x1 = jnp.concatenate([jnp.concatenate(ps, axis=1), jnp.concatenate(qs, axis=1)], axis=0)
    a = jnp.dot(rk_ref[...], x1.astype(BF16), preferred_element_type=F32)
    o_ref[...] = a.reshape(2, sub, na, w)


def _fourier_inner_body(a_ref, tc_ref, ts_ref, kf_ref, o_ref):
    _, nb, sub, c = a_ref.shape
    ar = a_ref[0].reshape(nb * sub, c)
    ai = a_ref[1].reshape(nb * sub, c)
    tc = jnp.tile(tc_ref[...].reshape(nb * sub, LANES), (1, c // LANES))
    ts = jnp.tile(ts_ref[...].reshape(nb * sub, LANES), (1, c // LANES))
    x3 = jnp.concatenate([ar * tc + ai * ts, ai * tc - ar * ts], axis=0).astype(BF16)
    y = jnp.dot(kf_ref[...], x3, preferred_element_type=F32)
    o_ref[...] = y.reshape(nb, sub, c)


def fourier_mix(z2d, B, S):
    T, W = z2d.shape
    G = F_GROUPS
    C = W // G
    Na = S // FFT_INNER
    assert Na % SUBLANES == 0
    w0, rk, kf = _fourier_constants(S, C)
    tc, ts = _twiddles(S)
    a5 = pl.pallas_call(
        functools.partial(_fourier_outer_body, groups=G),
        grid=(B, FFT_INNER // SUBLANES),
        in_specs=[pl.BlockSpec((None, Na, SUBLANES, W), lambda b, j: (b, 0, j, 0)),
                  pl.BlockSpec(w0.shape, lambda b, j: (0, 0)),
                  pl.BlockSpec(rk.shape, lambda b, j: (0, 0))],
        out_specs=pl.BlockSpec((None, 2, SUBLANES, Na, W), lambda b, j: (b, 0, j, 0, 0)),
        out_shape=jax.ShapeDtypeStruct((B, 2, FFT_INNER, Na, W), F32),
        compiler_params=_params("parallel", "arbitrary"),
        name="fourier_outer",
    )(z2d.reshape(B, Na, FFT_INNER, W), w0, rk)
    y = pl.pallas_call(
        _fourier_inner_body,
        grid=(B, Na // SUBLANES, G),
        in_specs=[pl.BlockSpec((None, 2, FFT_INNER, SUBLANES, C), lambda b, j, g: (b, 0, 0, j, g)),
                  pl.BlockSpec((FFT_INNER, SUBLANES, LANES), lambda b, j, g: (0, j, 0)),
                  pl.BlockSpec((FFT_INNER, SUBLANES, LANES), lambda b, j, g: (0, j, 0)),
                  pl.BlockSpec(kf.shape, lambda b, j, g: (0, 0))],
        out_specs=pl.BlockSpec((None, FFT_INNER, SUBLANES, C), lambda b, j, g: (b, 0, j, g)),
        out_shape=jax.ShapeDtypeStruct((B, FFT_INNER, Na, W), F32),
        compiler_params=_params("parallel", "parallel", "arbitrary"),
        name="fourier_inner",
    )(a5, tc, ts, kf)
    return y.reshape(T, W)


def _attn_body(slope_ref, shift_ref, lamv_ref, g_ref, qt_ref, ka_ref, q_ref, k_ref, v_ref, o_ref,
               m_ref, l_ref, acc_ref, bias_ref, *, tq, tk, dh, lam_init, fixed_shift):
    h = pl.program_id(1)
    qi = pl.program_id(2)
    ki = pl.program_id(3)
    dv = 2 * dh

    @pl.when(ki == 0)
    def _():
        m_ref[...] = jnp.full(m_ref.shape, NEG_INF, F32)
        l_ref[...] = jnp.zeros(l_ref.shape, F32)
        acc_ref[...] = jnp.zeros(acc_ref.shape, F32)

    slope = slope_ref[h]

    def online_softmax_update(scores):
        v = v_ref[...]
        for c in range(2):
            s = scores[c]
            m_prev = m_ref[c]
            m_new = jnp.maximum(m_prev, jnp.max(s, axis=-1, keepdims=True))
            alpha = jnp.exp(m_prev - m_new)
            p = jnp.exp(s - jnp.tile(m_new, (1, tk // LANES)))
            l_ref[c] = alpha * l_ref[c] + jnp.sum(p, axis=-1, keepdims=True)
            acc_ref[c] = (jnp.tile(alpha, (1, dv // LANES)) * acc_ref[c]
                          + jnp.dot(p.astype(BF16), v, preferred_element_type=F32))
            m_ref[c] = m_new

    def shifted_update(scores):
        v = v_ref[...]
        for c in range(2):
            p = jnp.exp(scores[c])
            part = p[:, 0:LANES]
            for t in range(1, tk // LANES):
                part = part + p[:, t * LANES:(t + 1) * LANES]
            l_ref[c] += part
            acc_ref[c] += jnp.dot(p.astype(BF16), v, preferred_element_type=F32)

    update = shifted_update if fixed_shift else online_softmax_update
    nt = (((1,), (1,)), ((), ()))

    @pl.when(qi != ki)
    def _():
        off = (qi - ki) * tq
        sgn = jnp.where(off > 0, slope, -slope)
        qa = ((qt_ref[0] + off.astype(F32) * qt_ref[1]) * sgn + qt_ref[2]).astype(BF16)
        ka = ka_ref[...]
        q = q_ref[...]
        k = k_ref[...]
        scores = []
        for c in range(2):
            qc = jnp.concatenate([q[:, c * dh:(c + 1) * dh], qa], axis=1)
            kc = jnp.concatenate([k[:, c * dh:(c + 1) * dh], ka], axis=1)
            scores.append(lax.dot_general(qc, kc, nt, preferred_element_type=F32))
        update(scores)

    @pl.when(qi == ki)
    def _():
        @pl.when(qi == 0)
        def _():
            row = lax.broadcasted_iota(jnp.int32, (tq, tk), 0)
            col = lax.broadcasted_iota(jnp.int32, (tq, tk), 1)
            bias_ref[...] = jnp.abs(row - col).astype(F32) * (-slope) - shift_ref[0]

        bias = bias_ref[...]
        q = q_ref[...]
        k = k_ref[...]
        scores = [lax.dot_general(q[:, c * dh:(c + 1) * dh], k[:, c * dh:(c + 1) * dh], nt,
                                  preferred_element_type=F32) + bias for c in range(2)]
        update(scores)

    @pl.when(ki == pl.num_programs(3) - 1)
    def _():
        lv = lamv_ref[...]
        lam = (jnp.exp(jnp.sum(lv[0:1] * lv[1:2], axis=-1, keepdims=True))
               - jnp.exp(jnp.sum(lv[2:3] * lv[3:4], axis=-1, keepdims=True)) + lam_init)
        if fixed_shift:
            l0 = jnp.sum(l_ref[0], axis=-1, keepdims=True)
            l1 = jnp.sum(l_ref[1], axis=-1, keepdims=True)
        else:
            l0 = jnp.tile(l_ref[0], (1, dv // LANES))
            l1 = jnp.tile(l_ref[1], (1, dv // LANES))
        o = acc_ref[0] / l0 - lam * (acc_ref[1] / l1)
        ms = jnp.mean(o * o, axis=-1, keepdims=True)
        o_ref[...] = (o * lax.rsqrt(ms + EPS) * g_ref[...] * (1.0 - lam_init)).astype(o_ref.dtype)


SHIFT_LANES = (4, 5, 6)


def _split_bf16(x, parts):
    out = []
    for _ in range(parts):
        p = x.astype(BF16).astype(F32)
        out.append(p)
        x = x - p
    return out


def _alibi_templates(tq, tk):
    r = np.arange(tq)
    qt = np.zeros((2, tq, LANES), np.float32)
    qt[0, :, 0] = -(r // 256) * 256
    qt[0, :, 1] = -(r % 256)
    qt[0, :, 2] = 1.0
    qt[0, :, 3] = 1.0
    qt[1, :, 0] = -1.0
    c = np.arange(tk)
    ka = np.zeros((tk, LANES), np.float32)
    ka[:, 0] = 1.0
    ka[:, 1] = 1.0
    ka[:, 2] = c % 256
    ka[:, 3] = (c // 256) * 256
    ka[:, SHIFT_LANES[0]:SHIFT_LANES[-1] + 1] = 1.0
    return jnp.asarray(qt, F32), jnp.asarray(ka, BF16)


def diff_attention(qk2d, v2d, lamv, subln_g, shift, B, S, lam_init, fixed_shift, tile=1024):
    T, W = qk2d.shape
    H = A_HEADS
    dh = W // (4 * H)
    dv = 2 * dh
    tq = tk = min(tile, S)
    nq, nk = S // tq, S // tk
    slopes = 2.0 ** (-8.0 * np.arange(1, H + 1) / H)
    assert np.all(np.log2(slopes) == np.round(np.log2(slopes))) and S <= 8192
    qt, ka = _alibi_templates(tq, tk)
    parts = _split_bf16(shift if fixed_shift else jnp.zeros((), F32), len(SHIFT_LANES))
    lane = lax.broadcasted_iota(jnp.int32, (1, tq, LANES), 2)
    shift_plane = sum(jnp.where(lane == ln, -p, 0.0) for ln, p in zip(SHIFT_LANES, parts))
    qt = jnp.concatenate([qt, shift_plane.astype(F32)], axis=0)
    applied = sum(parts).reshape(1)
    return pl.pallas_call(
        functools.partial(_attn_body, tq=tq, tk=tk, dh=dh, lam_init=lam_init, fixed_shift=fixed_shift),
        grid=(B, H, nq, nk),
        in_specs=[pl.BlockSpec(memory_space=pltpu.SMEM),
                  pl.BlockSpec(memory_space=pltpu.SMEM),
                  pl.BlockSpec((4, dh), lambda b, h, i, j: (0, 0)),
                  pl.BlockSpec((1, dv), lambda b, h, i, j: (0, 0)),
                  pl.BlockSpec((3, tq, LANES), lambda b, h, i, j: (0, 0, 0)),
                  pl.BlockSpec((tk, LANES), lambda b, h, i, j: (0, 0)),
                  pl.BlockSpec((tq, dv), lambda b, h, i, j: (b * nq + i, h)),
                  pl.BlockSpec((tk, dv), lambda b, h, i, j: (b * nk + j, H + h)),
                  pl.BlockSpec((tk, dv), lambda b, h, i, j: (b * nk + j, h))],
        out_specs=pl.BlockSpec((tq, dv), lambda b, h, i, j: (b * nq + i, h)),
        out_shape=jax.ShapeDtypeStruct((T, H * dv), BF16),
        scratch_shapes=[pltpu.VMEM((2, tq, LANES), F32), pltpu.VMEM((2, tq, LANES), F32),
                        pltpu.VMEM((2, tq, dv), F32), pltpu.VMEM((tq, tk), F32)],
        compiler_params=_params("parallel", "parallel", "arbitrary", "arbitrary"),
        name="diff_attention_shifted" if fixed_shift else "diff_attention_online",
    )(jnp.asarray(slopes, F32), applied, lamv, subln_g.reshape(1, dv).astype(F32), qt, ka, qk2d, qk2d, v2d)


def _merge_body(fm_ref, on_ref, gf_ref, ga_ref, wf_ref, wa_ref, o_ref):
    yf = jnp.dot(fm_ref[...].astype(BF16), wf_ref[...], preferred_element_type=F32)
    ya = jnp.dot(on_ref[...], wa_ref[...], preferred_element_type=F32)
    o_ref[...] = (gf_ref[...].astype(F32) * yf + ga_ref[...].astype(F32) * ya).astype(o_ref.dtype)


def gated_merge(fm, on, gates, wf, wa, tm=256):
    T, D = on.shape
    tm = min(tm, T)
    return pl.pallas_call(
        _merge_body,
        grid=(T // tm,),
        in_specs=[pl.BlockSpec((tm, fm.shape[1]), lambda i: (i, 0)),
                  pl.BlockSpec((tm, D), lambda i: (i, 0)),
                  pl.BlockSpec((tm, D), lambda i: (i, 0)),
                  pl.BlockSpec((tm, D), lambda i: (i, 1)),
                  pl.BlockSpec(wf.shape, lambda i: (0, 0)),
                  pl.BlockSpec(wa.shape, lambda i: (0, 0))],
        out_specs=pl.BlockSpec((tm, D), lambda i: (i, 0)),
        out_shape=jax.ShapeDtypeStruct((T, D), BF16),
        compiler_params=_params("parallel"),
        name="gated_merge",
    )(fm, on, gates, gates, wf, wa)


def _out_body(mx_ref, x_ref, wo_ref, g_ref, h_ref, hn_ref, hnt_ref):
    h = x_ref[...] + jnp.dot(mx_ref[...], wo_ref[...], preferred_element_type=F32)
    h_ref[...] = h
    ms = jnp.mean(h * h, axis=-1, keepdims=True)
    hn = h * lax.rsqrt(ms + EPS) * g_ref[...]
    hn_ref[...] = hn.astype(hn_ref.dtype)
    hnt_ref[...] = hn.T.astype(hnt_ref.dtype)


def out_projection(mixed, x2d, wo, g2, tm=512):
    T, D = x2d.shape
    tm = min(tm, T)
    return pl.pallas_call(
        _out_body,
        grid=(T // tm,),
        in_specs=[pl.BlockSpec((tm, D), lambda i: (i, 0)),
                  pl.BlockSpec((tm, D), lambda i: (i, 0)),
                  pl.BlockSpec((D, D), lambda i: (0, 0)),
                  pl.BlockSpec((1, D), lambda i: (0, 0))],
        out_specs=[pl.BlockSpec((tm, D), lambda i: (i, 0)),
                   pl.BlockSpec((tm, D), lambda i: (i, 0)),
                   pl.BlockSpec((D, tm), lambda i: (0, i))],
        out_shape=[jax.ShapeDtypeStruct((T, D), F32), jax.ShapeDtypeStruct((T, D), BF16),
                   jax.ShapeDtypeStruct((D, T), BF16)],
        compiler_params=_params("parallel"),
        name="out_projection",
    )(mixed, x2d, wo, g2.reshape(1, D).astype(F32))


def _extract_top(x, dst_ref, n, want_rank=False):
    rank = jnp.full(x.shape, float(n), F32) if want_rank else None
    for r in range(n):
        mx = jnp.max(x, axis=0, keepdims=True)
        dst_ref[pl.ds(r, 1), :] = mx
        hit = x >= mx
        if want_rank:
            rank = jnp.where(hit, float(r), rank)
        x = jnp.where(hit, NEG_INF, x)
    return rank


def _peer_query_body(hn_ref, wq_ref, keys_ref, n1_ref, e1_ref, r2_ref, e2_ref, sv_ref, cs_ref):
    qp = jnp.dot(hn_ref[...], wq_ref[...], preferred_element_type=F32).astype(BF16)
    half = keys_ref.shape[-1]
    nt = (((1,), (1,)), ((), ()))
    for h in range(P_HEADS):
        lo = 2 * h * half
        st1 = lax.dot_general(keys_ref[h, 0], qp[:, lo:lo + half], nt, preferred_element_type=F32)
        st2 = lax.dot_general(keys_ref[h, 1], qp[:, lo + half:lo + 2 * half], nt,
                              preferred_element_type=F32)
        _extract_top(st1, sv_ref.at[0], P_TOPK)
        r2 = _extract_top(st2, sv_ref.at[1], P_TOPK, want_rank=True)
        sv1 = sv_ref[0]
        sv2 = sv_ref[1]
        parts = [sv1[0:1] + sv2]
        for a in range(1, 8):
            parts.append(sv1[a:a + 1] + sv2[0:8])
        parts.append(sv1[8:16] + sv2[0:1])
        _extract_top(jnp.concatenate(parts, axis=0), cs_ref, P_TOPK)
        cs = cs_ref[...]
        tau = cs[P_TOPK - 1:P_TOPK]
        z = jnp.sum(jnp.exp(cs - cs[0:1]), axis=0, keepdims=True)
        n1 = jnp.zeros(st1.shape, F32)
        for a in range(P_TOPK):
            va = sv1[a:a + 1]
            n_a = jnp.sum(jnp.where(va + sv2 >= tau, 1.0, 0.0), axis=0, keepdims=True)
            n1 = jnp.where(st1 == va, n_a, n1)
        e1 = jnp.exp(st1 - sv1[0:1])
        e2 = jnp.exp(st2 - sv2[0:1]) / z
        for tb in range(st1.shape[1] // LANES):
            cols = slice(tb * LANES, (tb + 1) * LANES)
            n1_ref[h, tb] = n1[:, cols]
            e1_ref[h, tb] = e1[:, cols]
            r2_ref[h, tb] = pltpu.bitcast(r2[:, cols].astype(BF16), jnp.uint32)
            e2_ref[h, tb] = pltpu.bitcast(e2[:, cols].astype(BF16), jnp.uint32)


def peer_query(hn, wq, keys, tt=256):
    T, D = hn.shape
    H, _, nk, half = keys.shape
    tt = min(tt, T)
    spec = pl.BlockSpec((H, tt // LANES, nk, LANES), lambda i: (0, i, 0, 0))
    rows = jax.ShapeDtypeStruct((H, T // LANES, nk, LANES), F32)
    pspec = pl.BlockSpec((H, tt // LANES, nk // 2, LANES), lambda i: (0, i, 0, 0))
    tiles = jax.ShapeDtypeStruct((H, T // LANES, nk // 2, LANES), jnp.uint32)
    return pl.pallas_call(
        _peer_query_body,
        grid=(T // tt,),
        in_specs=[pl.BlockSpec((tt, D), lambda i: (i, 0)),
                  pl.BlockSpec(wq.shape, lambda i: (0, 0)),
                  pl.BlockSpec(keys.shape, lambda i: (0, 0, 0, 0))],
        out_specs=[spec, spec, pspec, pspec],
        out_shape=[rows, rows, tiles, tiles],
        scratch_shapes=[pltpu.VMEM((2, P_TOPK, tt), F32), pltpu.VMEM((P_TOPK, tt), F32)],
        compiler_params=_params("parallel"),
        name="peer_query",
    )(hn, wq, keys)


def _peer_dense_body(hnt_ref, u_ref, v_ref, n1_ref, e1_ref, r2_ref, e2_ref, h_ref, o_ref,
                     sc0_ref, sc1_ref, a0_ref, a1_ref, *, nk, rows_per_chunk, n_chunks):
    e = pl.program_id(1)

    @pl.when(e == 0)
    def _():
        o_ref[...] = h_ref[...]

    def stages(sc_w, sc_r, a_w, a_r, do_up=True, do_gate=True, do_down=True):
        tt = o_ref.shape[0]
        chunk = e - 1

        def anchored(lhs, dep):
            if dep is None:
                return lhs
            z = jnp.tile(dep, (BF16_ROWS // dep.shape[0], lhs.shape[1] // LANES)).astype(BF16)
            return jnp.concatenate([lhs[0:BF16_ROWS, :] + z, lhs[BF16_ROWS:, :]], axis=0)

        def up_piece(half, kb, dep):
            rows = slice(half * (tt // 2), (half + 1) * (tt // 2))
            ks = slice(kb * UP_K, (kb + 1) * UP_K)
            return jnp.dot(anchored(u_ref[:, ks], dep), hnt_ref[ks, rows], preferred_element_type=F32)

        def gate(tb, ii):
            i = chunk * rows_per_chunk + ii
            def key_row(ref, h):
                row = jnp.broadcast_to(ref[h, tb, pl.ds(i, 1), :], (BF16_ROWS, LANES)).astype(BF16)
                return jnp.tile(row, (nk // BF16_ROWS, 1))

            w = jnp.zeros((nk, LANES), BF16)
            for h in range(P_HEADS):
                w = w + jnp.where(pltpu.bitcast(r2_ref[h, tb], BF16) < key_row(n1_ref, h),
                                  pltpu.bitcast(e2_ref[h, tb], BF16) * key_row(e1_ref, h), 0.0)
            x = sc_r[ii * nk:(ii + 1) * nk, tb * LANES:(tb + 1) * LANES]
            act = 0.5 * x * (1.0 + lax.erf(x * (1.0 / math.sqrt(2.0))))
            prod = (act * w.astype(F32)).T
            a_w[tb * LANES:(tb + 1) * LANES, ii * nk:(ii + 1) * nk] = prod.astype(BF16)
            bits = pltpu.bitcast(prod[0:8, :], jnp.uint32)
            bits = lax.shift_right_logical(lax.shift_right_logical(bits, jnp.uint32(16)), jnp.uint32(16))
            return pltpu.bitcast(bits, F32)

        def down(n, dep):
            cols = slice(n * MXU_COLS, (n + 1) * MXU_COLS)
            o_ref[:, cols] += jnp.dot(anchored(a_r[...], dep), v_ref[:, cols], preferred_element_type=F32)

        blocks = [(tb, ii) for tb in range(tt // LANES) for ii in range(rows_per_chunk)]
        n_down = o_ref.shape[1] // MXU_COLS
        n_kb = u_ref.shape[1] // UP_K
        per_gate = 2 * n_kb // (len(blocks) - n_down)
        dep = None
        g = 0
        for half in range(2 if do_up else 0):
            acc = None
            for kb in range(n_kb):
                piece = up_piece(half, kb, dep)
                acc = piece if acc is None else acc + piece
                if do_gate and (half * n_kb + kb + 1) % per_gate == 0:
                    dep = gate(*blocks[g])
                    g += 1
            sc_w[:, half * (tt // 2):(half + 1) * (tt // 2)] = acc
        if do_gate and not do_up:
            for blk in blocks[:len(blocks) - n_down]:
                dep = gate(*blk)
            g = len(blocks) - n_down
        for n in range(n_down):
            if do_down:
                down(n, dep)
            if do_gate:
                dep = gate(*blocks[g + n])

    steady = jnp.logical_and(e >= 2, e < n_chunks)

    @pl.when(e == 0)
    def _():
        stages(sc0_ref, None, None, None, do_gate=False, do_down=False)

    @pl.when(e == 1)
    def _():
        stages(sc1_ref, sc0_ref, a0_ref, None, do_down=False)

    @pl.when(jnp.logical_and(steady, e % 2 == 0))
    def _():
        stages(sc0_ref, sc1_ref, a1_ref, a0_ref)

    @pl.when(jnp.logical_and(steady, e % 2 == 1))
    def _():
        stages(sc1_ref, sc0_ref, a0_ref, a1_ref)

    @pl.when(e == n_chunks)
    def _():
        stages(None, sc1_ref, a1_ref, a0_ref, do_up=False)

    @pl.when(e == n_chunks + 1)
    def _():
        stages(None, None, None, a1_ref, do_up=False, do_gate=False)


def peer_dense(hnt, u, v, n1, e1, r2, e2, h1, tt=512, ec=512):
    D, T = hnt.shape
    E = u.shape[0]
    H, _, nk, _ = n1.shape
    tt = min(tt, T)
    n_chunks = E // ec
    assert n_chunks % 2 == 0 and n_chunks >= 2
    tok = pl.BlockSpec((H, tt // LANES, nk, LANES), lambda t, e: (0, t, 0, 0))
    ptok = pl.BlockSpec((H, tt // LANES, nk // 2, LANES), lambda t, e: (0, t, 0, 0))
    return pl.pallas_call(
        functools.partial(_peer_dense_body, nk=nk, rows_per_chunk=ec // nk, n_chunks=n_chunks),
        grid=(T // tt, n_chunks + 2),
        in_specs=[pl.BlockSpec((D, tt), lambda t, e: (0, t)),
                  pl.BlockSpec((ec, D), lambda t, e: (jnp.minimum(e, n_chunks - 1), 0)),
                  pl.BlockSpec((ec, D), lambda t, e: (jnp.clip(e - 2, 0, n_chunks - 1), 0)),
                  tok, tok, ptok, ptok,
                  pl.BlockSpec((tt, D), lambda t, e: (t, 0))],
        out_specs=pl.BlockSpec((tt, D), lambda t, e: (t, 0)),
        out_shape=jax.ShapeDtypeStruct((T, D), F32),
        scratch_shapes=[pltpu.VMEM((ec, tt), F32), pltpu.VMEM((ec, tt), F32),
                        pltpu.VMEM((tt, ec), BF16), pltpu.VMEM((tt, ec), BF16)],
        compiler_params=_params("parallel", "arbitrary"),
        name="peer_dense",
    )(hnt, u, v, n1, e1, r2, e2, h1)


def kernel(x, norm1_g, w_in, w_fourier, w_attn, q_norm_g, k_norm_g, lambda_q1, lambda_k1,
           lambda_q2, lambda_k2, subln_g, w_out, norm2_g, w_query, sub_keys, expert_u, expert_v):
    B, S, D = x.shape
    T = B * S
    depth = w_in.shape[0]
    dh = D // (2 * A_HEADS)
    f_width = w_fourier.shape[1]
    qk_width = A_HEADS * 2 * dh
    v_width = w_attn.shape[1]
    o_q = f_width
    o_k = o_q + qk_width
    o_v = o_k + qk_width
    o_g = o_v + v_width
    h = x.reshape(T, D)
    for i in range(depth):
        lam_init = 0.8 - 0.6 * math.exp(-0.3 * i)
        xn = rmsnorm_rows(h, norm1_g[i])
        w_in_b = w_in[i]
        qk_gain = jnp.concatenate([jnp.tile(q_norm_g[i].astype(F32) * dh ** -0.5, 2 * A_HEADS),
                                   jnp.tile(k_norm_g[i].astype(F32), 2 * A_HEADS)]).reshape(1, 2 * qk_width)
        z = project(xn, w_in_b, 0, f_width, "cast", out_dtype=F32)
        qk = project(xn, w_in_b, o_q, 2 * qk_width, "headnorm", gain=qk_gain, group=dh)
        vv = project(xn, w_in_b, o_v, v_width, "cast")
        gates = project(xn, w_in_b, o_g, 2 * D, "sigmoid")

        fm = fourier_mix(z, B, S)
        lamv = jnp.stack([lambda_q1[i], lambda_k1[i], lambda_q2[i], lambda_k2[i]]).astype(F32)
        bound = (jnp.max(jnp.abs(q_norm_g[i])) * jnp.max(jnp.abs(k_norm_g[i]))).astype(F32) * (1.01 * dh ** 0.5)
        attend = functools.partial(diff_attention, qk, vv, lamv, subln_g[i], bound, B, S, lam_init)
        on = lax.cond(bound <= MAX_SAFE_SHIFT, lambda: attend(True), lambda: attend(False))

        mixed = gated_merge(fm, on, gates, w_fourier[i].astype(BF16), w_attn[i].astype(BF16))
        h1, hn, hnt = out_projection(mixed, h, w_out[i].astype(BF16), norm2_g[i])

        n1, e1, r2, e2 = peer_query(hn, w_query[i].astype(BF16), sub_keys[i].astype(BF16))
        h = peer_dense(hnt, expert_u[i].astype(BF16), expert_v[i].astype(BF16), n1, e1, r2, e2, h1)
    return h.reshape(B, S, D)
```

```python
import functools
import math

import numpy as np
import jax
import jax.numpy as jnp
from jax import lax
from jax.experimental import pallas as pl
from jax.experimental.pallas import tpu as pltpu

EPS = 1e-6
F_GROUPS = 4
A_HEADS = 8
P_HEADS = 8
P_TOPK = 16
LANES = 128
SUBLANES = 8
MXU_COLS = 256
BF16_ROWS = 16
UP_K = 256
MAX_SAFE_SHIFT = 40.0
FFT_INNER = 128
VMEM_LIMIT_BYTES = 56 * 1024 * 1024

F32 = jnp.float32
BF16 = jnp.bfloat16
NEG_INF = float("-inf")


def _params(*semantics):
    return pltpu.CompilerParams(dimension_semantics=semantics, vmem_limit_bytes=VMEM_LIMIT_BYTES)


def _rmsnorm_body(x_ref, g_ref, o_ref):
    x = x_ref[...]
    ms = jnp.mean(x * x, axis=-1, keepdims=True)
    o_ref[...] = (x * lax.rsqrt(ms + EPS) * g_ref[...]).astype(o_ref.dtype)


def rmsnorm_rows(x2d, g, tm=512):
    T, D = x2d.shape
    return pl.pallas_call(
        _rmsnorm_body,
        grid=(T // tm,),
        in_specs=[pl.BlockSpec((tm, D), lambda i: (i, 0)),
                  pl.BlockSpec((1, D), lambda i: (0, 0))],
        out_specs=pl.BlockSpec((tm, D), lambda i: (i, 0)),
        out_shape=jax.ShapeDtypeStruct((T, D), BF16),
        compiler_params=_params("parallel"),
        name="rmsnorm",
    )(x2d, g.reshape(1, D).astype(F32))


def _proj_body(x_ref, w_ref, g_ref, o_ref, wb_ref, *, mode, group):
    @pl.when(pl.program_id(1) == 0)
    def _():
        wb_ref[...] = w_ref[...].astype(BF16)

    acc = jnp.dot(x_ref[...], wb_ref[...], preferred_element_type=F32)
    if mode == "cast":
        o_ref[...] = acc.astype(o_ref.dtype)
    elif mode == "sigmoid":
        o_ref[...] = jax.nn.sigmoid(acc).astype(o_ref.dtype)
    else:
        for s in range(acc.shape[1] // group):
            blk = acc[:, s * group:(s + 1) * group]
            ms = jnp.mean(blk * blk, axis=-1, keepdims=True)
            o_ref[:, s * group:(s + 1) * group] = (
                blk * lax.rsqrt(ms + EPS) * g_ref[:, s * group:(s + 1) * group]).astype(o_ref.dtype)


def project(x, w, col0, ncols, mode, gain=None, group=LANES, out_dtype=BF16, tm=1024, tn=1024):
    T, K = x.shape
    tm = min(tm, T)
    if gain is None:
        gain = jnp.ones((1, ncols), F32)
    off = col0 // tn
    return pl.pallas_call(
        functools.partial(_proj_body, mode=mode, group=group),
        grid=(ncols // tn, T // tm),
        in_specs=[pl.BlockSpec((tm, K), lambda j, i: (i, 0)),
                  pl.BlockSpec((K, tn), lambda j, i: (0, off + j)),
                  pl.BlockSpec((1, tn), lambda j, i: (0, j))],
        out_specs=pl.BlockSpec((tm, tn), lambda j, i: (i, j)),
        out_shape=jax.ShapeDtypeStruct((T, ncols), out_dtype),
        scratch_shapes=[pltpu.VMEM((K, tn), BF16)],
        compiler_params=_params("parallel", "arbitrary"),
        name="proj_" + mode,
    )(x, w, gain)


def _fourier_constants(S, C):
    Na = S // FFT_INNER
    eye = np.eye(SUBLANES)
    n2 = np.arange(C, dtype=np.int64)
    ang = 2.0 * np.pi * ((n2[:, None] * n2[None, :]) % C) / C
    w0 = np.concatenate([np.cos(ang), -np.sin(ang)], axis=1)
    a = np.arange(Na, dtype=np.int64)
    phi = 2.0 * np.pi * ((a[:, None] * a[None, :]) % Na) / Na
    r = np.block([[np.cos(phi), np.sin(phi)], [-np.sin(phi), np.cos(phi)]])
    r4 = r.reshape(2, Na, 2, Na)
    rk = np.einsum('qcpa,xy->qxcpay', r4, eye).reshape(2 * SUBLANES * Na, 2 * Na * SUBLANES)
    d = np.arange(FFT_INNER, dtype=np.int64)
    psi = 2.0 * np.pi * ((d[:, None] * d[None, :]) % FFT_INNER) / FFT_INNER
    f2 = np.stack([np.cos(psi), np.sin(psi)], axis=1) / math.sqrt(S * C)
    kf = np.einsum('drb,xy->dxrby', f2, eye).reshape(FFT_INNER * SUBLANES, 2 * FFT_INNER * SUBLANES)
    return (jnp.asarray(w0, BF16), jnp.asarray(rk, BF16), jnp.asarray(kf, BF16))


def _twiddles(S):
    Na = S // FFT_INNER
    b = lax.broadcasted_iota(jnp.int32, (FFT_INNER, Na), 0)
    c = lax.broadcasted_iota(jnp.int32, (FFT_INNER, Na), 1)
    th = ((b * c) % S).astype(F32) * (2.0 * math.pi / S)
    shape = (FFT_INNER, Na, LANES)
    return (jnp.broadcast_to(jnp.cos(th)[:, :, None], shape), jnp.broadcast_to(jnp.sin(th)[:, :, None], shape))


def _fourier_outer_body(z_ref, w0_ref, rk_ref, o_ref, *, groups):
    na, sub, w = z_ref.shape
    c = w // groups
    x = z_ref[...].reshape(na * sub, w).astype(BF16)
    ps, qs = [], []
    for g in range(groups):
        pq = jnp.dot(x[:, g * c:(g + 1) * c], w0_ref[...], preferred_element_type=F32)
        ps.append(pq[:, :c])
        qs.append(pq[:, c:])
    x1 = jnp.concatenate([jnp.concatenate(ps, axis=1), jnp.concatenate(qs, axis=1)], axis=0)
    a = jnp.dot(rk_ref[...], x1.astype(BF16), preferred_element_type=F32)
    o_ref[...] = a.reshape(2, sub, na, w)


def _fourier_inner_body(a_ref, tc_ref, ts_ref, kf_ref, o_ref):
    _, nb, sub, c = a_ref.shape
    ar = a_ref[0].reshape(nb * sub, c)
    ai = a_ref[1].reshape(nb * sub, c)
    tc = jnp.tile(tc_ref[...].reshape(nb * sub, LANES), (1, c // LANES))
    ts = jnp.tile(ts_ref[...].reshape(nb * sub, LANES), (1, c // LANES))
    x3 = jnp.concatenate([ar * tc + ai * ts, ai * tc - ar * ts], axis=0).astype(BF16)
    y = jnp.dot(kf_ref[...], x3, preferred_element_type=F32)
    o_ref[...] = y.reshape(nb, sub, c)


def fourier_mix(z2d, B, S):
    T, W = z2d.shape
    G = F_GROUPS
    C = W // G
    Na = S // FFT_INNER
    assert Na % SUBLANES == 0
    w0, rk, kf = _fourier_constants(S, C)
    tc, ts = _twiddles(S)
    a5 = pl.pallas_call(
        functools.partial(_fourier_outer_body, groups=G),
        grid=(B, FFT_INNER // SUBLANES),
        in_specs=[pl.BlockSpec((None, Na, SUBLANES, W), lambda b, j: (b, 0, j, 0)),
                  pl.BlockSpec(w0.shape, lambda b, j: (0, 0)),
                  pl.BlockSpec(rk.shape, lambda b, j: (0, 0))],
        out_specs=pl.BlockSpec((None, 2, SUBLANES, Na, W), lambda b, j: (b, 0, j, 0, 0)),
        out_shape=jax.ShapeDtypeStruct((B, 2, FFT_INNER, Na, W), F32),
        compiler_params=_params("parallel", "arbitrary"),
        name="fourier_outer",
    )(z2d.reshape(B, Na, FFT_INNER, W), w0, rk)
    y = pl.pallas_call(
        _fourier_inner_body,
        grid=(B, Na // SUBLANES, G),
        in_specs=[pl.BlockSpec((None, 2, FFT_INNER, SUBLANES, C), lambda b, j, g: (b, 0, 0, j, g)),
                  pl.BlockSpec((FFT_INNER, SUBLANES, LANES), lambda b, j, g: (0, j, 0)),
                  pl.BlockSpec((FFT_INNER, SUBLANES, LANES), lambda b, j, g: (0, j, 0)),
                  pl.BlockSpec(kf.shape, lambda b, j, g: (0, 0))],
        out_specs=pl.BlockSpec((None, FFT_INNER, SUBLANES, C), lambda b, j, g: (b, 0, j, g)),
        out_shape=jax.ShapeDtypeStruct((B, FFT_INNER, Na, W), F32),
        compiler_params=_params("parallel", "parallel", "arbitrary"),
        name="fourier_inner",
    )(a5, tc, ts, kf)
    return y.reshape(T, W)


def _attn_body(slope_ref, shift_ref, lamv_ref, g_ref, qt_ref, ka_ref, q_ref, k_ref, v_ref, o_ref,
               m_ref, l_ref, acc_ref, bias_ref, *, tq, tk, dh, lam_init, fixed_shift):
    h = pl.program_id(1)
    qi = pl.program_id(2)
    ki = pl.program_id(3)
    dv = 2 * dh

    @pl.when(ki == 0)
    def _():
        m_ref[...] = jnp.full(m_ref.shape, NEG_INF, F32)
        l_ref[...] = jnp.zeros(l_ref.shape, F32)
        acc_ref[...] = jnp.zeros(acc_ref.shape, F32)

    slope = slope_ref[h]

    def online_softmax_update(scores):
        v = v_ref[...]
        for c in range(2):
            s = scores[c]
            m_prev = m_ref[c]
            m_new = jnp.maximum(m_prev, jnp.max(s, axis=-1, keepdims=True))
            alpha = jnp.exp(m_prev - m_new)
            p = jnp.exp(s - jnp.tile(m_new, (1, tk // LANES)))
            l_ref[c] = alpha * l_ref[c] + jnp.sum(p, axis=-1, keepdims=True)
            acc_ref[c] = (jnp.tile(alpha, (1, dv // LANES)) * acc_ref[c]
                          + jnp.dot(p.astype(BF16), v, preferred_element_type=F32))
            m_ref[c] = m_new

    def shifted_update(scores):
        v = v_ref[...]
        for c in range(2):
            p = jnp.exp(scores[c])
            part = p[:, 0:LANES]
            for t in range(1, tk // LANES):
                part = part + p[:, t * LANES:(t + 1) * LANES]
            l_ref[c] += part
            acc_ref[c] += jnp.dot(p.astype(BF16), v, preferred_element_type=F32)

    update = shifted_update if fixed_shift else online_softmax_update
    nt = (((1,), (1,)), ((), ()))

    @pl.when(qi != ki)
    def _():
        off = (qi - ki) * tq
        sgn = jnp.where(off > 0, slope, -slope)
        qa = ((qt_ref[0] + off.astype(F32) * qt_ref[1]) * sgn + qt_ref[2]).astype(BF16)
        ka = ka_ref[...]
        q = q_ref[...]
        k = k_ref[...]
        scores = []
        for c in range(2):
            qc = jnp.concatenate([q[:, c * dh:(c + 1) * dh], qa], axis=1)
            kc = jnp.concatenate([k[:, c * dh:(c + 1) * dh], ka], axis=1)
            scores.append(lax.dot_general(qc, kc, nt, preferred_element_type=F32))
        update(scores)

    @pl.when(qi == ki)
    def _():
        @pl.when(qi == 0)
        def _():
            row = lax.broadcasted_iota(jnp.int32, (tq, tk), 0)
            col = lax.broadcasted_iota(jnp.int32, (tq, tk), 1)
            bias_ref[...] = jnp.abs(row - col).astype(F32) * (-slope) - shift_ref[0]

        bias = bias_ref[...]
        q = q_ref[...]
        k = k_ref[...]
        scores = [lax.dot_general(q[:, c * dh:(c + 1) * dh], k[:, c * dh:(c + 1) * dh], nt,
                                  preferred_element_type=F32) + bias for c in range(2)]
        update(scores)

    @pl.when(ki == pl.num_programs(3) - 1)
    def _():
        lv = lamv_ref[...]
        lam = (jnp.exp(jnp.sum(lv[0:1] * lv[1:2], axis=-1, keepdims=True))
               - jnp.exp(jnp.sum(lv[2:3] * lv[3:4], axis=-1, keepdims=True)) + lam_init)
        if fixed_shift:
            l0 = jnp.sum(l_ref[0], axis=-1, keepdims=True)
            l1 = jnp.sum(l_ref[1], axis=-1, keepdims=True)
        else:
            l0 = jnp.tile(l_ref[0], (1, dv // LANES))
            l1 = jnp.tile(l_ref[1], (1, dv // LANES))
        o = acc_ref[0] / l0 - lam * (acc_ref[1] / l1)
        ms = jnp.mean(o * o, axis=-1, keepdims=True)
        o_ref[...] = (o * lax.rsqrt(ms + EPS) * g_ref[...] * (1.0 - lam_init)).astype(o_ref.dtype)


SHIFT_LANES = (4, 5, 6)


def _split_bf16(x, parts):
    out = []
    for _ in range(parts):
        p = x.astype(BF16).astype(F32)
        out.append(p)
        x = x - p
    return out


def _alibi_templates(tq, tk):
    r = np.arange(tq)
    qt = np.zeros((2, tq, LANES), np.float32)
    qt[0, :, 0] = -(r // 256) * 256
    qt[0, :, 1] = -(r % 256)
    qt[0, :, 2] = 1.0
    qt[0, :, 3] = 1.0
    qt[1, :, 0] = -1.0
    c = np.arange(tk)
    ka = np.zeros((tk, LANES), np.float32)
    ka[:, 0] = 1.0
    ka[:, 1] = 1.0
    ka[:, 2] = c % 256
    ka[:, 3] = (c // 256) * 256
    ka[:, SHIFT_LANES[0]:SHIFT_LANES[-1] + 1] = 1.0
    return jnp.asarray(qt, F32), jnp.asarray(ka, BF16)


def diff_attention(qk2d, v2d, lamv, subln_g, shift, B, S, lam_init, fixed_shift, tile=1024):
    T, W = qk2d.shape
    H = A_HEADS
    dh = W // (4 * H)
    dv = 2 * dh
    tq = tk = min(tile, S)
    nq, nk = S // tq, S // tk
    slopes = 2.0 ** (-8.0 * np.arange(1, H + 1) / H)
    assert np.all(np.log2(slopes) == np.round(np.log2(slopes))) and S <= 8192
    qt, ka = _alibi_templates(tq, tk)
    parts = _split_bf16(shift if fixed_shift else jnp.zeros((), F32), len(SHIFT_LANES))
    lane = lax.broadcasted_iota(jnp.int32, (1, tq, LANES), 2)
    shift_plane = sum(jnp.where(lane == ln, -p, 0.0) for ln, p in zip(SHIFT_LANES, parts))
    qt = jnp.concatenate([qt, shift_plane.astype(F32)], axis=0)
    applied = sum(parts).reshape(1)
    return pl.pallas_call(
        functools.partial(_attn_body, tq=tq, tk=tk, dh=dh, lam_init=lam_init, fixed_shift=fixed_shift),
        grid=(B, H, nq, nk),
        in_specs=[pl.BlockSpec(memory_space=pltpu.SMEM),
                  pl.BlockSpec(memory_space=pltpu.SMEM),
                  pl.BlockSpec((4, dh), lambda b, h, i, j: (0, 0)),
                  pl.BlockSpec((1, dv), lambda b, h, i, j: (0, 0)),
                  pl.BlockSpec((3, tq, LANES), lambda b, h, i, j: (0, 0, 0)),
                  pl.BlockSpec((tk, LANES), lambda b, h, i, j: (0, 0)),
                  pl.BlockSpec((tq, dv), lambda b, h, i, j: (b * nq + i, h)),
                  pl.BlockSpec((tk, dv), lambda b, h, i, j: (b * nk + j, H + h)),
                  pl.BlockSpec((tk, dv), lambda b, h, i, j: (b * nk + j, h))],
        out_specs=pl.BlockSpec((tq, dv), lambda b, h, i, j: (b * nq + i, h)),
        out_shape=jax.ShapeDtypeStruct((T, H * dv), BF16),
        scratch_shapes=[pltpu.VMEM((2, tq, LANES), F32), pltpu.VMEM((2, tq, LANES), F32),
                        pltpu.VMEM((2, tq, dv), F32), pltpu.VMEM((tq, tk), F32)],
        compiler_params=_params("parallel", "parallel", "arbitrary", "arbitrary"),
        name="diff_attention_shifted" if fixed_shift else "diff_attention_online",
    )(jnp.asarray(slopes, F32), applied, lamv, subln_g.reshape(1, dv).astype(F32), qt, ka, qk2d, qk2d, v2d)


def _merge_body(fm_ref, on_ref, gf_ref, ga_ref, wf_ref, wa_ref, o_ref):
    yf = jnp.dot(fm_ref[...].astype(BF16), wf_ref[...], preferred_element_type=F32)
    ya = jnp.dot(on_ref[...], wa_ref[...], preferred_element_type=F32)
    o_ref[...] = (gf_ref[...].astype(F32) * yf + ga_ref[...].astype(F32) * ya).astype(o_ref.dtype)


def gated_merge(fm, on, gates, wf, wa, tm=256):
    T, D = on.shape
    tm = min(tm, T)
    return pl.pallas_call(
        _merge_body,
        grid=(T // tm,),
        in_specs=[pl.BlockSpec((tm, fm.shape[1]), lambda i: (i, 0)),
                  pl.BlockSpec((tm, D), lambda i: (i, 0)),
                  pl.BlockSpec((tm, D), lambda i: (i, 0)),
                  pl.BlockSpec((tm, D), lambda i: (i, 1)),
                  pl.BlockSpec(wf.shape, lambda i: (0, 0)),
                  pl.BlockSpec(wa.shape, lambda i: (0, 0))],
        out_specs=pl.BlockSpec((tm, D), lambda i: (i, 0)),
        out_shape=jax.ShapeDtypeStruct((T, D), BF16),
        compiler_params=_params("parallel"),
        name="gated_merge",
    )(fm, on, gates, gates, wf, wa)


def _out_body(mx_ref, x_ref, wo_ref, g_ref, h_ref, hn_ref, hnt_ref):
    h = x_ref[...] + jnp.dot(mx_ref[...], wo_ref[...], preferred_element_type=F32)
    h_ref[...] = h
    ms = jnp.mean(h * h, axis=-1, keepdims=True)
    hn = h * lax.rsqrt(ms + EPS) * g_ref[...]
    hn_ref[...] = hn.astype(hn_ref.dtype)
    hnt_ref[...] = hn.T.astype(hnt_ref.dtype)


def out_projection(mixed, x2d, wo, g2, tm=512):
    T, D = x2d.shape
    tm = min(tm, T)
    return pl.pallas_call(
        _out_body,
        grid=(T // tm,),
        in_specs=[pl.BlockSpec((tm, D), lambda i: (i, 0)),
                  pl.BlockSpec((tm, D), lambda i: (i, 0)),
                  pl.BlockSpec((D, D), lambda i: (0, 0)),
                  pl.BlockSpec((1, D), lambda i: (0, 0))],
        out_specs=[pl.BlockSpec((tm, D), lambda i: (i, 0)),
                   pl.BlockSpec((tm, D), lambda i: (i, 0)),
                   pl.BlockSpec((D, tm), lambda i: (0, i))],
        out_shape=[jax.ShapeDtypeStruct((T, D), F32), jax.ShapeDtypeStruct((T, D), BF16),
                   jax.ShapeDtypeStruct((D, T), BF16)],
        compiler_params=_params("parallel"),
        name="out_projection",
    )(mixed, x2d, wo, g2.reshape(1, D).astype(F32))


def _extract_top(x, dst_ref, n, want_rank=False):
    rank = jnp.full(x.shape, float(n), F32) if want_rank else None
    for r in range(n):
        mx = jnp.max(x, axis=0, keepdims=True)
        dst_ref[pl.ds(r, 1), :] = mx
        hit = x >= mx
        if want_rank:
            rank = jnp.where(hit, float(r), rank)
        x = jnp.where(hit, NEG_INF, x)
    return rank


def _peer_query_body(hn_ref, wq_ref, keys_ref, n1_ref, e1_ref, r2_ref, e2_ref, sv_ref, cs_ref):
    qp = jnp.dot(hn_ref[...], wq_ref[...], preferred_element_type=F32).astype(BF16)
    half = keys_ref.shape[-1]
    nt = (((1,), (1,)), ((), ()))
    for h in range(P_HEADS):
        lo = 2 * h * half
        st1 = lax.dot_general(keys_ref[h, 0], qp[:, lo:lo + half], nt, preferred_element_type=F32)
        st2 = lax.dot_general(keys_ref[h, 1], qp[:, lo + half:lo + 2 * half], nt,
                              preferred_element_type=F32)
        _extract_top(st1, sv_ref.at[0], P_TOPK)
        r2 = _extract_top(st2, sv_ref.at[1], P_TOPK, want_rank=True)
        sv1 = sv_ref[0]
        sv2 = sv_ref[1]
        parts = [sv1[0:1] + sv2]
        for a in range(1, 8):
            parts.append(sv1[a:a + 1] + sv2[0:8])
        parts.append(sv1[8:16] + sv2[0:1])
        _extract_top(jnp.concatenate(parts, axis=0), cs_ref, P_TOPK)
        cs = cs_ref[...]
        tau = cs[P_TOPK - 1:P_TOPK]
        z = jnp.sum(jnp.exp(cs - cs[0:1]), axis=0, keepdims=True)
        n1 = jnp.zeros(st1.shape, F32)
        for a in range(P_TOPK):
            va = sv1[a:a + 1]
            n_a = jnp.sum(jnp.where(va + sv2 >= tau, 1.0, 0.0), axis=0, keepdims=True)
            n1 = jnp.where(st1 == va, n_a, n1)
        e1 = jnp.exp(st1 - sv1[0:1])
        e2 = jnp.exp(st2 - sv2[0:1]) / z
        for tb in range(st1.shape[1] // LANES):
            cols = slice(tb * LANES, (tb + 1) * LANES)
            n1_ref[h, tb] = n1[:, cols]
            e1_ref[h, tb] = e1[:, cols]
            r2_ref[h, tb] = pltpu.bitcast(r2[:, cols].astype(BF16), jnp.uint32)
            e2_ref[h, tb] = pltpu.bitcast(e2[:, cols].astype(BF16), jnp.uint32)


def peer_query(hn, wq, keys, tt=256):
    T, D = hn.shape
    H, _, nk, half = keys.shape
    tt = min(tt, T)
    spec = pl.BlockSpec((H, tt // LANES, nk, LANES), lambda i: (0, i, 0, 0))
    rows = jax.ShapeDtypeStruct((H, T // LANES, nk, LANES), F32)
    pspec = pl.BlockSpec((H, tt // LANES, nk // 2, LANES), lambda i: (0, i, 0, 0))
    tiles = jax.ShapeDtypeStruct((H, T // LANES, nk // 2, LANES), jnp.uint32)
    return pl.pallas_call(
        _peer_query_body,
        grid=(T // tt,),
        in_specs=[pl.BlockSpec((tt, D), lambda i: (i, 0)),
                  pl.BlockSpec(wq.shape, lambda i: (0, 0)),
                  pl.BlockSpec(keys.shape, lambda i: (0, 0, 0, 0))],
        out_specs=[spec, spec, pspec, pspec],
        out_shape=[rows, rows, tiles, tiles],
        scratch_shapes=[pltpu.VMEM((2, P_TOPK, tt), F32), pltpu.VMEM((P_TOPK, tt), F32)],
        compiler_params=_params("parallel"),
        name="peer_query",
    )(hn, wq, keys)


def _peer_dense_body(hnt_ref, u_ref, v_ref, n1_ref, e1_ref, r2_ref, e2_ref, h_ref, o_ref,
                     sc0_ref, sc1_ref, a0_ref, a1_ref, *, nk, rows_per_chunk):
    e = pl.program_id(1)

    @pl.when(e == 0)
    def _():
        o_ref[...] = h_ref[...]

    def stages(sub, sc_w, sc_r, a_w, a_r, do_up=True, do_gate=True, do_down=True):
        tt = o_ref.shape[0]
        ec = sc0_ref.shape[0]
        chunk = 2 * e + sub - 1
        rows = slice(sub * ec, (sub + 1) * ec)

        def anchored(lhs, dep):
            if dep is None:
                return lhs
            z = jnp.tile(dep, (BF16_ROWS // dep.shape[0], lhs.shape[1] // LANES)).astype(BF16)
            return jnp.concatenate([lhs[0:BF16_ROWS, :] + z, lhs[BF16_ROWS:, :]], axis=0)

        def up_piece(half, kb, dep):
            toks = slice(half * (tt // 2), (half + 1) * (tt // 2))
            ks = slice(kb * UP_K, (kb + 1) * UP_K)
            return jnp.dot(anchored(u_ref[rows, ks], dep), hnt_ref[ks, toks], preferred_element_type=F32)

        def gate(tb, ii):
            i = chunk * rows_per_chunk + ii
            def key_row(ref, h):
                row = jnp.broadcast_to(ref[h, tb, pl.ds(i, 1), :], (BF16_ROWS, LANES)).astype(BF16)
                return jnp.tile(row, (nk // BF16_ROWS, 1))

            w = jnp.zeros((nk, LANES), BF16)
            for h in range(P_HEADS):
                w = w + jnp.where(pltpu.bitcast(r2_ref[h, tb], BF16) < key_row(n1_ref, h),
                                  pltpu.bitcast(e2_ref[h, tb], BF16) * key_row(e1_ref, h), 0.0)
            x = sc_r[ii * nk:(ii + 1) * nk, tb * LANES:(tb + 1) * LANES]
            act = 0.5 * x * (1.0 + lax.erf(x * (1.0 / math.sqrt(2.0))))
            prod = (act * w.astype(F32)).T
            a_w[tb * LANES:(tb + 1) * LANES, ii * nk:(ii + 1) * nk] = prod.astype(BF16)
            bits = pltpu.bitcast(prod[0:8, :], jnp.uint32)
            bits = lax.shift_right_logical(lax.shift_right_logical(bits, jnp.uint32(16)), jnp.uint32(16))
            return pltpu.bitcast(bits, F32)

        def down(n, dep):
            cols = slice(n * MXU_COLS, (n + 1) * MXU_COLS)
            o_ref[:, cols] += jnp.dot(anchored(a_r[...], dep), v_ref[rows, cols], preferred_element_type=F32)

        blocks = [(tb, ii) for tb in range(tt // LANES) for ii in range(rows_per_chunk)]
        n_down = o_ref.shape[1] // MXU_COLS
        n_kb = u_ref.shape[1] // UP_K
        per_gate = 2 * n_kb // (len(blocks) - n_down)
        dep = None
        g = 0
        for half in range(2 if do_up else 0):
            acc = None
            for kb in range(n_kb):
                piece = up_piece(half, kb, dep)
                acc = piece if acc is None else acc + piece
                if do_gate and (half * n_kb + kb + 1) % per_gate == 0:
                    dep = gate(*blocks[g])
                    g += 1
            sc_w[:, half * (tt // 2):(half + 1) * (tt // 2)] = acc
        if do_gate and not do_up:
            for blk in blocks[:len(blocks) - n_down]:
                dep = gate(*blk)
            g = len(blocks) - n_down
        for n in range(n_down):
            if do_down:
                down(n, dep)
            if do_gate:
                dep = gate(*blocks[g + n])

    last = pl.num_programs(1) - 1

    @pl.when(e == 0)
    def _():
        stages(0, sc0_ref, None, None, None, do_gate=False, do_down=False)
        stages(1, sc1_ref, sc0_ref, a0_ref, None, do_down=False)

    @pl.when(jnp.logical_and(e > 0, e < last))
    def _():
        stages(0, sc0_ref, sc1_ref, a1_ref, a0_ref)
        stages(1, sc1_ref, sc0_ref, a0_ref, a1_ref)

    @pl.when(e == last)
    def _():
        stages(0, None, sc1_ref, a1_ref, a0_ref, do_up=False)
        stages(1, None, None, None, a1_ref, do_up=False, do_gate=False)


def peer_dense(hnt, u, v, n1, e1, r2, e2, h1, tt=512, ec=512):
    D, T = hnt.shape
    E = u.shape[0]
    H, _, nk, _ = n1.shape
    tt = min(tt, T)
    n_steps = E // (2 * ec)
    tok = pl.BlockSpec((H, tt // LANES, nk, LANES), lambda t, e: (0, t, 0, 0))
    ptok = pl.BlockSpec((H, tt // LANES, nk // 2, LANES), lambda t, e: (0, t, 0, 0))
    return pl.pallas_call(
        functools.partial(_peer_dense_body, nk=nk, rows_per_chunk=ec // nk),
        grid=(T // tt, n_steps + 1),
        in_specs=[pl.BlockSpec((D, tt), lambda t, e: (0, t)),
                  pl.BlockSpec((2 * ec, D), lambda t, e: (jnp.minimum(e, n_steps - 1), 0)),
                  pl.BlockSpec((2 * ec, D), lambda t, e: (jnp.maximum(e - 1, 0), 0)),
                  tok, tok, ptok, ptok,
                  pl.BlockSpec((tt, D), lambda t, e: (t, 0))],
        out_specs=pl.BlockSpec((tt, D), lambda t, e: (t, 0)),
        out_shape=jax.ShapeDtypeStruct((T, D), F32),
        scratch_shapes=[pltpu.VMEM((ec, tt), F32), pltpu.VMEM((ec, tt), F32),
                        pltpu.VMEM((tt, ec), BF16), pltpu.VMEM((tt, ec), BF16)],
        compiler_params=_params("parallel", "arbitrary"),
        name="peer_dense",
    )(hnt, u, v, n1, e1, r2, e2, h1)


def kernel(x, norm1_g, w_in, w_fourier, w_attn, q_norm_g, k_norm_g, lambda_q1, lambda_k1,
           lambda_q2, lambda_k2, subln_g, w_out, norm2_g, w_query, sub_keys, expert_u, expert_v):
    B, S, D = x.shape
    T = B * S
    depth = w_in.shape[0]
    dh = D // (2 * A_HEADS)
    f_width = w_fourier.shape[1]
    qk_width = A_HEADS * 2 * dh
    v_width = w_attn.shape[1]
    o_q = f_width
    o_k = o_q + qk_width
    o_v = o_k + qk_width
    o_g = o_v + v_width
    h = x.reshape(T, D)
    for i in range(depth):
        lam_init = 0.8 - 0.6 * math.exp(-0.3 * i)
        xn = rmsnorm_rows(h, norm1_g[i])
        w_in_b = w_in[i]
        qk_gain = jnp.concatenate([jnp.tile(q_norm_g[i].astype(F32) * dh ** -0.5, 2 * A_HEADS),
                                   jnp.tile(k_norm_g[i].astype(F32), 2 * A_HEADS)]).reshape(1, 2 * qk_width)
        z = project(xn, w_in_b, 0, f_width, "cast", out_dtype=F32)
        qk = project(xn, w_in_b, o_q, 2 * qk_width, "headnorm", gain=qk_gain, group=dh)
        vv = project(xn, w_in_b, o_v, v_width, "cast")
        gates = project(xn, w_in_b, o_g, 2 * D, "sigmoid")

        fm = fourier_mix(z, B, S)
        lamv = jnp.stack([lambda_q1[i], lambda_k1[i], lambda_q2[i], lambda_k2[i]]).astype(F32)
        bound = (jnp.max(jnp.abs(q_norm_g[i])) * jnp.max(jnp.abs(k_norm_g[i]))).astype(F32) * (1.01 * dh ** 0.5)
        attend = functools.partial(diff_attention, qk, vv, lamv, subln_g[i], bound, B, S, lam_init)
        on = lax.cond(bound <= MAX_SAFE_SHIFT, lambda: attend(True), lambda: attend(False))

        mixed = gated_merge(fm, on, gates, w_fourier[i].astype(BF16), w_attn[i].astype(BF16))
        h1, hn, hnt = out_projection(mixed, h, w_out[i].astype(BF16), norm2_g[i])

        n1, e1, r2, e2 = peer_query(hn, w_query[i].astype(BF16), sub_keys[i].astype(BF16))
        h = peer_dense(hnt, expert_u[i].astype(BF16), expert_v[i].astype(BF16), n1, e1, r2, e2, h1)
    return h.reshape(B, S, D)
```

```python
import functools
import math

import numpy as np
import jax
import jax.numpy as jnp
from jax import lax
from jax.experimental import pallas as pl
from jax.experimental.pallas import tpu as pltpu

EPS = 1e-6
F_GROUPS = 4
A_HEADS = 8
P_HEADS = 8
P_TOPK = 16
LANES = 128
SUBLANES = 8
MXU_COLS = 256
BF16_ROWS = 16
UP_K = 256
MAX_SAFE_SHIFT = 40.0
FFT_INNER = 128
VMEM_LIMIT_BYTES = 56 * 1024 * 1024

F32 = jnp.float32
BF16 = jnp.bfloat16
NEG_INF = float("-inf")


def _params(*semantics):
    return pltpu.CompilerParams(dimension_semantics=semantics, vmem_limit_bytes=VMEM_LIMIT_BYTES)


def _rmsnorm_body(x_ref, g_ref, o_ref):
    x = x_ref[...]
    ms = jnp.mean(x * x, axis=-1, keepdims=True)
    o_ref[...] = (x * lax.rsqrt(ms + EPS) * g_ref[...]).astype(o_ref.dtype)


def rmsnorm_rows(x2d, g, tm=512):
    T, D = x2d.shape
    return pl.pallas_call(
        _rmsnorm_body,
        grid=(T // tm,),
        in_specs=[pl.BlockSpec((tm, D), lambda i: (i, 0)),
                  pl.BlockSpec((1, D), lambda i: (0, 0))],
        out_specs=pl.BlockSpec((tm, D), lambda i: (i, 0)),
        out_shape=jax.ShapeDtypeStruct((T, D), BF16),
        compiler_params=_params("parallel"),
        name="rmsnorm",
    )(x2d, g.reshape(1, D).astype(F32))


def _proj_body(x_ref, w_ref, g_ref, o_ref, wb_ref, *, mode, group):
    @pl.when(pl.program_id(1) == 0)
    def _():
        wb_ref[...] = w_ref[...].astype(BF16)

    acc = jnp.dot(x_ref[...], wb_ref[...], preferred_element_type=F32)
    if mode == "cast":
        o_ref[...] = acc.astype(o_ref.dtype)
    elif mode == "sigmoid":
        o_ref[...] = jax.nn.sigmoid(acc).astype(o_ref.dtype)
    else:
        for s in range(acc.shape[1] // group):
            blk = acc[:, s * group:(s + 1) * group]
            ms = jnp.mean(blk * blk, axis=-1, keepdims=True)
            o_ref[:, s * group:(s + 1) * group] = (
                blk * lax.rsqrt(ms + EPS) * g_ref[:, s * group:(s + 1) * group]).astype(o_ref.dtype)


def project(x, w, col0, ncols, mode, gain=None, group=LANES, out_dtype=BF16, tm=1024, tn=1024):
    T, K = x.shape
    tm = min(tm, T)
    if gain is None:
        gain = jnp.ones((1, ncols), F32)
    off = col0 // tn
    return pl.pallas_call(
        functools.partial(_proj_body, mode=mode, group=group),
        grid=(ncols // tn, T // tm),
        in_specs=[pl.BlockSpec((tm, K), lambda j, i: (i, 0)),
                  pl.BlockSpec((K, tn), lambda j, i: (0, off + j)),
                  pl.BlockSpec((1, tn), lambda j, i: (0, j))],
        out_specs=pl.BlockSpec((tm, tn), lambda j, i: (i, j)),
        out_shape=jax.ShapeDtypeStruct((T, ncols), out_dtype),
        scratch_shapes=[pltpu.VMEM((K, tn), BF16)],
        compiler_params=_params("parallel", "arbitrary"),
        name="proj_" + mode,
    )(x, w, gain)


def _fourier_constants(S, C):
    Na = S // FFT_INNER
    eye = np.eye(SUBLANES)
    n2 = np.arange(C, dtype=np.int64)
    ang = 2.0 * np.pi * ((n2[:, None] * n2[None, :]) % C) / C
    w0 = np.concatenate([np.cos(ang), -np.sin(ang)], axis=1)
    a = np.arange(Na, dtype=np.int64)
    phi = 2.0 * np.pi * ((a[:, None] * a[None, :]) % Na) / Na
    r = np.block([[np.cos(phi), np.sin(phi)], [-np.sin(phi), np.cos(phi)]])
    r4 = r.reshape(2, Na, 2, Na)
    rk = np.einsum('qcpa,xy->qxcpay', r4, eye).reshape(2 * SUBLANES * Na, 2 * Na * SUBLANES)
    d = np.arange(FFT_INNER, dtype=np.int64)
    psi = 2.0 * np.pi * ((d[:, None] * d[None, :]) % FFT_INNER) / FFT_INNER
    f2 = np.stack([np.cos(psi), np.sin(psi)], axis=1) / math.sqrt(S * C)
    kf = np.einsum('drb,xy->dxrby', f2, eye).reshape(FFT_INNER * SUBLANES, 2 * FFT_INNER * SUBLANES)
    return (jnp.asarray(w0, BF16), jnp.asarray(rk, BF16), jnp.asarray(kf, BF16))


def _twiddles(S):
    Na = S // FFT_INNER
    b = lax.broadcasted_iota(jnp.int32, (FFT_INNER, Na), 0)
    c = lax.broadcasted_iota(jnp.int32, (FFT_INNER, Na), 1)
    th = ((b * c) % S).astype(F32) * (2.0 * math.pi / S)
    shape = (FFT_INNER, Na, LANES)
    return (jnp.broadcast_to(jnp.cos(th)[:, :, None], shape), jnp.broadcast_to(jnp.sin(th)[:, :, None], shape))


def _fourier_outer_body(z_ref, w0_ref, rk_ref, o_ref, *, groups):
    na, sub, w = z_ref.shape
    c = w // groups
    x = z_ref[...].reshape(na * sub, w).astype(BF16)
    ps, qs = [], []
    for g in range(groups):
        pq = jnp.dot(x[:, g * c:(g + 1) * c], w0_ref[...], preferred_element_type=F32)
        ps.append(pq[:, :c])
        qs.append(pq[:, c:])
    x1 = jnp.concatenate([jnp.concatenate(ps, axis=1), jnp.concatenate(qs, axis=1)], axis=0)
    a = jnp.dot(rk_ref[...], x1.astype(BF16), preferred_element_type=F32)
    o_ref[...] = a.reshape(2, sub, na, w)


def _fourier_inner_body(a_ref, tc_ref, ts_ref, kf_ref, o_ref):
    _, nb, sub, c = a_ref.shape
    ar = a_ref[0].reshape(nb * sub, c)
    ai = a_ref[1].reshape(nb * sub, c)
    tc = jnp.tile(tc_ref[...].reshape(nb * sub, LANES), (1, c // LANES))
    ts = jnp.tile(ts_ref[...].reshape(nb * sub, LANES), (1, c // LANES))
    x3 = jnp.concatenate([ar * tc + ai * ts, ai * tc - ar * ts], axis=0).astype(BF16)
    y = jnp.dot(kf_ref[...], x3, preferred_element_type=F32)
    o_ref[...] = y.reshape(nb, sub, c)


def fourier_mix(z2d, B, S):
    T, W = z2d.shape
    G = F_GROUPS
    C = W // G
    Na = S // FFT_INNER
    assert Na % SUBLANES == 0
    w0, rk, kf = _fourier_constants(S, C)
    tc, ts = _twiddles(S)
    a5 = pl.pallas_call(
        functools.partial(_fourier_outer_body, groups=G),
        grid=(B, FFT_INNER // SUBLANES),
        in_specs=[pl.BlockSpec((None, Na, SUBLANES, W), lambda b, j: (b, 0, j, 0)),
                  pl.BlockSpec(w0.shape, lambda b, j: (0, 0)),
                  pl.BlockSpec(rk.shape, lambda b, j: (0, 0))],
        out_specs=pl.BlockSpec((None, 2, SUBLANES, Na, W), lambda b, j: (b, 0, j, 0, 0)),
        out_shape=jax.ShapeDtypeStruct((B, 2, FFT_INNER, Na, W), F32),
        compiler_params=_params("parallel", "arbitrary"),
        name="fourier_outer",
    )(z2d.reshape(B, Na, FFT_INNER, W), w0, rk)
    y = pl.pallas_call(
        _fourier_inner_body,
        grid=(B, Na // SUBLANES, G),
        in_specs=[pl.BlockSpec((None, 2, FFT_INNER, SUBLANES, C), lambda b, j, g: (b, 0, 0, j, g)),
                  pl.BlockSpec((FFT_INNER, SUBLANES, LANES), lambda b, j, g: (0, j, 0)),
                  pl.BlockSpec((FFT_INNER, SUBLANES, LANES), lambda b, j, g: (0, j, 0)),
                  pl.BlockSpec(kf.shape, lambda b, j, g: (0, 0))],
        out_specs=pl.BlockSpec((None, FFT_INNER, SUBLANES, C), lambda b, j, g: (b, 0, j, g)),
        out_shape=jax.ShapeDtypeStruct((B, FFT_INNER, Na, W), F32),
        compiler_params=_params("parallel", "parallel", "arbitrary"),
        name="fourier_inner",
    )(a5, tc, ts, kf)
    return y.reshape(T, W)


def _attn_body(slope_ref, shift_ref, lamv_ref, g_ref, qt_ref, ka_ref, q_ref, k_ref, v_ref, o_ref,
               m_ref, l_ref, acc_ref, bias_ref, *, tq, tk, dh, lam_init, fixed_shift):
    h = pl.program_id(1)
    qi = pl.program_id(2)
    ki = pl.program_id(3)
    dv = 2 * dh

    @pl.when(ki == 0)
    def _():
        m_ref[...] = jnp.full(m_ref.shape, NEG_INF, F32)
        l_ref[...] = jnp.zeros(l_ref.shape, F32)
        acc_ref[...] = jnp.zeros(acc_ref.shape, F32)

    slope = slope_ref[h]

    def online_softmax_update(scores):
        v = v_ref[...]
        for c in range(2):
            s = scores[c]
            m_prev = m_ref[c]
            m_new = jnp.maximum(m_prev, jnp.max(s, axis=-1, keepdims=True))
            alpha = jnp.exp(m_prev - m_new)
            p = jnp.exp(s - jnp.tile(m_new, (1, tk // LANES)))
            l_ref[c] = alpha * l_ref[c] + jnp.sum(p, axis=-1, keepdims=True)
            acc_ref[c] = (jnp.tile(alpha, (1, dv // LANES)) * acc_ref[c]
                          + jnp.dot(p.astype(BF16), v, preferred_element_type=F32))
            m_ref[c] = m_new

    def shifted_update(scores):
        v = v_ref[...]
        for c in range(2):
            p = jnp.exp(scores[c])
            part = p[:, 0:LANES]
            for t in range(1, tk // LANES):
                part = part + p[:, t * LANES:(t + 1) * LANES]
            l_ref[c] += part
            acc_ref[c] += jnp.dot(p.astype(BF16), v, preferred_element_type=F32)

    update = shifted_update if fixed_shift else online_softmax_update
    nt = (((1,), (1,)), ((), ()))

    @pl.when(qi != ki)
    def _():
        off = (qi - ki) * tq
        sgn = jnp.where(off > 0, slope, -slope)
        qa = ((qt_ref[0] + off.astype(F32) * qt_ref[1]) * sgn + qt_ref[2]).astype(BF16)
        ka = ka_ref[...]
        q = q_ref[...]
        k = k_ref[...]
        scores = []
        for c in range(2):
            qc = jnp.concatenate([q[:, c * dh:(c + 1) * dh], qa], axis=1)
            kc = jnp.concatenate([k[:, c * dh:(c + 1) * dh], ka], axis=1)
            scores.append(lax.dot_general(qc, kc, nt, preferred_element_type=F32))
        update(scores)

    @pl.when(qi == ki)
    def _():
        @pl.when(qi == 0)
        def _():
            row = lax.broadcasted_iota(jnp.int32, (tq, tk), 0)
            col = lax.broadcasted_iota(jnp.int32, (tq, tk), 1)
            bias_ref[...] = jnp.abs(row - col).astype(F32) * (-slope) - shift_ref[0]

        bias = bias_ref[...]
        q = q_ref[...]
        k = k_ref[...]
        scores = [lax.dot_general(q[:, c * dh:(c + 1) * dh], k[:, c * dh:(c + 1) * dh], nt,
                                  preferred_element_type=F32) + bias for c in range(2)]
        update(scores)

    @pl.when(ki == pl.num_programs(3) - 1)
    def _():
        lv = lamv_ref[...]
        lam = (jnp.exp(jnp.sum(lv[0:1] * lv[1:2], axis=-1, keepdims=True))
               - jnp.exp(jnp.sum(lv[2:3] * lv[3:4], axis=-1, keepdims=True)) + lam_init)
        if fixed_shift:
            l0 = jnp.sum(l_ref[0], axis=-1, keepdims=True)
            l1 = jnp.sum(l_ref[1], axis=-1, keepdims=True)
        else:
            l0 = jnp.tile(l_ref[0], (1, dv // LANES))
            l1 = jnp.tile(l_ref[1], (1, dv // LANES))
        o = acc_ref[0] / l0 - lam * (acc_ref[1] / l1)
        ms = jnp.mean(o * o, axis=-1, keepdims=True)
        o_ref[...] = (o * lax.rsqrt(ms + EPS) * g_ref[...] * (1.0 - lam_init)).astype(o_ref.dtype)


SHIFT_LANES = (4, 5, 6)


def _split_bf16(x, parts):
    out = []
    for _ in range(parts):
        p = x.astype(BF16).astype(F32)
        out.append(p)
        x = x - p
    return out


def _alibi_templates(tq, tk):
    r = np.arange(tq)
    qt = np.zeros((2, tq, LANES), np.float32)
    qt[0, :, 0] = -(r // 256) * 256
    qt[0, :, 1] = -(r % 256)
    qt[0, :, 2] = 1.0
    qt[0, :, 3] = 1.0
    qt[1, :, 0] = -1.0
    c = np.arange(tk)
    ka = np.zeros((tk, LANES), np.float32)
    ka[:, 0] = 1.0
    ka[:, 1] = 1.0
    ka[:, 2] = c % 256
    ka[:, 3] = (c // 256) * 256
    ka[:, SHIFT_LANES[0]:SHIFT_LANES[-1] + 1] = 1.0
    return jnp.asarray(qt, F32), jnp.asarray(ka, BF16)


def diff_attention(qk2d, v2d, lamv, subln_g, shift, B, S, lam_init, fixed_shift, tile=1024):
    T, W = qk2d.shape
    H = A_HEADS
    dh = W // (4 * H)
    dv = 2 * dh
    tq = tk = min(tile, S)
    nq, nk = S // tq, S // tk
    slopes = 2.0 ** (-8.0 * np.arange(1, H + 1) / H)
    assert np.all(np.log2(slopes) == np.round(np.log2(slopes))) and S <= 8192
    qt, ka = _alibi_templates(tq, tk)
    parts = _split_bf16(shift if fixed_shift else jnp.zeros((), F32), len(SHIFT_LANES))
    lane = lax.broadcasted_iota(jnp.int32, (1, tq, LANES), 2)
    shift_plane = sum(jnp.where(lane == ln, -p, 0.0) for ln, p in zip(SHIFT_LANES, parts))
    qt = jnp.concatenate([qt, shift_plane.astype(F32)], axis=0)
    applied = sum(parts).reshape(1)
    return pl.pallas_call(
        functools.partial(_attn_body, tq=tq, tk=tk, dh=dh, lam_init=lam_init, fixed_shift=fixed_shift),
        grid=(B, H, nq, nk),
        in_specs=[pl.BlockSpec(memory_space=pltpu.SMEM),
                  pl.BlockSpec(memory_space=pltpu.SMEM),
                  pl.BlockSpec((4, dh), lambda b, h, i, j: (0, 0)),
                  pl.BlockSpec((1, dv), lambda b, h, i, j: (0, 0)),
                  pl.BlockSpec((3, tq, LANES), lambda b, h, i, j: (0, 0, 0)),
                  pl.BlockSpec((tk, LANES), lambda b, h, i, j: (0, 0)),
                  pl.BlockSpec((tq, dv), lambda b, h, i, j: (b * nq + i, h)),
                  pl.BlockSpec((tk, dv), lambda b, h, i, j: (b * nk + j, H + h)),
                  pl.BlockSpec((tk, dv), lambda b, h, i, j: (b * nk + j, h))],
        out_specs=pl.BlockSpec((tq, dv), lambda b, h, i, j: (b * nq + i, h)),
        out_shape=jax.ShapeDtypeStruct((T, H * dv), BF16),
        scratch_shapes=[pltpu.VMEM((2, tq, LANES), F32), pltpu.VMEM((2, tq, LANES), F32),
                        pltpu.VMEM((2, tq, dv), F32), pltpu.VMEM((tq, tk), F32)],
        compiler_params=_params("parallel", "parallel", "arbitrary", "arbitrary"),
        name="diff_attention_shifted" if fixed_shift else "diff_attention_online",
    )(jnp.asarray(slopes, F32), applied, lamv, subln_g.reshape(1, dv).astype(F32), qt, ka, qk2d, qk2d, v2d)


def _merge_body(fm_ref, on_ref, gf_ref, ga_ref, wf_ref, wa_ref, o_ref):
    yf = jnp.dot(fm_ref[...].astype(BF16), wf_ref[...], preferred_element_type=F32)
    ya = jnp.dot(on_ref[...], wa_ref[...], preferred_element_type=F32)
    o_ref[...] = (gf_ref[...].astype(F32) * yf + ga_ref[...].astype(F32) * ya).astype(o_ref.dtype)


def gated_merge(fm, on, gates, wf, wa, tm=256):
    T, D = on.shape
    tm = min(tm, T)
    return pl.pallas_call(
        _merge_body,
        grid=(T // tm,),
        in_specs=[pl.BlockSpec((tm, fm.shape[1]), lambda i: (i, 0)),
                  pl.BlockSpec((tm, D), lambda i: (i, 0)),
                  pl.BlockSpec((tm, D), lambda i: (i, 0)),
                  pl.BlockSpec((tm, D), lambda i: (i, 1)),
                  pl.BlockSpec(wf.shape, lambda i: (0, 0)),
                  pl.BlockSpec(wa.shape, lambda i: (0, 0))],
        out_specs=pl.BlockSpec((tm, D), lambda i: (i, 0)),
        out_shape=jax.ShapeDtypeStruct((T, D), BF16),
        compiler_params=_params("parallel"),
        name="gated_merge",
    )(fm, on, gates, gates, wf, wa)


def _out_body(mx_ref, x_ref, wo_ref, g_ref, h_ref, hn_ref, hnt_ref):
    h = x_ref[...] + jnp.dot(mx_ref[...], wo_ref[...], preferred_element_type=F32)
    h_ref[...] = h
    ms = jnp.mean(h * h, axis=-1, keepdims=True)
    hn = h * lax.rsqrt(ms + EPS) * g_ref[...]
    hn_ref[...] = hn.astype(hn_ref.dtype)
    hnt_ref[...] = hn.T.astype(hnt_ref.dtype)


def out_projection(mixed, x2d, wo, g2, tm=512):
    T, D = x2d.shape
    tm = min(tm, T)
    return pl.pallas_call(
        _out_body,
        grid=(T // tm,),
        in_specs=[pl.BlockSpec((tm, D), lambda i: (i, 0)),
                  pl.BlockSpec((tm, D), lambda i: (i, 0)),
                  pl.BlockSpec((D, D), lambda i: (0, 0)),
                  pl.BlockSpec((1, D), lambda i: (0, 0))],
        out_specs=[pl.BlockSpec((tm, D), lambda i: (i, 0)),
                   pl.BlockSpec((tm, D), lambda i: (i, 0)),
                   pl.BlockSpec((D, tm), lambda i: (0, i))],
        out_shape=[jax.ShapeDtypeStruct((T, D), F32), jax.ShapeDtypeStruct((T, D), BF16),
                   jax.ShapeDtypeStruct((D, T), BF16)],
        compiler_params=_params("parallel"),
        name="out_projection",
    )(mixed, x2d, wo, g2.reshape(1, D).astype(F32))


def _extract_top(x, dst_ref, n, want_rank=False):
    rank = jnp.full(x.shape, float(n), F32) if want_rank else None
    for r in range(n):
        mx = jnp.max(x, axis=0, keepdims=True)
        dst_ref[pl.ds(r, 1), :] = mx
        hit = x >= mx
        if want_rank:
            rank = jnp.where(hit, float(r), rank)
        x = jnp.where(hit, NEG_INF, x)
    return rank


def _peer_query_body(hn_ref, wq_ref, keys_ref, n1_ref, e1_ref, r2_ref, e2_ref, sv_ref, cs_ref):
    qp = jnp.dot(hn_ref[...], wq_ref[...], preferred_element_type=F32).astype(BF16)
    half = keys_ref.shape[-1]
    nt = (((1,), (1,)), ((), ()))
    for h in range(P_HEADS):
        lo = 2 * h * half
        st1 = lax.dot_general(keys_ref[h, 0], qp[:, lo:lo + half], nt, preferred_element_type=F32)
        st2 = lax.dot_general(keys_ref[h, 1], qp[:, lo + half:lo + 2 * half], nt,
                              preferred_element_type=F32)
        _extract_top(st1, sv_ref.at[0], P_TOPK)
        r2 = _extract_top(st2, sv_ref.at[1], P_TOPK, want_rank=True)
        sv1 = sv_ref[0]
        sv2 = sv_ref[1]
        parts = [sv1[0:1] + sv2]
        for a in range(1, 8):
            parts.append(sv1[a:a + 1] + sv2[0:8])
        parts.append(sv1[8:16] + sv2[0:1])
        _extract_top(jnp.concatenate(parts, axis=0), cs_ref, P_TOPK)
        cs = cs_ref[...]
        tau = cs[P_TOPK - 1:P_TOPK]
        z = jnp.sum(jnp.exp(cs - cs[0:1]), axis=0, keepdims=True)
        n1 = jnp.zeros(st1.shape, F32)
        for a in range(P_TOPK):
            va = sv1[a:a + 1]
            n_a = jnp.sum(jnp.where(va + sv2 >= tau, 1.0, 0.0), axis=0, keepdims=True)
            n1 = jnp.where(st1 == va, n_a, n1)
        e1 = jnp.exp(st1 - sv1[0:1])
        e2 = jnp.exp(st2 - sv2[0:1]) / z
        for tb in range(st1.shape[1] // LANES):
            cols = slice(tb * LANES, (tb + 1) * LANES)
            n1_ref[h, tb] = n1[:, cols]
            e1_ref[h, tb] = e1[:, cols]
            r2_ref[h, tb] = pltpu.bitcast(r2[:, cols].astype(BF16), jnp.uint32)
            e2_ref[h, tb] = pltpu.bitcast(e2[:, cols].astype(BF16), jnp.uint32)


def peer_query(hn, wq, keys, tt=256):
    T, D = hn.shape
    H, _, nk, half = keys.shape
    tt = min(tt, T)
    spec = pl.BlockSpec((H, tt // LANES, nk, LANES), lambda i: (0, i, 0, 0))
    rows = jax.ShapeDtypeStruct((H, T // LANES, nk, LANES), F32)
    pspec = pl.BlockSpec((H, tt // LANES, nk // 2, LANES), lambda i: (0, i, 0, 0))
    tiles = jax.ShapeDtypeStruct((H, T // LANES, nk // 2, LANES), jnp.uint32)
    return pl.pallas_call(
        _peer_query_body,
        grid=(T // tt,),
        in_specs=[pl.BlockSpec((tt, D), lambda i: (i, 0)),
                  pl.BlockSpec(wq.shape, lambda i: (0, 0)),
                  pl.BlockSpec(keys.shape, lambda i: (0, 0, 0, 0))],
        out_specs=[spec, spec, pspec, pspec],
        out_shape=[rows, rows, tiles, tiles],
        scratch_shapes=[pltpu.VMEM((2, P_TOPK, tt), F32), pltpu.VMEM((P_TOPK, tt), F32)],
        compiler_params=_params("parallel"),
        name="peer_query",
    )(hn, wq, keys)


def _peer_dense_body(hnt_ref, u_ref, v_ref, n1_ref, e1_ref, r2_ref, e2_ref, h_ref, o_ref,
                     sc0_ref, sc1_ref, a0_ref, a1_ref, *, nk, rows_per_chunk, n_chunks):
    e = pl.program_id(1)

    @pl.when(e == 0)
    def _():
        o_ref[...] = h_ref[...]

    def stages(sc_w, sc_r, a_w, a_r, do_up=True, do_gate=True, do_down=True):
        tt = o_ref.shape[0]
        chunk = e - 1

        def anchored(lhs, dep):
            if dep is None:
                return lhs
            z = jnp.tile(dep, (BF16_ROWS // dep.shape[0], lhs.shape[1] // LANES)).astype(BF16)
            return jnp.concatenate([lhs[0:BF16_ROWS, :] + z, lhs[BF16_ROWS:, :]], axis=0)

        def up_piece(half, kb, dep):
            rows = slice(half * (tt // 2), (half + 1) * (tt // 2))
            ks = slice(kb * UP_K, (kb + 1) * UP_K)
            return jnp.dot(anchored(u_ref[:, ks], dep), hnt_ref[ks, rows], preferred_element_type=F32)

        def gate(tb, ii):
            i = chunk * rows_per_chunk + ii
            def key_row(ref, h):
                row = jnp.broadcast_to(ref[h, tb, pl.ds(i, 1), :], (BF16_ROWS, LANES)).astype(BF16)
                return jnp.tile(row, (nk // BF16_ROWS, 1))

            w = jnp.zeros((nk, LANES), BF16)
            for h in range(P_HEADS):
                w = w + jnp.where(pltpu.bitcast(r2_ref[h, tb], BF16) < key_row(n1_ref, h),
                                  pltpu.bitcast(e2_ref[h, tb], BF16) * key_row(e1_ref, h), 0.0)
            x = sc_r[ii * nk:(ii + 1) * nk, tb * LANES:(tb + 1) * LANES]
            act = 0.5 * x * (1.0 + lax.erf(x * (1.0 / math.sqrt(2.0))))
            prod = (act * w.astype(F32)).T
            a_w[tb * LANES:(tb + 1) * LANES, ii * nk:(ii + 1) * nk] = prod.astype(BF16)
            bits = pltpu.bitcast(prod[0:8, :], jnp.uint32)
            bits = lax.shift_right_logical(lax.shift_right_logical(bits, jnp.uint32(16)), jnp.uint32(16))
            return pltpu.bitcast(bits, F32)

        def down(n, dep):
            cols = slice(n * MXU_COLS, (n + 1) * MXU_COLS)
            o_ref[:, cols] += jnp.dot(anchored(a_r[...], dep), v_ref[:, cols], preferred_element_type=F32)

        blocks = [(tb, ii) for tb in range(tt // LANES) for ii in range(rows_per_chunk)]
        n_down = o_ref.shape[1] // MXU_COLS
        n_kb = u_ref.shape[1] // UP_K
        n_up = 2 * n_kb
        gates_up = len(blocks) // 2
        dep = None
        issued = 0
        acc = None
        for p in range(n_up if do_up else 0):
            half, kb = divmod(p, n_kb)
            piece = up_piece(half, kb, dep)
            acc = piece if kb == 0 else acc + piece
            if kb == n_kb - 1:
                sc_w[:, half * (tt // 2):(half + 1) * (tt // 2)] = acc
            while do_gate and issued < (p + 1) * gates_up // n_up:
                dep = gate(*blocks[issued])
                issued += 1
        while do_gate and issued < gates_up:
            dep = gate(*blocks[issued])
            issued += 1
        for n in range(n_down):
            if do_down:
                down(n, dep)
            while do_gate and issued < gates_up + (n + 1) * (len(blocks) - gates_up) // n_down:
                dep = gate(*blocks[issued])
                issued += 1

    steady = jnp.logical_and(e >= 2, e < n_chunks)

    @pl.when(e == 0)
    def _():
        stages(sc0_ref, None, None, None, do_gate=False, do_down=False)

    @pl.when(e == 1)
    def _():
        stages(sc1_ref, sc0_ref, a0_ref, None, do_down=False)

    @pl.when(jnp.logical_and(steady, e % 2 == 0))
    def _():
        stages(sc0_ref, sc1_ref, a1_ref, a0_ref)

    @pl.when(jnp.logical_and(steady, e % 2 == 1))
    def _():
        stages(sc1_ref, sc0_ref, a0_ref, a1_ref)

    @pl.when(e == n_chunks)
    def _():
        stages(None, sc1_ref, a1_ref, a0_ref, do_up=False)

    @pl.when(e == n_chunks + 1)
    def _():
        stages(None, None, None, a1_ref, do_up=False, do_gate=False)


def peer_dense(hnt, u, v, n1, e1, r2, e2, h1, tt=512, ec=1024):
    D, T = hnt.shape
    E = u.shape[0]
    H, _, nk, _ = n1.shape
    tt = min(tt, T)
    n_chunks = E // ec
    assert n_chunks % 2 == 0 and n_chunks >= 2
    tok = pl.BlockSpec((H, tt // LANES, nk, LANES), lambda t, e: (0, t, 0, 0))
    ptok = pl.BlockSpec((H, tt // LANES, nk // 2, LANES), lambda t, e: (0, t, 0, 0))
    return pl.pallas_call(
        functools.partial(_peer_dense_body, nk=nk, rows_per_chunk=ec // nk, n_chunks=n_chunks),
        grid=(T // tt, n_chunks + 2),
        in_specs=[pl.BlockSpec((D, tt), lambda t, e: (0, t)),
                  pl.BlockSpec((ec, D), lambda t, e: (jnp.minimum(e, n_chunks - 1), 0)),
                  pl.BlockSpec((ec, D), lambda t, e: (jnp.clip(e - 2, 0, n_chunks - 1), 0)),
                  tok, tok, ptok, ptok,
                  pl.BlockSpec((tt, D), lambda t, e: (t, 0), pipeline_mode=pl.Buffered(1))],
        out_specs=pl.BlockSpec((tt, D), lambda t, e: (t, 0)),
        out_shape=jax.ShapeDtypeStruct((T, D), F32),
        scratch_shapes=[pltpu.VMEM((ec, tt), F32), pltpu.VMEM((ec, tt), F32),
                        pltpu.VMEM((tt, ec), BF16), pltpu.VMEM((tt, ec), BF16)],
        compiler_params=_params("parallel", "arbitrary"),
        name="peer_dense",
    )(hnt, u, v, n1, e1, r2, e2, h1)


def kernel(x, norm1_g, w_in, w_fourier, w_attn, q_norm_g, k_norm_g, lambda_q1, lambda_k1,
           lambda_q2, lambda_k2, subln_g, w_out, norm2_g, w_query, sub_keys, expert_u, expert_v):
    B, S, D = x.shape
    T = B * S
    depth = w_in.shape[0]
    dh = D // (2 * A_HEADS)
    f_width = w_fourier.shape[1]
    qk_width = A_HEADS * 2 * dh
    v_width = w_attn.shape[1]
    o_q = f_width
    o_k = o_q + qk_width
    o_v = o_k + qk_width
    o_g = o_v + v_width
    h = x.reshape(T, D)
    for i in range(depth):
        lam_init = 0.8 - 0.6 * math.exp(-0.3 * i)
        xn = rmsnorm_rows(h, norm1_g[i])
        w_in_b = w_in[i]
        qk_gain = jnp.concatenate([jnp.tile(q_norm_g[i].astype(F32) * dh ** -0.5, 2 * A_HEADS),
                                   jnp.tile(k_norm_g[i].astype(F32), 2 * A_HEADS)]).reshape(1, 2 * qk_width)
        z = project(xn, w_in_b, 0, f_width, "cast", out_dtype=F32)
        qk = project(xn, w_in_b, o_q, 2 * qk_width, "headnorm", gain=qk_gain, group=dh)
        vv = project(xn, w_in_b, o_v, v_width, "cast")
        gates = project(xn, w_in_b, o_g, 2 * D, "sigmoid")

        fm = fourier_mix(z, B, S)
        lamv = jnp.stack([lambda_q1[i], lambda_k1[i], lambda_q2[i], lambda_k2[i]]).astype(F32)
        bound = (jnp.max(jnp.abs(q_norm_g[i])) * jnp.max(jnp.abs(k_norm_g[i]))).astype(F32) * (1.01 * dh ** 0.5)
        attend = functools.partial(diff_attention, qk, vv, lamv, subln_g[i], bound, B, S, lam_init)
        on = lax.cond(bound <= MAX_SAFE_SHIFT, lambda: attend(True), lambda: attend(False))

        mixed = gated_merge(fm, on, gates, w_fourier[i].astype(BF16), w_attn[i].astype(BF16))
        h1, hn, hnt = out_projection(mixed, h, w_out[i].astype(BF16), norm2_g[i])

        n1, e1, r2, e2 = peer_query(hn, w_query[i].astype(BF16), sub_keys[i].astype(BF16))
        h = peer_dense(hnt, expert_u[i].astype(BF16), expert_v[i].astype(BF16), n1, e1, r2, e2, h1)
    return h.reshape(B, S, D)
```

```python
import functools
import math

import numpy as np
import jax
import jax.numpy as jnp
from jax import lax
from jax.experimental import pallas as pl
from jax.experimental.pallas import tpu as pltpu

EPS = 1e-6
F_GROUPS = 4
A_HEADS = 8
P_HEADS = 8
P_TOPK = 16
LANES = 128
SUBLANES = 8
MXU_COLS = 256
BF16_ROWS = 16
BF16_EXACT_INT = 256
UP_K = 256
MAX_SAFE_SHIFT = 40.0
FFT_INNER = 128
VMEM_LIMIT_BYTES = 56 * 1024 * 1024

F32 = jnp.float32
BF16 = jnp.bfloat16
NEG_INF = float("-inf")


def _params(*semantics):
    return pltpu.CompilerParams(dimension_semantics=semantics, vmem_limit_bytes=VMEM_LIMIT_BYTES)


def _rmsnorm_body(x_ref, g_ref, o_ref):
    x = x_ref[...]
    ms = jnp.mean(x * x, axis=-1, keepdims=True)
    o_ref[...] = (x * lax.rsqrt(ms + EPS) * g_ref[...]).astype(o_ref.dtype)


def rmsnorm_rows(x2d, g, tm=512):
    T, D = x2d.shape
    return pl.pallas_call(
        _rmsnorm_body,
        grid=(T // tm,),
        in_specs=[pl.BlockSpec((tm, D), lambda i: (i, 0)),
                  pl.BlockSpec((1, D), lambda i: (0, 0))],
        out_specs=pl.BlockSpec((tm, D), lambda i: (i, 0)),
        out_shape=jax.ShapeDtypeStruct((T, D), BF16),
        compiler_params=_params("parallel"),
        name="rmsnorm",
    )(x2d, g.reshape(1, D).astype(F32))


def _proj_body(x_ref, w_ref, g_ref, o_ref, wb_ref, *, mode, group):
    @pl.when(pl.program_id(1) == 0)
    def _():
        wb_ref[...] = w_ref[...].astype(BF16)

    acc = jnp.dot(x_ref[...], wb_ref[...], preferred_element_type=F32)
    if mode == "cast":
        o_ref[...] = acc.astype(o_ref.dtype)
    elif mode == "sigmoid":
        o_ref[...] = jax.nn.sigmoid(acc).astype(o_ref.dtype)
    else:
        for s in range(acc.shape[1] // group):
            blk = acc[:, s * group:(s + 1) * group]
            ms = jnp.mean(blk * blk, axis=-1, keepdims=True)
            o_ref[:, s * group:(s + 1) * group] = (
                blk * lax.rsqrt(ms + EPS) * g_ref[:, s * group:(s + 1) * group]).astype(o_ref.dtype)


def project(x, w, col0, ncols, mode, gain=None, group=LANES, out_dtype=BF16, tm=1024, tn=1024):
    T, K = x.shape
    tm = min(tm, T)
    if gain is None:
        gain = jnp.ones((1, ncols), F32)
    off = col0 // tn
    return pl.pallas_call(
        functools.partial(_proj_body, mode=mode, group=group),
        grid=(ncols // tn, T // tm),
        in_specs=[pl.BlockSpec((tm, K), lambda j, i: (i, 0)),
                  pl.BlockSpec((K, tn), lambda j, i: (0, off + j)),
                  pl.BlockSpec((1, tn), lambda j, i: (0, j))],
        out_specs=pl.BlockSpec((tm, tn), lambda j, i: (i, j)),
        out_shape=jax.ShapeDtypeStruct((T, ncols), out_dtype),
        scratch_shapes=[pltpu.VMEM((K, tn), BF16)],
        compiler_params=_params("parallel", "arbitrary"),
        name="proj_" + mode,
    )(x, w, gain)


def _fourier_constants(S, C):
    Na = S // FFT_INNER
    eye = np.eye(SUBLANES)
    n2 = np.arange(C, dtype=np.int64)
    ang = 2.0 * np.pi * ((n2[:, None] * n2[None, :]) % C) / C
    w0 = np.concatenate([np.cos(ang), -np.sin(ang)], axis=1)
    a = np.arange(Na, dtype=np.int64)
    phi = 2.0 * np.pi * ((a[:, None] * a[None, :]) % Na) / Na
    r = np.block([[np.cos(phi), np.sin(phi)], [-np.sin(phi), np.cos(phi)]])
    r4 = r.reshape(2, Na, 2, Na)
    rk = np.einsum('qcpa,xy->qxcpay', r4, eye).reshape(2 * SUBLANES * Na, 2 * Na * SUBLANES)
    d = np.arange(FFT_INNER, dtype=np.int64)
    psi = 2.0 * np.pi * ((d[:, None] * d[None, :]) % FFT_INNER) / FFT_INNER
    f2 = np.stack([np.cos(psi), np.sin(psi)], axis=1) / math.sqrt(S * C)
    kf = np.einsum('drb,xy->dxrby', f2, eye).reshape(FFT_INNER * SUBLANES, 2 * FFT_INNER * SUBLANES)
    return (jnp.asarray(w0, BF16), jnp.asarray(rk, BF16), jnp.asarray(kf, BF16))


def _twiddles(S):
    Na = S // FFT_INNER
    b = lax.broadcasted_iota(jnp.int32, (FFT_INNER, Na), 0)
    c = lax.broadcasted_iota(jnp.int32, (FFT_INNER, Na), 1)
    th = ((b * c) % S).astype(F32) * (2.0 * math.pi / S)
    shape = (FFT_INNER, Na, LANES)
    return (jnp.broadcast_to(jnp.cos(th)[:, :, None], shape), jnp.broadcast_to(jnp.sin(th)[:, :, None], shape))


def _fourier_outer_body(z_ref, w0_ref, rk_ref, o_ref, *, groups):
    na, sub, w = z_ref.shape
    c = w // groups
    x = z_ref[...].reshape(na * sub, w).astype(BF16)
    ps, qs = [], []
    for g in range(groups):
        pq = jnp.dot(x[:, g * c:(g + 1) * c], w0_ref[...], preferred_element_type=F32)
        ps.append(pq[:, :c])
        qs.append(pq[:, c:])
    x1 = jnp.concatenate([jnp.concatenate(ps, axis=1), jnp.concatenate(qs, axis=1)], axis=0)
    a = jnp.dot(rk_ref[...], x1.astype(BF16), preferred_element_type=F32)
    o_ref[...] = a.reshape(2, sub, na, w)


def _fourier_inner_body(a_ref, tc_ref, ts_ref, kf_ref, o_ref):
    _, nb, sub, c = a_ref.shape
    ar = a_ref[0].reshape(nb * sub, c)
    ai = a_ref[1].reshape(nb * sub, c)
    tc = jnp.tile(tc_ref[...].reshape(nb * sub, LANES), (1, c // LANES))
    ts = jnp.tile(ts_ref[...].reshape(nb * sub, LANES), (1, c // LANES))
    x3 = jnp.concatenate([ar * tc + ai * ts, ai * tc - ar * ts], axis=0).astype(BF16)
    y = jnp.dot(kf_ref[...], x3, preferred_element_type=F32)
    o_ref[...] = y.reshape(nb, sub, c)


def fourier_mix(z2d, B, S):
    T, W = z2d.shape
    G = F_GROUPS
    C = W // G
    Na = S // FFT_INNER
    assert Na % SUBLANES == 0
    w0, rk, kf = _fourier_constants(S, C)
    tc, ts = _twiddles(S)
    a5 = pl.pallas_call(
        functools.partial(_fourier_outer_body, groups=G),
        grid=(B, FFT_INNER // SUBLANES),
        in_specs=[pl.BlockSpec((None, Na, SUBLANES, W), lambda b, j: (b, 0, j, 0)),
                  pl.BlockSpec(w0.shape, lambda b, j: (0, 0)),
                  pl.BlockSpec(rk.shape, lambda b, j: (0, 0))],
        out_specs=pl.BlockSpec((None, 2, SUBLANES, Na, W), lambda b, j: (b, 0, j, 0, 0)),
        out_shape=jax.ShapeDtypeStruct((B, 2, FFT_INNER, Na, W), F32),
        compiler_params=_params("parallel", "arbitrary"),
        name="fourier_outer",
    )(z2d.reshape(B, Na, FFT_INNER, W), w0, rk)
    y = pl.pallas_call(
        _fourier_inner_body,
        grid=(B, Na // SUBLANES, G),
        in_specs=[pl.BlockSpec((None, 2, FFT_INNER, SUBLANES, C), lambda b, j, g: (b, 0, 0, j, g)),
                  pl.BlockSpec((FFT_INNER, SUBLANES, LANES), lambda b, j, g: (0, j, 0)),
                  pl.BlockSpec((FFT_INNER, SUBLANES, LANES), lambda b, j, g: (0, j, 0)),
                  pl.BlockSpec(kf.shape, lambda b, j, g: (0, 0))],
        out_specs=pl.BlockSpec((None, FFT_INNER, SUBLANES, C), lambda b, j, g: (b, 0, j, g)),
        out_shape=jax.ShapeDtypeStruct((B, FFT_INNER, Na, W), F32),
        compiler_params=_params("parallel", "parallel", "arbitrary"),
        name="fourier_inner",
    )(a5, tc, ts, kf)
    return y.reshape(T, W)


def _attn_body(slope_ref, shift_ref, lamv_ref, g_ref, qt_ref, ka_ref, q_ref, k_ref, v_ref, o_ref,
               m_ref, l_ref, acc_ref, bias_ref, *, tq, tk, dh, lam_init, fixed_shift):
    h = pl.program_id(1)
    qi = pl.program_id(2)
    ki = pl.program_id(3)
    dv = 2 * dh

    @pl.when(ki == 0)
    def _():
        m_ref[...] = jnp.full(m_ref.shape, NEG_INF, F32)
        l_ref[...] = jnp.zeros(l_ref.shape, F32)
        acc_ref[...] = jnp.zeros(acc_ref.shape, F32)

    slope = slope_ref[h]

    def online_softmax_update(scores):
        v = v_ref[...]
        for c in range(2):
            s = scores[c]
            m_prev = m_ref[c]
            m_new = jnp.maximum(m_prev, jnp.max(s, axis=-1, keepdims=True))
            alpha = jnp.exp(m_prev - m_new)
            p = jnp.exp(s - jnp.tile(m_new, (1, tk // LANES)))
            l_ref[c] = alpha * l_ref[c] + jnp.sum(p, axis=-1, keepdims=True)
            acc_ref[c] = (jnp.tile(alpha, (1, dv // LANES)) * acc_ref[c]
                          + jnp.dot(p.astype(BF16), v, preferred_element_type=F32))
            m_ref[c] = m_new

    def shifted_update(scores):
        v = v_ref[...]
        for c in range(2):
            p = jnp.exp(scores[c])
            part = p[:, 0:LANES]
            for t in range(1, tk // LANES):
                part = part + p[:, t * LANES:(t + 1) * LANES]
            l_ref[c] += part
            acc_ref[c] += jnp.dot(p.astype(BF16), v, preferred_element_type=F32)

    update = shifted_update if fixed_shift else online_softmax_update
    nt = (((1,), (1,)), ((), ()))

    @pl.when(qi != ki)
    def _():
        off = (qi - ki) * tq
        sgn = jnp.where(off > 0, slope, -slope)
        qa = ((qt_ref[0] + off.astype(F32) * qt_ref[1]) * sgn + qt_ref[2]).astype(BF16)
        ka = ka_ref[...]
        q = q_ref[...]
        k = k_ref[...]
        scores = []
        for c in range(2):
            qc = jnp.concatenate([q[:, c * dh:(c + 1) * dh], qa], axis=1)
            kc = jnp.concatenate([k[:, c * dh:(c + 1) * dh], ka], axis=1)
            scores.append(lax.dot_general(qc, kc, nt, preferred_element_type=F32))
        update(scores)

    @pl.when(qi == ki)
    def _():
        @pl.when(qi == 0)
        def _():
            row = lax.broadcasted_iota(jnp.int32, (tq, tk), 0)
            col = lax.broadcasted_iota(jnp.int32, (tq, tk), 1)
            bias_ref[...] = jnp.abs(row - col).astype(F32) * (-slope) - shift_ref[0]

        bias = bias_ref[...]
        q = q_ref[...]
        k = k_ref[...]
        scores = [lax.dot_general(q[:, c * dh:(c + 1) * dh], k[:, c * dh:(c + 1) * dh], nt,
                                  preferred_element_type=F32) + bias for c in range(2)]
        update(scores)

    @pl.when(ki == pl.num_programs(3) - 1)
    def _():
        lv = lamv_ref[...]
        lam = (jnp.exp(jnp.sum(lv[0:1] * lv[1:2], axis=-1, keepdims=True))
               - jnp.exp(jnp.sum(lv[2:3] * lv[3:4], axis=-1, keepdims=True)) + lam_init)
        if fixed_shift:
            l0 = jnp.sum(l_ref[0], axis=-1, keepdims=True)
            l1 = jnp.sum(l_ref[1], axis=-1, keepdims=True)
        else:
            l0 = jnp.tile(l_ref[0], (1, dv // LANES))
            l1 = jnp.tile(l_ref[1], (1, dv // LANES))
        o = acc_ref[0] / l0 - lam * (acc_ref[1] / l1)
        ms = jnp.mean(o * o, axis=-1, keepdims=True)
        o_ref[...] = (o * lax.rsqrt(ms + EPS) * g_ref[...] * (1.0 - lam_init)).astype(o_ref.dtype)


SHIFT_LANES = (4, 5, 6)


def _split_bf16(x, parts):
    out = []
    for _ in range(parts):
        p = x.astype(BF16).astype(F32)
        out.append(p)
        x = x - p
    return out


def _alibi_templates(tq, tk):
    r = np.arange(tq)
    qt = np.zeros((2, tq, LANES), np.float32)
    qt[0, :, 0] = -(r // BF16_EXACT_INT) * BF16_EXACT_INT
    qt[0, :, 1] = -(r % BF16_EXACT_INT)
    qt[0, :, 2] = 1.0
    qt[0, :, 3] = 1.0
    qt[1, :, 0] = -1.0
    c = np.arange(tk)
    ka = np.zeros((tk, LANES), np.float32)
    ka[:, 0] = 1.0
    ka[:, 1] = 1.0
    ka[:, 2] = c % BF16_EXACT_INT
    ka[:, 3] = (c // BF16_EXACT_INT) * BF16_EXACT_INT
    ka[:, SHIFT_LANES[0]:SHIFT_LANES[-1] + 1] = 1.0
    return jnp.asarray(qt, F32), jnp.asarray(ka, BF16)


def diff_attention(qk2d, v2d, lamv, subln_g, shift, B, S, lam_init, fixed_shift, tile=1024):
    T, W = qk2d.shape
    H = A_HEADS
    dh = W // (4 * H)
    dv = 2 * dh
    tq = tk = min(tile, S)
    nq, nk = S // tq, S // tk
    slopes = 2.0 ** (-8.0 * np.arange(1, H + 1) / H)
    assert np.all(np.log2(slopes) == np.round(np.log2(slopes))) and S <= BF16_EXACT_INT ** 2
    qt, ka = _alibi_templates(tq, tk)
    parts = _split_bf16(shift if fixed_shift else jnp.zeros((), F32), len(SHIFT_LANES))
    lane = lax.broadcasted_iota(jnp.int32, (1, tq, LANES), 2)
    shift_plane = sum(jnp.where(lane == ln, -p, 0.0) for ln, p in zip(SHIFT_LANES, parts))
    qt = jnp.concatenate([qt, shift_plane.astype(F32)], axis=0)
    applied = sum(parts).reshape(1)
    return pl.pallas_call(
        functools.partial(_attn_body, tq=tq, tk=tk, dh=dh, lam_init=lam_init, fixed_shift=fixed_shift),
        grid=(B, H, nq, nk),
        in_specs=[pl.BlockSpec(memory_space=pltpu.SMEM),
                  pl.BlockSpec(memory_space=pltpu.SMEM),
                  pl.BlockSpec((4, dh), lambda b, h, i, j: (0, 0)),
                  pl.BlockSpec((1, dv), lambda b, h, i, j: (0, 0)),
                  pl.BlockSpec((3, tq, LANES), lambda b, h, i, j: (0, 0, 0)),
                  pl.BlockSpec((tk, LANES), lambda b, h, i, j: (0, 0)),
                  pl.BlockSpec((tq, dv), lambda b, h, i, j: (b * nq + i, h)),
                  pl.BlockSpec((tk, dv), lambda b, h, i, j: (b * nk + j, H + h)),
                  pl.BlockSpec((tk, dv), lambda b, h, i, j: (b * nk + j, h))],
        out_specs=pl.BlockSpec((tq, dv), lambda b, h, i, j: (b * nq + i, h)),
        out_shape=jax.ShapeDtypeStruct((T, H * dv), BF16),
        scratch_shapes=[pltpu.VMEM((2, tq, LANES), F32), pltpu.VMEM((2, tq, LANES), F32),
                        pltpu.VMEM((2, tq, dv), F32), pltpu.VMEM((tq, tk), F32)],
        compiler_params=_params("parallel", "parallel", "arbitrary", "arbitrary"),
        name="diff_attention_shifted" if fixed_shift else "diff_attention_online",
    )(jnp.asarray(slopes, F32), applied, lamv, subln_g.reshape(1, dv).astype(F32), qt, ka, qk2d, qk2d, v2d)


def _merge_body(fm_ref, on_ref, gf_ref, ga_ref, wf_ref, wa_ref, o_ref):
    yf = jnp.dot(fm_ref[...].astype(BF16), wf_ref[...], preferred_element_type=F32)
    ya = jnp.dot(on_ref[...], wa_ref[...], preferred_element_type=F32)
    o_ref[...] = (gf_ref[...].astype(F32) * yf + ga_ref[...].astype(F32) * ya).astype(o_ref.dtype)


def gated_merge(fm, on, gates, wf, wa, tm=512):
    T, D = on.shape
    tm = min(tm, T)
    return pl.pallas_call(
        _merge_body,
        grid=(T // tm,),
        in_specs=[pl.BlockSpec((tm, fm.shape[1]), lambda i: (i, 0)),
                  pl.BlockSpec((tm, D), lambda i: (i, 0)),
                  pl.BlockSpec((tm, D), lambda i: (i, 0)),
                  pl.BlockSpec((tm, D), lambda i: (i, 1)),
                  pl.BlockSpec(wf.shape, lambda i: (0, 0), pipeline_mode=pl.Buffered(1)),
                  pl.BlockSpec(wa.shape, lambda i: (0, 0), pipeline_mode=pl.Buffered(1))],
        out_specs=pl.BlockSpec((tm, D), lambda i: (i, 0)),
        out_shape=jax.ShapeDtypeStruct((T, D), BF16),
        compiler_params=_params("parallel"),
        name="gated_merge",
    )(fm, on, gates, gates, wf, wa)


def _out_body(mx_ref, x_ref, wo_ref, g_ref, h_ref, hn_ref, hnt_ref):
    h = x_ref[...] + jnp.dot(mx_ref[...], wo_ref[...], preferred_element_type=F32)
    h_ref[...] = h
    ms = jnp.mean(h * h, axis=-1, keepdims=True)
    hn = h * lax.rsqrt(ms + EPS) * g_ref[...]
    hn_ref[...] = hn.astype(hn_ref.dtype)
    hnt_ref[...] = hn.T.astype(hnt_ref.dtype)


def out_projection(mixed, x2d, wo, g2, tm=512):
    T, D = x2d.shape
    tm = min(tm, T)
    return pl.pallas_call(
        _out_body,
        grid=(T // tm,),
        in_specs=[pl.BlockSpec((tm, D), lambda i: (i, 0)),
                  pl.BlockSpec((tm, D), lambda i: (i, 0)),
                  pl.BlockSpec((D, D), lambda i: (0, 0)),
                  pl.BlockSpec((1, D), lambda i: (0, 0))],
        out_specs=[pl.BlockSpec((tm, D), lambda i: (i, 0)),
                   pl.BlockSpec((tm, D), lambda i: (i, 0)),
                   pl.BlockSpec((D, tm), lambda i: (0, i))],
        out_shape=[jax.ShapeDtypeStruct((T, D), F32), jax.ShapeDtypeStruct((T, D), BF16),
                   jax.ShapeDtypeStruct((D, T), BF16)],
        compiler_params=_params("parallel"),
        name="out_projection",
    )(mixed, x2d, wo, g2.reshape(1, D).astype(F32))


def _extract_top(x, dst_ref, n, want_rank=False):
    rank = jnp.full(x.shape, float(n), F32) if want_rank else None
    for r in range(n):
        mx = jnp.max(x, axis=0, keepdims=True)
        dst_ref[pl.ds(r, 1), :] = mx
        hit = x >= mx
        if want_rank:
            rank = jnp.where(hit, float(r), rank)
        x = jnp.where(hit, NEG_INF, x)
    return rank


def _peer_query_body(hn_ref, wq_ref, keys_ref, n1_ref, e1_ref, r2_ref, e2_ref, sv_ref, cs_ref):
    qp = jnp.dot(hn_ref[...], wq_ref[...], preferred_element_type=F32).astype(BF16)
    half = keys_ref.shape[-1]
    nt = (((1,), (1,)), ((), ()))
    for h in range(P_HEADS):
        lo = 2 * h * half
        st1 = lax.dot_general(keys_ref[h, 0], qp[:, lo:lo + half], nt, preferred_element_type=F32)
        st2 = lax.dot_general(keys_ref[h, 1], qp[:, lo + half:lo + 2 * half], nt,
                              preferred_element_type=F32)
        _extract_top(st1, sv_ref.at[0], P_TOPK)
        r2 = _extract_top(st2, sv_ref.at[1], P_TOPK, want_rank=True)
        sv1 = sv_ref[0]
        sv2 = sv_ref[1]
        parts = [sv1[0:1] + sv2]
        for a in range(1, 8):
            parts.append(sv1[a:a + 1] + sv2[0:8])
        parts.append(sv1[8:16] + sv2[0:1])
        _extract_top(jnp.concatenate(parts, axis=0), cs_ref, P_TOPK)
        cs = cs_ref[...]
        tau = cs[P_TOPK - 1:P_TOPK]
        z = jnp.sum(jnp.exp(cs - cs[0:1]), axis=0, keepdims=True)
        n1 = jnp.zeros(st1.shape, F32)
        for a in range(P_TOPK):
            va = sv1[a:a + 1]
            n_a = jnp.sum(jnp.where(va + sv2 >= tau, 1.0, 0.0), axis=0, keepdims=True)
            n1 = jnp.where(st1 == va, n_a, n1)
        e1 = jnp.exp(st1 - sv1[0:1])
        e2 = jnp.exp(st2 - sv2[0:1]) / z
        for tb in range(st1.shape[1] // LANES):
            cols = slice(tb * LANES, (tb + 1) * LANES)
            n1_ref[h, tb] = n1[:, cols]
            e1_ref[h, tb] = e1[:, cols]
            r2_ref[h, tb] = pltpu.bitcast(r2[:, cols].astype(BF16), jnp.uint32)
            e2_ref[h, tb] = pltpu.bitcast(e2[:, cols].astype(BF16), jnp.uint32)


def peer_query(hn, wq, keys, tt=256):
    T, D = hn.shape
    H, _, nk, half = keys.shape
    tt = min(tt, T)
    spec = pl.BlockSpec((H, tt // LANES, nk, LANES), lambda i: (0, i, 0, 0))
    rows = jax.ShapeDtypeStruct((H, T // LANES, nk, LANES), F32)
    pspec = pl.BlockSpec((H, tt // LANES, nk // 2, LANES), lambda i: (0, i, 0, 0))
    tiles = jax.ShapeDtypeStruct((H, T // LANES, nk // 2, LANES), jnp.uint32)
    return pl.pallas_call(
        _peer_query_body,
        grid=(T // tt,),
        in_specs=[pl.BlockSpec((tt, D), lambda i: (i, 0)),
                  pl.BlockSpec(wq.shape, lambda i: (0, 0)),
                  pl.BlockSpec(keys.shape, lambda i: (0, 0, 0, 0))],
        out_specs=[spec, spec, pspec, pspec],
        out_shape=[rows, rows, tiles, tiles],
        scratch_shapes=[pltpu.VMEM((2, P_TOPK, tt), F32), pltpu.VMEM((P_TOPK, tt), F32)],
        compiler_params=_params("parallel"),
        name="peer_query",
    )(hn, wq, keys)


def _peer_dense_body(hnt_ref, u_ref, v_ref, n1_ref, e1_ref, r2_ref, e2_ref, h_ref, o_ref,
                     sc0_ref, sc1_ref, a0_ref, a1_ref, *, nk, rows_per_chunk, n_chunks):
    e = pl.program_id(1)

    @pl.when(e == 0)
    def _():
        o_ref[...] = h_ref[...]

    def stages(sc_w, sc_r, a_w, a_r, do_up=True, do_gate=True, do_down=True):
        tt = o_ref.shape[0]
        chunk = e - 1

        def anchored(lhs, dep):
            if dep is None:
                return lhs
            z = jnp.tile(dep, (BF16_ROWS // dep.shape[0], lhs.shape[1] // LANES)).astype(BF16)
            return jnp.concatenate([lhs[0:BF16_ROWS, :] + z, lhs[BF16_ROWS:, :]], axis=0)

        def up_piece(half, kb, dep):
            rows = slice(half * (tt // 2), (half + 1) * (tt // 2))
            ks = slice(kb * UP_K, (kb + 1) * UP_K)
            return jnp.dot(anchored(u_ref[:, ks], dep), hnt_ref[ks, rows], preferred_element_type=F32)

        def gate(tb, ii):
            i = chunk * rows_per_chunk + ii
            def key_row(ref, h):
                row = jnp.broadcast_to(ref[h, tb, pl.ds(i, 1), :], (BF16_ROWS, LANES)).astype(BF16)
                return jnp.tile(row, (nk // BF16_ROWS, 1))

            w = jnp.zeros((nk, LANES), BF16)
            for h in range(P_HEADS):
                w = w + jnp.where(pltpu.bitcast(r2_ref[h, tb], BF16) < key_row(n1_ref, h),
                                  pltpu.bitcast(e2_ref[h, tb], BF16) * key_row(e1_ref, h), 0.0)
            x = sc_r[ii * nk:(ii + 1) * nk, tb * LANES:(tb + 1) * LANES]
            act = 0.5 * x * (1.0 + lax.erf(x * (1.0 / math.sqrt(2.0))))
            prod = (act * w.astype(F32)).T
            a_w[tb * LANES:(tb + 1) * LANES, ii * nk:(ii + 1) * nk] = prod.astype(BF16)
            bits = pltpu.bitcast(prod[0:8, :], jnp.uint32)
            bits = lax.shift_right_logical(lax.shift_right_logical(bits, jnp.uint32(16)), jnp.uint32(16))
            return pltpu.bitcast(bits, F32)

        def down(n, dep):
            cols = slice(n * MXU_COLS, (n + 1) * MXU_COLS)
            o_ref[:, cols] += jnp.dot(anchored(a_r[...], dep), v_ref[:, cols], preferred_element_type=F32)

        blocks = [(tb, ii) for tb in range(tt // LANES) for ii in range(rows_per_chunk)]
        n_down = o_ref.shape[1] // MXU_COLS
        n_kb = u_ref.shape[1] // UP_K
        per_gate = 2 * n_kb // (len(blocks) - n_down)
        dep = None
        g = 0
        for half in range(2 if do_up else 0):
            acc = None
            for kb in range(n_kb):
                piece = up_piece(half, kb, dep)
                acc = piece if acc is None else acc + piece
                if do_gate and (half * n_kb + kb + 1) % per_gate == 0:
                    dep = gate(*blocks[g])
                    g += 1
            sc_w[:, half * (tt // 2):(half + 1) * (tt // 2)] = acc
        if do_gate and not do_up:
            for blk in blocks[:len(blocks) - n_down]:
                dep = gate(*blk)
            g = len(blocks) - n_down
        for n in range(n_down):
            if do_down:
                down(n, dep)
            if do_gate:
                dep = gate(*blocks[g + n])

    steady = jnp.logical_and(e >= 2, e < n_chunks)

    @pl.when(e == 0)
    def _():
        stages(sc0_ref, None, None, None, do_gate=False, do_down=False)

    @pl.when(e == 1)
    def _():
        stages(sc1_ref, sc0_ref, a0_ref, None, do_down=False)

    @pl.when(jnp.logical_and(steady, e % 2 == 0))
    def _():
        stages(sc0_ref, sc1_ref, a1_ref, a0_ref)

    @pl.when(jnp.logical_and(steady, e % 2 == 1))
    def _():
        stages(sc1_ref, sc0_ref, a0_ref, a1_ref)

    @pl.when(e == n_chunks)
    def _():
        stages(None, sc1_ref, a1_ref, a0_ref, do_up=False)

    @pl.when(e == n_chunks + 1)
    def _():
        stages(None, None, None, a1_ref, do_up=False, do_gate=False)


def peer_dense(hnt, u, v, n1, e1, r2, e2, h1, tt=512, ec=512):
    D, T = hnt.shape
    E = u.shape[0]
    H, _, nk, _ = n1.shape
    tt = min(tt, T)
    n_chunks = E // ec
    assert n_chunks % 2 == 0 and n_chunks >= 2
    tok = pl.BlockSpec((H, tt // LANES, nk, LANES), lambda t, e: (0, t, 0, 0))
    ptok = pl.BlockSpec((H, tt // LANES, nk // 2, LANES), lambda t, e: (0, t, 0, 0))
    return pl.pallas_call(
        functools.partial(_peer_dense_body, nk=nk, rows_per_chunk=ec // nk, n_chunks=n_chunks),
        grid=(T // tt, n_chunks + 2),
        in_specs=[pl.BlockSpec((D, tt), lambda t, e: (0, t)),
                  pl.BlockSpec((ec, D), lambda t, e: (jnp.minimum(e, n_chunks - 1), 0)),
                  pl.BlockSpec((ec, D), lambda t, e: (jnp.clip(e - 2, 0, n_chunks - 1), 0)),
                  tok, tok, ptok, ptok,
                  pl.BlockSpec((tt, D), lambda t, e: (t, 0))],
        out_specs=pl.BlockSpec((tt, D), lambda t, e: (t, 0)),
        out_shape=jax.ShapeDtypeStruct((T, D), F32),
        scratch_shapes=[pltpu.VMEM((ec, tt), F32), pltpu.VMEM((ec, tt), F32),
                        pltpu.VMEM((tt, ec), BF16), pltpu.VMEM((tt, ec), BF16)],
        compiler_params=_params("parallel", "arbitrary"),
        name="peer_dense",
    )(hnt, u, v, n1, e1, r2, e2, h1)


def kernel(x, norm1_g, w_in, w_fourier, w_attn, q_norm_g, k_norm_g, lambda_q1, lambda_k1,
           lambda_q2, lambda_k2, subln_g, w_out, norm2_g, w_query, sub_keys, expert_u, expert_v):
    B, S, D = x.shape
    T = B * S
    depth = w_in.shape[0]
    dh = D // (2 * A_HEADS)
    f_width = w_fourier.shape[1]
    qk_width = A_HEADS * 2 * dh
    v_width = w_attn.shape[1]
    o_q = f_width
    o_k = o_q + qk_width
    o_v = o_k + qk_width
    o_g = o_v + v_width
    h = x.reshape(T, D)
    for i in range(depth):
        lam_init = 0.8 - 0.6 * math.exp(-0.3 * i)
        xn = rmsnorm_rows(h, norm1_g[i])
        w_in_i = w_in[i]
        qk_gain = jnp.concatenate([jnp.tile(q_norm_g[i].astype(F32) * dh ** -0.5, 2 * A_HEADS),
                                   jnp.tile(k_norm_g[i].astype(F32), 2 * A_HEADS)]).reshape(1, 2 * qk_width)
        z = project(xn, w_in_i, 0, f_width, "cast", out_dtype=F32)
        qk = project(xn, w_in_i, o_q, 2 * qk_width, "headnorm", gain=qk_gain, group=dh)
        vv = project(xn, w_in_i, o_v, v_width, "cast")
        gates = project(xn, w_in_i, o_g, 2 * D, "sigmoid")

        fm = fourier_mix(z, B, S)
        lamv = jnp.stack([lambda_q1[i], lambda_k1[i], lambda_q2[i], lambda_k2[i]]).astype(F32)
        bound = (jnp.max(jnp.abs(q_norm_g[i])) * jnp.max(jnp.abs(k_norm_g[i]))).astype(F32) * (1.01 * dh ** 0.5)
        attend = functools.partial(diff_attention, qk, vv, lamv, subln_g[i], bound, B, S, lam_init)
        on = lax.cond(bound <= MAX_SAFE_SHIFT, lambda: attend(True), lambda: attend(False))

        mixed = gated_merge(fm, on, gates, w_fourier[i].astype(BF16), w_attn[i].astype(BF16))
        h1, hn, hnt = out_projection(mixed, h, w_out[i].astype(BF16), norm2_g[i])

        n1, e1, r2, e2 = peer_query(hn, w_query[i].astype(BF16), sub_keys[i].astype(BF16))
        h = peer_dense(hnt, expert_u[i].astype(BF16), expert_v[i].astype(BF16), n1, e1, r2, e2, h1)
    return h.reshape(B, S, D)
```

```python
import functools
import math

import numpy as np
import jax
import jax.numpy as jnp
from jax import lax
from jax.experimental import pallas as pl
from jax.experimental.pallas import tpu as pltpu

EPS = 1e-6
F_GROUPS = 4
A_HEADS = 8
P_HEADS = 8
P_TOPK = 16
LANES = 128
SUBLANES = 8
MXU_COLS = 256
BF16_ROWS = 16
BF16_EXACT_INT = 256
UP_K = 256
DOWN_K = 512
MAX_SAFE_SHIFT = 40.0
FFT_INNER = 128
VMEM_LIMIT_BYTES = 56 * 1024 * 1024

F32 = jnp.float32
BF16 = jnp.bfloat16
NEG_INF = float("-inf")


def _params(*semantics):
    return pltpu.CompilerParams(dimension_semantics=semantics, vmem_limit_bytes=VMEM_LIMIT_BYTES)


def _rmsnorm_body(x_ref, g_ref, o_ref):
    x = x_ref[...]
    ms = jnp.mean(x * x, axis=-1, keepdims=True)
    o_ref[...] = (x * lax.rsqrt(ms + EPS) * g_ref[...]).astype(o_ref.dtype)


def rmsnorm_rows(x2d, g, tm=512):
    T, D = x2d.shape
    return pl.pallas_call(
        _rmsnorm_body,
        grid=(T // tm,),
        in_specs=[pl.BlockSpec((tm, D), lambda i: (i, 0)),
                  pl.BlockSpec((1, D), lambda i: (0, 0))],
        out_specs=pl.BlockSpec((tm, D), lambda i: (i, 0)),
        out_shape=jax.ShapeDtypeStruct((T, D), BF16),
        compiler_params=_params("parallel"),
        name="rmsnorm",
    )(x2d, g.reshape(1, D).astype(F32))


def _proj_body(x_ref, w_ref, g_ref, o_ref, wb_ref, *, mode, group):
    @pl.when(pl.program_id(1) == 0)
    def _():
        wb_ref[...] = w_ref[...].astype(BF16)

    acc = jnp.dot(x_ref[...], wb_ref[...], preferred_element_type=F32)
    if mode == "cast":
        o_ref[...] = acc.astype(o_ref.dtype)
    elif mode == "sigmoid":
        o_ref[...] = jax.nn.sigmoid(acc).astype(o_ref.dtype)
    else:
        for s in range(acc.shape[1] // group):
            blk = acc[:, s * group:(s + 1) * group]
            ms = jnp.mean(blk * blk, axis=-1, keepdims=True)
            o_ref[:, s * group:(s + 1) * group] = (
                blk * lax.rsqrt(ms + EPS) * g_ref[:, s * group:(s + 1) * group]).astype(o_ref.dtype)


def project(x, w, col0, ncols, mode, gain=None, group=LANES, out_dtype=BF16, tm=1024, tn=1024):
    T, K = x.shape
    tm = min(tm, T)
    if gain is None:
        gain = jnp.ones((1, ncols), F32)
    off = col0 // tn
    return pl.pallas_call(
        functools.partial(_proj_body, mode=mode, group=group),
        grid=(ncols // tn, T // tm),
        in_specs=[pl.BlockSpec((tm, K), lambda j, i: (i, 0)),
                  pl.BlockSpec((K, tn), lambda j, i: (0, off + j)),
                  pl.BlockSpec((1, tn), lambda j, i: (0, j))],
        out_specs=pl.BlockSpec((tm, tn), lambda j, i: (i, j)),
        out_shape=jax.ShapeDtypeStruct((T, ncols), out_dtype),
        scratch_shapes=[pltpu.VMEM((K, tn), BF16)],
        compiler_params=_params("parallel", "arbitrary"),
        name="proj_" + mode,
    )(x, w, gain)


def _fourier_constants(S, C):
    Na = S // FFT_INNER
    eye = np.eye(SUBLANES)
    n2 = np.arange(C, dtype=np.int64)
    ang = 2.0 * np.pi * ((n2[:, None] * n2[None, :]) % C) / C
    w0 = np.concatenate([np.cos(ang), -np.sin(ang)], axis=1)
    a = np.arange(Na, dtype=np.int64)
    phi = 2.0 * np.pi * ((a[:, None] * a[None, :]) % Na) / Na
    r = np.block([[np.cos(phi), np.sin(phi)], [-np.sin(phi), np.cos(phi)]])
    r4 = r.reshape(2, Na, 2, Na)
    rk = np.einsum('qcpa,xy->qxcpay', r4, eye).reshape(2 * SUBLANES * Na, 2 * Na * SUBLANES)
    d = np.arange(FFT_INNER, dtype=np.int64)
    psi = 2.0 * np.pi * ((d[:, None] * d[None, :]) % FFT_INNER) / FFT_INNER
    f2 = np.stack([np.cos(psi), np.sin(psi)], axis=1) / math.sqrt(S * C)
    kf = np.einsum('drb,xy->dxrby', f2, eye).reshape(FFT_INNER * SUBLANES, 2 * FFT_INNER * SUBLANES)
    return (jnp.asarray(w0, BF16), jnp.asarray(rk, BF16), jnp.asarray(kf, BF16))


def _twiddles(S):
    Na = S // FFT_INNER
    b = lax.broadcasted_iota(jnp.int32, (FFT_INNER, Na), 0)
    c = lax.broadcasted_iota(jnp.int32, (FFT_INNER, Na), 1)
    th = ((b * c) % S).astype(F32) * (2.0 * math.pi / S)
    shape = (FFT_INNER, Na, LANES)
    return (jnp.broadcast_to(jnp.cos(th)[:, :, None], shape), jnp.broadcast_to(jnp.sin(th)[:, :, None], shape))


def _fourier_outer_body(z_ref, w0_ref, rk_ref, o_ref, *, groups):
    na, sub, w = z_ref.shape
    c = w // groups
    x = z_ref[...].reshape(na * sub, w).astype(BF16)
    ps, qs = [], []
    for g in range(groups):
        pq = jnp.dot(x[:, g * c:(g + 1) * c], w0_ref[...], preferred_element_type=F32)
        ps.append(pq[:, :c])
        qs.append(pq[:, c:])
    x1 = jnp.concatenate([jnp.concatenate(ps, axis=1), jnp.concatenate(qs, axis=1)], axis=0)
    a = jnp.dot(rk_ref[...], x1.astype(BF16), preferred_element_type=F32)
    o_ref[...] = a.reshape(2, sub, na, w)


def _fourier_inner_body(a_ref, tc_ref, ts_ref, kf_ref, o_ref):
    _, nb, sub, c = a_ref.shape
    ar = a_ref[0].reshape(nb * sub, c)
    ai = a_ref[1].reshape(nb * sub, c)
    tc = jnp.tile(tc_ref[...].reshape(nb * sub, LANES), (1, c // LANES))
    ts = jnp.tile(ts_ref[...].reshape(nb * sub, LANES), (1, c // LANES))
    x3 = jnp.concatenate([ar * tc + ai * ts, ai * tc - ar * ts], axis=0).astype(BF16)
    y = jnp.dot(kf_ref[...], x3, preferred_element_type=F32)
    o_ref[...] = y.reshape(nb, sub, c)


def fourier_mix(z2d, B, S):
    T, W = z2d.shape
    G = F_GROUPS
    C = W // G
    Na = S // FFT_INNER
    assert Na % SUBLANES == 0
    w0, rk, kf = _fourier_constants(S, C)
    tc, ts = _twiddles(S)
    a5 = pl.pallas_call(
        functools.partial(_fourier_outer_body, groups=G),
        grid=(B, FFT_INNER // SUBLANES),
        in_specs=[pl.BlockSpec((None, Na, SUBLANES, W), lambda b, j: (b, 0, j, 0)),
                  pl.BlockSpec(w0.shape, lambda b, j: (0, 0)),
                  pl.BlockSpec(rk.shape, lambda b, j: (0, 0))],
        out_specs=pl.BlockSpec((None, 2, SUBLANES, Na, W), lambda b, j: (b, 0, j, 0, 0)),
        out_shape=jax.ShapeDtypeStruct((B, 2, FFT_INNER, Na, W), F32),
        compiler_params=_params("parallel", "arbitrary"),
        name="fourier_outer",
    )(z2d.reshape(B, Na, FFT_INNER, W), w0, rk)
    y = pl.pallas_call(
        _fourier_inner_body,
        grid=(B, Na // SUBLANES, G),
        in_specs=[pl.BlockSpec((None, 2, FFT_INNER, SUBLANES, C), lambda b, j, g: (b, 0, 0, j, g)),
                  pl.BlockSpec((FFT_INNER, SUBLANES, LANES), lambda b, j, g: (0, j, 0)),
                  pl.BlockSpec((FFT_INNER, SUBLANES, LANES), lambda b, j, g: (0, j, 0)),
                  pl.BlockSpec(kf.shape, lambda b, j, g: (0, 0))],
        out_specs=pl.BlockSpec((None, FFT_INNER, SUBLANES, C), lambda b, j, g: (b, 0, j, g)),
        out_shape=jax.ShapeDtypeStruct((B, FFT_INNER, Na, W), F32),
        compiler_params=_params("parallel", "parallel", "arbitrary"),
        name="fourier_inner",
    )(a5, tc, ts, kf)
    return y.reshape(T, W)


def _attn_body(slope_ref, shift_ref, lamv_ref, g_ref, qt_ref, ka_ref, q_ref, k_ref, v_ref, o_ref,
               m_ref, l_ref, acc_ref, bias_ref, *, tq, tk, dh, lam_init, fixed_shift):
    h = pl.program_id(1)
    qi = pl.program_id(2)
    ki = pl.program_id(3)
    dv = 2 * dh

    @pl.when(ki == 0)
    def _():
        m_ref[...] = jnp.full(m_ref.shape, NEG_INF, F32)
        l_ref[...] = jnp.zeros(l_ref.shape, F32)
        acc_ref[...] = jnp.zeros(acc_ref.shape, F32)

    slope = slope_ref[h]

    def online_softmax_update(scores):
        v = v_ref[...]
        for c in range(2):
            s = scores[c]
            m_prev = m_ref[c]
            m_new = jnp.maximum(m_prev, jnp.max(s, axis=-1, keepdims=True))
            alpha = jnp.exp(m_prev - m_new)
            p = jnp.exp(s - jnp.tile(m_new, (1, tk // LANES)))
            l_ref[c] = alpha * l_ref[c] + jnp.sum(p, axis=-1, keepdims=True)
            acc_ref[c] = (jnp.tile(alpha, (1, dv // LANES)) * acc_ref[c]
                          + jnp.dot(p.astype(BF16), v, preferred_element_type=F32))
            m_ref[c] = m_new

    def shifted_update(scores):
        v = v_ref[...]
        for c in range(2):
            p = jnp.exp(scores[c])
            part = p[:, 0:LANES]
            for t in range(1, tk // LANES):
                part = part + p[:, t * LANES:(t + 1) * LANES]
            l_ref[c] += part
            acc_ref[c] += jnp.dot(p.astype(BF16), v, preferred_element_type=F32)

    update = shifted_update if fixed_shift else online_softmax_update
    nt = (((1,), (1,)), ((), ()))

    @pl.when(qi != ki)
    def _():
        off = (qi - ki) * tq
        sgn = jnp.where(off > 0, slope, -slope)
        qa = ((qt_ref[0] + off.astype(F32) * qt_ref[1]) * sgn + qt_ref[2]).astype(BF16)
        ka = ka_ref[...]
        q = q_ref[...]
        k = k_ref[...]
        scores = []
        for c in range(2):
            qc = jnp.concatenate([q[:, c * dh:(c + 1) * dh], qa], axis=1)
            kc = jnp.concatenate([k[:, c * dh:(c + 1) * dh], ka], axis=1)
            scores.append(lax.dot_general(qc, kc, nt, preferred_element_type=F32))
        update(scores)

    @pl.when(qi == ki)
    def _():
        @pl.when(qi == 0)
        def _():
            row = lax.broadcasted_iota(jnp.int32, (tq, tk), 0)
            col = lax.broadcasted_iota(jnp.int32, (tq, tk), 1)
            bias_ref[...] = jnp.abs(row - col).astype(F32) * (-slope) - shift_ref[0]

        bias = bias_ref[...]
        q = q_ref[...]
        k = k_ref[...]
        scores = [lax.dot_general(q[:, c * dh:(c + 1) * dh], k[:, c * dh:(c + 1) * dh], nt,
                                  preferred_element_type=F32) + bias for c in range(2)]
        update(scores)

    @pl.when(ki == pl.num_programs(3) - 1)
    def _():
        lv = lamv_ref[...]
        lam = (jnp.exp(jnp.sum(lv[0:1] * lv[1:2], axis=-1, keepdims=True))
               - jnp.exp(jnp.sum(lv[2:3] * lv[3:4], axis=-1, keepdims=True)) + lam_init)
        if fixed_shift:
            l0 = jnp.sum(l_ref[0], axis=-1, keepdims=True)
            l1 = jnp.sum(l_ref[1], axis=-1, keepdims=True)
        else:
            l0 = jnp.tile(l_ref[0], (1, dv // LANES))
            l1 = jnp.tile(l_ref[1], (1, dv // LANES))
        o = acc_ref[0] / l0 - lam * (acc_ref[1] / l1)
        ms = jnp.mean(o * o, axis=-1, keepdims=True)
        o_ref[...] = (o * lax.rsqrt(ms + EPS) * g_ref[...] * (1.0 - lam_init)).astype(o_ref.dtype)


SHIFT_LANES = (4, 5, 6)


def _split_bf16(x, parts):
    out = []
    for _ in range(parts):
        p = x.astype(BF16).astype(F32)
        out.append(p)
        x = x - p
    return out


def _alibi_templates(tq, tk):
    r = np.arange(tq)
    qt = np.zeros((2, tq, LANES), np.float32)
    qt[0, :, 0] = -(r // BF16_EXACT_INT) * BF16_EXACT_INT
    qt[0, :, 1] = -(r % BF16_EXACT_INT)
    qt[0, :, 2] = 1.0
    qt[0, :, 3] = 1.0
    qt[1, :, 0] = -1.0
    c = np.arange(tk)
    ka = np.zeros((tk, LANES), np.float32)
    ka[:, 0] = 1.0
    ka[:, 1] = 1.0
    ka[:, 2] = c % BF16_EXACT_INT
    ka[:, 3] = (c // BF16_EXACT_INT) * BF16_EXACT_INT
    ka[:, SHIFT_LANES[0]:SHIFT_LANES[-1] + 1] = 1.0
    return jnp.asarray(qt, F32), jnp.asarray(ka, BF16)


def diff_attention(qk2d, v2d, lamv, subln_g, shift, B, S, lam_init, fixed_shift, tile=1024):
    T, W = qk2d.shape
    H = A_HEADS
    dh = W // (4 * H)
    dv = 2 * dh
    tq = tk = min(tile, S)
    nq, nk = S // tq, S // tk
    slopes = 2.0 ** (-8.0 * np.arange(1, H + 1) / H)
    assert np.all(np.log2(slopes) == np.round(np.log2(slopes))) and S <= BF16_EXACT_INT ** 2
    qt, ka = _alibi_templates(tq, tk)
    parts = _split_bf16(shift if fixed_shift else jnp.zeros((), F32), len(SHIFT_LANES))
    lane = lax.broadcasted_iota(jnp.int32, (1, tq, LANES), 2)
    shift_plane = sum(jnp.where(lane == ln, -p, 0.0) for ln, p in zip(SHIFT_LANES, parts))
    qt = jnp.concatenate([qt, shift_plane.astype(F32)], axis=0)
    applied = sum(parts).reshape(1)
    return pl.pallas_call(
        functools.partial(_attn_body, tq=tq, tk=tk, dh=dh, lam_init=lam_init, fixed_shift=fixed_shift),
        grid=(B, H, nq, nk),
        in_specs=[pl.BlockSpec(memory_space=pltpu.SMEM),
                  pl.BlockSpec(memory_space=pltpu.SMEM),
                  pl.BlockSpec((4, dh), lambda b, h, i, j: (0, 0)),
                  pl.BlockSpec((1, dv), lambda b, h, i, j: (0, 0)),
                  pl.BlockSpec((3, tq, LANES), lambda b, h, i, j: (0, 0, 0)),
                  pl.BlockSpec((tk, LANES), lambda b, h, i, j: (0, 0)),
                  pl.BlockSpec((tq, dv), lambda b, h, i, j: (b * nq + i, h)),
                  pl.BlockSpec((tk, dv), lambda b, h, i, j: (b * nk + j, H + h)),
                  pl.BlockSpec((tk, dv), lambda b, h, i, j: (b * nk + j, h))],
        out_specs=pl.BlockSpec((tq, dv), lambda b, h, i, j: (b * nq + i, h)),
        out_shape=jax.ShapeDtypeStruct((T, H * dv), BF16),
        scratch_shapes=[pltpu.VMEM((2, tq, LANES), F32), pltpu.VMEM((2, tq, LANES), F32),
                        pltpu.VMEM((2, tq, dv), F32), pltpu.VMEM((tq, tk), F32)],
        compiler_params=_params("parallel", "parallel", "arbitrary", "arbitrary"),
        name="diff_attention_shifted" if fixed_shift else "diff_attention_online",
    )(jnp.asarray(slopes, F32), applied, lamv, subln_g.reshape(1, dv).astype(F32), qt, ka, qk2d, qk2d, v2d)


def _merge_body(fm_ref, on_ref, gf_ref, ga_ref, wf_ref, wa_ref, o_ref):
    yf = jnp.dot(fm_ref[...].astype(BF16), wf_ref[...], preferred_element_type=F32)
    ya = jnp.dot(on_ref[...], wa_ref[...], preferred_element_type=F32)
    o_ref[...] = (gf_ref[...].astype(F32) * yf + ga_ref[...].astype(F32) * ya).astype(o_ref.dtype)


def gated_merge(fm, on, gates, wf, wa, tm=512):
    T, D = on.shape
    tm = min(tm, T)
    return pl.pallas_call(
        _merge_body,
        grid=(T // tm,),
        in_specs=[pl.BlockSpec((tm, fm.shape[1]), lambda i: (i, 0)),
                  pl.BlockSpec((tm, D), lambda i: (i, 0)),
                  pl.BlockSpec((tm, D), lambda i: (i, 0)),
                  pl.BlockSpec((tm, D), lambda i: (i, 1)),
                  pl.BlockSpec(wf.shape, lambda i: (0, 0), pipeline_mode=pl.Buffered(1)),
                  pl.BlockSpec(wa.shape, lambda i: (0, 0), pipeline_mode=pl.Buffered(1))],
        out_specs=pl.BlockSpec((tm, D), lambda i: (i, 0)),
        out_shape=jax.ShapeDtypeStruct((T, D), BF16),
        compiler_params=_params("parallel"),
        name="gated_merge",
    )(fm, on, gates, gates, wf, wa)


def _out_body(mx_ref, x_ref, wo_ref, g_ref, h_ref, hn_ref, hnt_ref):
    h = x_ref[...] + jnp.dot(mx_ref[...], wo_ref[...], preferred_element_type=F32)
    h_ref[...] = h
    ms = jnp.mean(h * h, axis=-1, keepdims=True)
    hn = h * lax.rsqrt(ms + EPS) * g_ref[...]
    hn_ref[...] = hn.astype(hn_ref.dtype)
    hnt_ref[...] = hn.T.astype(hnt_ref.dtype)


def out_projection(mixed, x2d, wo, g2, tm=512):
    T, D = x2d.shape
    tm = min(tm, T)
    return pl.pallas_call(
        _out_body,
        grid=(T // tm,),
        in_specs=[pl.BlockSpec((tm, D), lambda i: (i, 0)),
                  pl.BlockSpec((tm, D), lambda i: (i, 0)),
                  pl.BlockSpec((D, D), lambda i: (0, 0)),
                  pl.BlockSpec((1, D), lambda i: (0, 0))],
        out_specs=[pl.BlockSpec((tm, D), lambda i: (i, 0)),
                   pl.BlockSpec((tm, D), lambda i: (i, 0)),
                   pl.BlockSpec((D, tm), lambda i: (0, i))],
        out_shape=[jax.ShapeDtypeStruct((T, D), F32), jax.ShapeDtypeStruct((T, D), BF16),
                   jax.ShapeDtypeStruct((D, T), BF16)],
        compiler_params=_params("parallel"),
        name="out_projection",
    )(mixed, x2d, wo, g2.reshape(1, D).astype(F32))


def _extract_top(x, dst_ref, n, want_rank=False):
    rank = jnp.full(x.shape, float(n), F32) if want_rank else None
    for r in range(n):
        mx = jnp.max(x, axis=0, keepdims=True)
        dst_ref[pl.ds(r, 1), :] = mx
        hit = x >= mx
        if want_rank:
            rank = jnp.where(hit, float(r), rank)
        x = jnp.where(hit, NEG_INF, x)
    return rank


def _peer_query_body(hn_ref, wq_ref, keys_ref, n1_ref, e1_ref, r2_ref, e2_ref, sv_ref, cs_ref):
    qp = jnp.dot(hn_ref[...], wq_ref[...], preferred_element_type=F32).astype(BF16)
    half = keys_ref.shape[-1]
    nt = (((1,), (1,)), ((), ()))
    for h in range(P_HEADS):
        lo = 2 * h * half
        st1 = lax.dot_general(keys_ref[h, 0], qp[:, lo:lo + half], nt, preferred_element_type=F32)
        st2 = lax.dot_general(keys_ref[h, 1], qp[:, lo + half:lo + 2 * half], nt,
                              preferred_element_type=F32)
        _extract_top(st1, sv_ref.at[0], P_TOPK)
        r2 = _extract_top(st2, sv_ref.at[1], P_TOPK, want_rank=True)
        sv1 = sv_ref[0]
        sv2 = sv_ref[1]
        parts = [sv1[0:1] + sv2]
        for a in range(1, 8):
            parts.append(sv1[a:a + 1] + sv2[0:8])
        parts.append(sv1[8:16] + sv2[0:1])
        _extract_top(jnp.concatenate(parts, axis=0), cs_ref, P_TOPK)
        cs = cs_ref[...]
        tau = cs[P_TOPK - 1:P_TOPK]
        z = jnp.sum(jnp.exp(cs - cs[0:1]), axis=0, keepdims=True)
        n1 = jnp.zeros(st1.shape, F32)
        for a in range(P_TOPK):
            va = sv1[a:a + 1]
            n_a = jnp.sum(jnp.where(va + sv2 >= tau, 1.0, 0.0), axis=0, keepdims=True)
            n1 = jnp.where(st1 == va, n_a, n1)
        e1 = jnp.exp(st1 - sv1[0:1])
        e2 = jnp.exp(st2 - sv2[0:1]) / z
        for tb in range(st1.shape[1] // LANES):
            cols = slice(tb * LANES, (tb + 1) * LANES)
            n1_ref[h, tb] = n1[:, cols]
            e1_ref[h, tb] = e1[:, cols]
            r2_ref[h, tb] = pltpu.bitcast(r2[:, cols].astype(BF16), jnp.uint32)
            e2_ref[h, tb] = pltpu.bitcast(e2[:, cols].astype(BF16), jnp.uint32)


def peer_query(hn, wq, keys, tt=256):
    T, D = hn.shape
    H, _, nk, half = keys.shape
    tt = min(tt, T)
    spec = pl.BlockSpec((H, tt // LANES, nk, LANES), lambda i: (0, i, 0, 0))
    rows = jax.ShapeDtypeStruct((H, T // LANES, nk, LANES), F32)
    pspec = pl.BlockSpec((H, tt // LANES, nk // 2, LANES), lambda i: (0, i, 0, 0))
    tiles = jax.ShapeDtypeStruct((H, T // LANES, nk // 2, LANES), jnp.uint32)
    return pl.pallas_call(
        _peer_query_body,
        grid=(T // tt,),
        in_specs=[pl.BlockSpec((tt, D), lambda i: (i, 0)),
                  pl.BlockSpec(wq.shape, lambda i: (0, 0)),
                  pl.BlockSpec(keys.shape, lambda i: (0, 0, 0, 0))],
        out_specs=[spec, spec, pspec, pspec],
        out_shape=[rows, rows, tiles, tiles],
        scratch_shapes=[pltpu.VMEM((2, P_TOPK, tt), F32), pltpu.VMEM((P_TOPK, tt), F32)],
        compiler_params=_params("parallel"),
        name="peer_query",
    )(hn, wq, keys)


def _peer_dense_body(hnt_ref, u_ref, v_ref, n1_ref, e1_ref, r2_ref, e2_ref, h_ref, o_ref,
                     sc0_ref, sc1_ref, a0_ref, a1_ref, *, nk, rows_per_chunk, n_chunks):
    e = pl.program_id(1)

    @pl.when(e == 0)
    def _():
        o_ref[...] = h_ref[...]

    def stages(sc_w, sc_r, a_w, a_r, do_up=True, do_gate=True, do_down=True):
        tt = o_ref.shape[0]
        chunk = e - 1

        def anchored(lhs, dep):
            if dep is None:
                return lhs
            z = jnp.tile(dep, (BF16_ROWS // dep.shape[0], lhs.shape[1] // LANES)).astype(BF16)
            return jnp.concatenate([lhs[0:BF16_ROWS, :] + z, lhs[BF16_ROWS:, :]], axis=0)

        def up_piece(half, kb, dep):
            rows = slice(half * (tt // 2), (half + 1) * (tt // 2))
            ks = slice(kb * UP_K, (kb + 1) * UP_K)
            return jnp.dot(anchored(u_ref[:, ks], dep), hnt_ref[ks, rows], preferred_element_type=F32)

        def gate(tb, ii):
            i = chunk * rows_per_chunk + ii
            def key_row(ref, h):
                row = jnp.broadcast_to(ref[h, tb, pl.ds(i, 1), :], (BF16_ROWS, LANES)).astype(BF16)
                return jnp.tile(row, (nk // BF16_ROWS, 1))

            w = jnp.zeros((nk, LANES), BF16)
            for h in range(P_HEADS):
                w = w + jnp.where(pltpu.bitcast(r2_ref[h, tb], BF16) < key_row(n1_ref, h),
                                  pltpu.bitcast(e2_ref[h, tb], BF16) * key_row(e1_ref, h), 0.0)
            x = sc_r[ii * nk:(ii + 1) * nk, tb * LANES:(tb + 1) * LANES]
            act = 0.5 * x * (1.0 + lax.erf(x * (1.0 / math.sqrt(2.0))))
            prod = (act * w.astype(F32)).T
            a_w[tb * LANES:(tb + 1) * LANES, ii * nk:(ii + 1) * nk] = prod.astype(BF16)
            bits = pltpu.bitcast(prod[0:8, :], jnp.uint32)
            bits = lax.shift_right_logical(lax.shift_right_logical(bits, jnp.uint32(16)), jnp.uint32(16))
            return pltpu.bitcast(bits, F32)

        def down(n, sub, dep):
            cols = slice(n * MXU_COLS, (n + 1) * MXU_COLS)
            ex = slice(sub * DOWN_K, (sub + 1) * DOWN_K)
            o_ref[:, cols] += jnp.dot(anchored(a_r[:, ex], dep), v_ref[ex, cols], preferred_element_type=F32)

        blocks = [(tb, ii) for tb in range(tt // LANES) for ii in range(rows_per_chunk)]
        n_sub = u_ref.shape[0] // DOWN_K
        n_down = n_sub * (o_ref.shape[1] // MXU_COLS)
        n_kb = u_ref.shape[1] // UP_K
        per_gate = 2 * n_kb // (len(blocks) - n_down)
        dep = None
        g = 0
        for half in range(2 if do_up else 0):
            acc = None
            for kb in range(n_kb):
                piece = up_piece(half, kb, dep)
                acc = piece if acc is None else acc + piece
                if do_gate and (half * n_kb + kb + 1) % per_gate == 0:
                    dep = gate(*blocks[g])
                    g += 1
            sc_w[:, half * (tt // 2):(half + 1) * (tt // 2)] = acc
        if do_gate and not do_up:
            for blk in blocks[:len(blocks) - n_down]:
                dep = gate(*blk)
            g = len(blocks) - n_down
        for n in range(n_down):
            if do_down:
                down(n // n_sub, n % n_sub, dep)
            if do_gate:
                dep = gate(*blocks[g + n])

    steady = jnp.logical_and(e >= 2, e < n_chunks)

    @pl.when(e == 0)
    def _():
        stages(sc0_ref, None, None, None, do_gate=False, do_down=False)

    @pl.when(e == 1)
    def _():
        stages(sc1_ref, sc0_ref, a0_ref, None, do_down=False)

    @pl.when(jnp.logical_and(steady, e % 2 == 0))
    def _():
        stages(sc0_ref, sc1_ref, a1_ref, a0_ref)

    @pl.when(jnp.logical_and(steady, e % 2 == 1))
    def _():
        stages(sc1_ref, sc0_ref, a0_ref, a1_ref)

    @pl.when(e == n_chunks)
    def _():
        stages(None, sc1_ref, a1_ref, a0_ref, do_up=False)

    @pl.when(e == n_chunks + 1)
    def _():
        stages(None, None, None, a1_ref, do_up=False, do_gate=False)


def peer_dense(hnt, u, v, n1, e1, r2, e2, h1, tt=512, ec=1024):
    D, T = hnt.shape
    E = u.shape[0]
    H, _, nk, _ = n1.shape
    tt = min(tt, T)
    n_chunks = E // ec
    assert n_chunks % 2 == 0 and n_chunks >= 2
    tok = pl.BlockSpec((H, tt // LANES, nk, LANES), lambda t, e: (0, t, 0, 0))
    ptok = pl.BlockSpec((H, tt // LANES, nk // 2, LANES), lambda t, e: (0, t, 0, 0))
    return pl.pallas_call(
        functools.partial(_peer_dense_body, nk=nk, rows_per_chunk=ec // nk, n_chunks=n_chunks),
        grid=(T // tt, n_chunks + 2),
        in_specs=[pl.BlockSpec((D, tt), lambda t, e: (0, t)),
                  pl.BlockSpec((ec, D), lambda t, e: (jnp.minimum(e, n_chunks - 1), 0)),
                  pl.BlockSpec((ec, D), lambda t, e: (jnp.clip(e - 2, 0, n_chunks - 1), 0)),
                  tok, tok, ptok, ptok,
                  pl.BlockSpec((tt, D), lambda t, e: (t, 0), pipeline_mode=pl.Buffered(1))],
        out_specs=pl.BlockSpec((tt, D), lambda t, e: (t, 0)),
        out_shape=jax.ShapeDtypeStruct((T, D), F32),
        scratch_shapes=[pltpu.VMEM((ec, tt), F32), pltpu.VMEM((ec, tt), F32),
                        pltpu.VMEM((tt, ec), BF16), pltpu.VMEM((tt, ec), BF16)],
        compiler_params=_params("parallel", "arbitrary"),
        name="peer_dense",
    )(hnt, u, v, n1, e1, r2, e2, h1)


def kernel(x, norm1_g, w_in, w_fourier, w_attn, q_norm_g, k_norm_g, lambda_q1, lambda_k1,
           lambda_q2, lambda_k2, subln_g, w_out, norm2_g, w_query, sub_keys, expert_u, expert_v):
    B, S, D = x.shape
    T = B * S
    depth = w_in.shape[0]
    dh = D // (2 * A_HEADS)
    f_width = w_fourier.shape[1]
    qk_width = A_HEADS * 2 * dh
    v_width = w_attn.shape[1]
    o_q = f_width
    o_k = o_q + qk_width
    o_v = o_k + qk_width
    o_g = o_v + v_width
    h = x.reshape(T, D)
    for i in range(depth):
        lam_init = 0.8 - 0.6 * math.exp(-0.3 * i)
        xn = rmsnorm_rows(h, norm1_g[i])
        w_in_i = w_in[i]
        qk_gain = jnp.concatenate([jnp.tile(q_norm_g[i].astype(F32) * dh ** -0.5, 2 * A_HEADS),
                                   jnp.tile(k_norm_g[i].astype(F32), 2 * A_HEADS)]).reshape(1, 2 * qk_width)
        z = project(xn, w_in_i, 0, f_width, "cast", out_dtype=F32)
        qk = project(xn, w_in_i, o_q, 2 * qk_width, "headnorm", gain=qk_gain, group=dh)
        vv = project(xn, w_in_i, o_v, v_width, "cast")
        gates = project(xn, w_in_i, o_g, 2 * D, "sigmoid")

        fm = fourier_mix(z, B, S)
        lamv = jnp.stack([lambda_q1[i], lambda_k1[i], lambda_q2[i], lambda_k2[i]]).astype(F32)
        bound = (jnp.max(jnp.abs(q_norm_g[i])) * jnp.max(jnp.abs(k_norm_g[i]))).astype(F32) * (1.01 * dh ** 0.5)
        attend = functools.partial(diff_attention, qk, vv, lamv, subln_g[i], bound, B, S, lam_init)
        on = lax.cond(bound <= MAX_SAFE_SHIFT, lambda: attend(True), lambda: attend(False))

        mixed = gated_merge(fm, on, gates, w_fourier[i].astype(BF16), w_attn[i].astype(BF16))
        h1, hn, hnt = out_projection(mixed, h, w_out[i].astype(BF16), norm2_g[i])

        n1, e1, r2, e2 = peer_query(hn, w_query[i].astype(BF16), sub_keys[i].astype(BF16))
        h = peer_dense(hnt, expert_u[i].astype(BF16), expert_v[i].astype(BF16), n1, e1, r2, e2, h1)
    return h.reshape(B, S, D)
```

```python
import functools
import math

import numpy as np
import jax
import jax.numpy as jnp
from jax import lax
from jax.experimental import pallas as pl
from jax.experimental.pallas import tpu as pltpu

EPS = 1e-6
F_GROUPS = 4
A_HEADS = 8
P_HEADS = 8
P_TOPK = 16
LANES = 128
SUBLANES = 8
MXU_COLS = 256
BF16_ROWS = 16
BF16_EXACT_INT = 256
UP_K = 256
DOWN_K = 512
MAX_SAFE_SHIFT = 40.0
FFT_INNER = 128
VMEM_LIMIT_BYTES = 56 * 1024 * 1024

F32 = jnp.float32
BF16 = jnp.bfloat16
NEG_INF = float("-inf")


def _params(*semantics):
    return pltpu.CompilerParams(dimension_semantics=semantics, vmem_limit_bytes=VMEM_LIMIT_BYTES)


def _rmsnorm_body(x_ref, g_ref, o_ref):
    x = x_ref[...]
    ms = jnp.mean(x * x, axis=-1, keepdims=True)
    o_ref[...] = (x * lax.rsqrt(ms + EPS) * g_ref[...]).astype(o_ref.dtype)


def rmsnorm_rows(x2d, g, tm=512):
    T, D = x2d.shape
    return pl.pallas_call(
        _rmsnorm_body,
        grid=(T // tm,),
        in_specs=[pl.BlockSpec((tm, D), lambda i: (i, 0)),
                  pl.BlockSpec((1, D), lambda i: (0, 0))],
        out_specs=pl.BlockSpec((tm, D), lambda i: (i, 0)),
        out_shape=jax.ShapeDtypeStruct((T, D), BF16),
        compiler_params=_params("parallel"),
        name="rmsnorm",
    )(x2d, g.reshape(1, D).astype(F32))


def _proj_body(x_ref, w_ref, g_ref, o_ref, wb_ref, *, mode, group):
    @pl.when(pl.program_id(1) == 0)
    def _():
        wb_ref[...] = w_ref[...].astype(BF16)

    acc = jnp.dot(x_ref[...], wb_ref[...], preferred_element_type=F32)
    if mode == "cast":
        o_ref[...] = acc.astype(o_ref.dtype)
    elif mode == "sigmoid":
        o_ref[...] = jax.nn.sigmoid(acc).astype(o_ref.dtype)
    else:
        for s in range(acc.shape[1] // group):
            blk = acc[:, s * group:(s + 1) * group]
            ms = jnp.mean(blk * blk, axis=-1, keepdims=True)
            o_ref[:, s * group:(s + 1) * group] = (
                blk * lax.rsqrt(ms + EPS) * g_ref[:, s * group:(s + 1) * group]).astype(o_ref.dtype)


def project(x, w, col0, ncols, mode, gain=None, group=LANES, out_dtype=BF16, tm=1024, tn=1024):
    T, K = x.shape
    tm = min(tm, T)
    if gain is None:
        gain = jnp.ones((1, ncols), F32)
    off = col0 // tn
    return pl.pallas_call(
        functools.partial(_proj_body, mode=mode, group=group),
        grid=(ncols // tn, T // tm),
        in_specs=[pl.BlockSpec((tm, K), lambda j, i: (i, 0)),
                  pl.BlockSpec((K, tn), lambda j, i: (0, off + j)),
                  pl.BlockSpec((1, tn), lambda j, i: (0, j))],
        out_specs=pl.BlockSpec((tm, tn), lambda j, i: (i, j)),
        out_shape=jax.ShapeDtypeStruct((T, ncols), out_dtype),
        scratch_shapes=[pltpu.VMEM((K, tn), BF16)],
        compiler_params=_params("parallel", "arbitrary"),
        name="proj_" + mode,
    )(x, w, gain)


def _fourier_constants(S, C):
    Na = S // FFT_INNER
    eye = np.eye(SUBLANES)
    n2 = np.arange(C, dtype=np.int64)
    ang = 2.0 * np.pi * ((n2[:, None] * n2[None, :]) % C) / C
    w0 = np.concatenate([np.cos(ang), -np.sin(ang)], axis=1)
    a = np.arange(Na, dtype=np.int64)
    phi = 2.0 * np.pi * ((a[:, None] * a[None, :]) % Na) / Na
    r = np.block([[np.cos(phi), np.sin(phi)], [-np.sin(phi), np.cos(phi)]])
    r4 = r.reshape(2, Na, 2, Na)
    rk = np.einsum('qcpa,xy->qxcpay', r4, eye).reshape(2 * SUBLANES * Na, 2 * Na * SUBLANES)
    d = np.arange(FFT_INNER, dtype=np.int64)
    psi = 2.0 * np.pi * ((d[:, None] * d[None, :]) % FFT_INNER) / FFT_INNER
    f2 = np.stack([np.cos(psi), np.sin(psi)], axis=1) / math.sqrt(S * C)
    kf = np.einsum('drb,xy->dxrby', f2, eye).reshape(FFT_INNER * SUBLANES, 2 * FFT_INNER * SUBLANES)
    return (jnp.asarray(w0, BF16), jnp.asarray(rk, BF16), jnp.asarray(kf, BF16))


def _twiddles(S):
    Na = S // FFT_INNER
    b = lax.broadcasted_iota(jnp.int32, (FFT_INNER, Na), 0)
    c = lax.broadcasted_iota(jnp.int32, (FFT_INNER, Na), 1)
    th = ((b * c) % S).astype(F32) * (2.0 * math.pi / S)
    shape = (FFT_INNER, Na, LANES)
    return (jnp.broadcast_to(jnp.cos(th)[:, :, None], shape), jnp.broadcast_to(jnp.sin(th)[:, :, None], shape))


def _fourier_outer_body(z_ref, w0_ref, rk_ref, o_ref, *, groups):
    na, sub, w = z_ref.shape
    c = w // groups
    x = z_ref[...].reshape(na * sub, w).astype(BF16)
    ps, qs = [], []
    for g in range(groups):
        pq = jnp.dot(x[:, g * c:(g + 1) * c], w0_ref[...], preferred_element_type=F32)
        ps.append(pq[:, :c])
        qs.append(pq[:, c:])
    x1 = jnp.concatenate([jnp.concatenate(ps, axis=1), jnp.concatenate(qs, axis=1)], axis=0)
    a = jnp.dot(rk_ref[...], x1.astype(BF16), preferred_element_type=F32)
    o_ref[...] = a.reshape(2, sub, na, w)


def _fourier_inner_body(a_ref, tc_ref, ts_ref, kf_ref, o_ref):
    _, nb, sub, c = a_ref.shape
    ar = a_ref[0].reshape(nb * sub, c)
    ai = a_ref[1].reshape(nb * sub, c)
    tc = jnp.tile(tc_ref[...].reshape(nb * sub, LANES), (1, c // LANES))
    ts = jnp.tile(ts_ref[...].reshape(nb * sub, LANES), (1, c // LANES))
    x3 = jnp.concatenate([ar * tc + ai * ts, ai * tc - ar * ts], axis=0).astype(BF16)
    y = jnp.dot(kf_ref[...], x3, preferred_element_type=F32)
    o_ref[...] = y.reshape(nb, sub, c)


def fourier_mix(z2d, B, S):
    T, W = z2d.shape
    G = F_GROUPS
    C = W // G
    Na = S // FFT_INNER
    assert Na % SUBLANES == 0
    w0, rk, kf = _fourier_constants(S, C)
    tc, ts = _twiddles(S)
    a5 = pl.pallas_call(
        functools.partial(_fourier_outer_body, groups=G),
        grid=(B, FFT_INNER // SUBLANES),
        in_specs=[pl.BlockSpec((None, Na, SUBLANES, W), lambda b, j: (b, 0, j, 0)),
                  pl.BlockSpec(w0.shape, lambda b, j: (0, 0)),
                  pl.BlockSpec(rk.shape, lambda b, j: (0, 0))],
        out_specs=pl.BlockSpec((None, 2, SUBLANES, Na, W), lambda b, j: (b, 0, j, 0, 0)),
        out_shape=jax.ShapeDtypeStruct((B, 2, FFT_INNER, Na, W), F32),
        compiler_params=_params("parallel", "arbitrary"),
        name="fourier_outer",
    )(z2d.reshape(B, Na, FFT_INNER, W), w0, rk)
    y = pl.pallas_call(
        _fourier_inner_body,
        grid=(B, Na // SUBLANES, G),
        in_specs=[pl.BlockSpec((None, 2, FFT_INNER, SUBLANES, C), lambda b, j, g: (b, 0, 0, j, g)),
                  pl.BlockSpec((FFT_INNER, SUBLANES, LANES), lambda b, j, g: (0, j, 0)),
                  pl.BlockSpec((FFT_INNER, SUBLANES, LANES), lambda b, j, g: (0, j, 0)),
                  pl.BlockSpec(kf.shape, lambda b, j, g: (0, 0))],
        out_specs=pl.BlockSpec((None, FFT_INNER, SUBLANES, C), lambda b, j, g: (b, 0, j, g)),
        out_shape=jax.ShapeDtypeStruct((B, FFT_INNER, Na, W), F32),
        compiler_params=_params("parallel", "parallel", "arbitrary"),
        name="fourier_inner",
    )(a5, tc, ts, kf)
    return y.reshape(T, W)


def _attn_body(slope_ref, shift_ref, lamv_ref, g_ref, qt_ref, ka_ref, q_ref, k_ref, v_ref, o_ref,
               m_ref, l_ref, acc_ref, bias_ref, *, tq, tk, dh, lam_init, fixed_shift):
    h = pl.program_id(1)
    qi = pl.program_id(2)
    ki = pl.program_id(3)
    dv = 2 * dh

    @pl.when(ki == 0)
    def _():
        m_ref[...] = jnp.full(m_ref.shape, NEG_INF, F32)
        l_ref[...] = jnp.zeros(l_ref.shape, F32)
        acc_ref[...] = jnp.zeros(acc_ref.shape, F32)

    slope = slope_ref[h]

    def online_softmax_update(scores):
        v = v_ref[...]
        for c in range(2):
            s = scores[c]
            m_prev = m_ref[c]
            m_new = jnp.maximum(m_prev, jnp.max(s, axis=-1, keepdims=True))
            alpha = jnp.exp(m_prev - m_new)
            p = jnp.exp(s - jnp.tile(m_new, (1, tk // LANES)))
            l_ref[c] = alpha * l_ref[c] + jnp.sum(p, axis=-1, keepdims=True)
            acc_ref[c] = (jnp.tile(alpha, (1, dv // LANES)) * acc_ref[c]
                          + jnp.dot(p.astype(BF16), v, preferred_element_type=F32))
            m_ref[c] = m_new

    def shifted_update(scores):
        v = v_ref[...]
        for c in range(2):
            p = jnp.exp(scores[c])
            part = p[:, 0:LANES]
            for t in range(1, tk // LANES):
                part = part + p[:, t * LANES:(t + 1) * LANES]
            l_ref[c] += part
            acc_ref[c] += jnp.dot(p.astype(BF16), v, preferred_element_type=F32)

    update = shifted_update if fixed_shift else online_softmax_update
    nt = (((1,), (1,)), ((), ()))

    @pl.when(qi != ki)
    def _():
        off = (qi - ki) * tq
        sgn = jnp.where(off > 0, slope, -slope)
        qa = ((qt_ref[0] + off.astype(F32) * qt_ref[1]) * sgn + qt_ref[2]).astype(BF16)
        ka = ka_ref[...]
        q = q_ref[...]
        k = k_ref[...]
        scores = []
        for c in range(2):
            qc = jnp.concatenate([q[:, c * dh:(c + 1) * dh], qa], axis=1)
            kc = jnp.concatenate([k[:, c * dh:(c + 1) * dh], ka], axis=1)
            scores.append(lax.dot_general(qc, kc, nt, preferred_element_type=F32))
        update(scores)

    @pl.when(qi == ki)
    def _():
        @pl.when(qi == 0)
        def _():
            row = lax.broadcasted_iota(jnp.int32, (tq, tk), 0)
            col = lax.broadcasted_iota(jnp.int32, (tq, tk), 1)
            bias_ref[...] = jnp.abs(row - col).astype(F32) * (-slope) - shift_ref[0]

        bias = bias_ref[...]
        q = q_ref[...]
        k = k_ref[...]
        scores = [lax.dot_general(q[:, c * dh:(c + 1) * dh], k[:, c * dh:(c + 1) * dh], nt,
                                  preferred_element_type=F32) + bias for c in range(2)]
        update(scores)

    @pl.when(ki == pl.num_programs(3) - 1)
    def _():
        lv = lamv_ref[...]
        lam = (jnp.exp(jnp.sum(lv[0:1] * lv[1:2], axis=-1, keepdims=True))
               - jnp.exp(jnp.sum(lv[2:3] * lv[3:4], axis=-1, keepdims=True)) + lam_init)
        if fixed_shift:
            l0 = jnp.sum(l_ref[0], axis=-1, keepdims=True)
            l1 = jnp.sum(l_ref[1], axis=-1, keepdims=True)
        else:
            l0 = jnp.tile(l_ref[0], (1, dv // LANES))
            l1 = jnp.tile(l_ref[1], (1, dv // LANES))
        o = acc_ref[0] / l0 - lam * (acc_ref[1] / l1)
        ms = jnp.mean(o * o, axis=-1, keepdims=True)
        o_ref[...] = (o * lax.rsqrt(ms + EPS) * g_ref[...] * (1.0 - lam_init)).astype(o_ref.dtype)


SHIFT_LANES = (4, 5, 6)


def _split_bf16(x, parts):
    out = []
    for _ in range(parts):
        p = x.astype(BF16).astype(F32)
        out.append(p)
        x = x - p
    return out


def _alibi_templates(tq, tk):
    r = np.arange(tq)
    qt = np.zeros((2, tq, LANES), np.float32)
    qt[0, :, 0] = -(r // BF16_EXACT_INT) * BF16_EXACT_INT
    qt[0, :, 1] = -(r % BF16_EXACT_INT)
    qt[0, :, 2] = 1.0
    qt[0, :, 3] = 1.0
    qt[1, :, 0] = -1.0
    c = np.arange(tk)
    ka = np.zeros((tk, LANES), np.float32)
    ka[:, 0] = 1.0
    ka[:, 1] = 1.0
    ka[:, 2] = c % BF16_EXACT_INT
    ka[:, 3] = (c // BF16_EXACT_INT) * BF16_EXACT_INT
    ka[:, SHIFT_LANES[0]:SHIFT_LANES[-1] + 1] = 1.0
    return jnp.asarray(qt, F32), jnp.asarray(ka, BF16)


def diff_attention(qk2d, v2d, lamv, subln_g, shift, B, S, lam_init, fixed_shift, tile=1024):
    T, W = qk2d.shape
    H = A_HEADS
    dh = W // (4 * H)
    dv = 2 * dh
    tq = tk = min(tile, S)
    nq, nk = S // tq, S // tk
    slopes = 2.0 ** (-8.0 * np.arange(1, H + 1) / H)
    assert np.all(np.log2(slopes) == np.round(np.log2(slopes))) and S <= BF16_EXACT_INT ** 2
    qt, ka = _alibi_templates(tq, tk)
    parts = _split_bf16(shift if fixed_shift else jnp.zeros((), F32), len(SHIFT_LANES))
    lane = lax.broadcasted_iota(jnp.int32, (1, tq, LANES), 2)
    shift_plane = sum(jnp.where(lane == ln, -p, 0.0) for ln, p in zip(SHIFT_LANES, parts))
    qt = jnp.concatenate([qt, shift_plane.astype(F32)], axis=0)
    applied = sum(parts).reshape(1)
    return pl.pallas_call(
        functools.partial(_attn_body, tq=tq, tk=tk, dh=dh, lam_init=lam_init, fixed_shift=fixed_shift),
        grid=(B, H, nq, nk),
        in_specs=[pl.BlockSpec(memory_space=pltpu.SMEM),
                  pl.BlockSpec(memory_space=pltpu.SMEM),
                  pl.BlockSpec((4, dh), lambda b, h, i, j: (0, 0)),
                  pl.BlockSpec((1, dv), lambda b, h, i, j: (0, 0)),
                  pl.BlockSpec((3, tq, LANES), lambda b, h, i, j: (0, 0, 0)),
                  pl.BlockSpec((tk, LANES), lambda b, h, i, j: (0, 0)),
                  pl.BlockSpec((tq, dv), lambda b, h, i, j: (b * nq + i, h)),
                  pl.BlockSpec((tk, dv), lambda b, h, i, j: (b * nk + j, H + h)),
                  pl.BlockSpec((tk, dv), lambda b, h, i, j: (b * nk + j, h))],
        out_specs=pl.BlockSpec((tq, dv), lambda b, h, i, j: (b * nq + i, h)),
        out_shape=jax.ShapeDtypeStruct((T, H * dv), BF16),
        scratch_shapes=[pltpu.VMEM((2, tq, LANES), F32), pltpu.VMEM((2, tq, LANES), F32),
                        pltpu.VMEM((2, tq, dv), F32), pltpu.VMEM((tq, tk), F32)],
        compiler_params=_params("parallel", "parallel", "arbitrary", "arbitrary"),
        name="diff_attention_shifted" if fixed_shift else "diff_attention_online",
    )(jnp.asarray(slopes, F32), applied, lamv, subln_g.reshape(1, dv).astype(F32), qt, ka, qk2d, qk2d, v2d)


def _merge_body(fm_ref, on_ref, gf_ref, ga_ref, wf_ref, wa_ref, o_ref):
    yf = jnp.dot(fm_ref[...].astype(BF16), wf_ref[...], preferred_element_type=F32)
    ya = jnp.dot(on_ref[...], wa_ref[...], preferred_element_type=F32)
    o_ref[...] = (gf_ref[...].astype(F32) * yf + ga_ref[...].astype(F32) * ya).astype(o_ref.dtype)


def gated_merge(fm, on, gates, wf, wa, tm=512):
    T, D = on.shape
    tm = min(tm, T)
    return pl.pallas_call(
        _merge_body,
        grid=(T // tm,),
        in_specs=[pl.BlockSpec((tm, fm.shape[1]), lambda i: (i, 0)),
                  pl.BlockSpec((tm, D), lambda i: (i, 0)),
                  pl.BlockSpec((tm, D), lambda i: (i, 0)),
                  pl.BlockSpec((tm, D), lambda i: (i, 1)),
                  pl.BlockSpec(wf.shape, lambda i: (0, 0), pipeline_mode=pl.Buffered(1)),
                  pl.BlockSpec(wa.shape, lambda i: (0, 0), pipeline_mode=pl.Buffered(1))],
        out_specs=pl.BlockSpec((tm, D), lambda i: (i, 0)),
        out_shape=jax.ShapeDtypeStruct((T, D), BF16),
        compiler_params=_params("parallel"),
        name="gated_merge",
    )(fm, on, gates, gates, wf, wa)


def _out_body(mx_ref, x_ref, wo_ref, g_ref, h_ref, hn_ref, hnt_ref):
    h = x_ref[...] + jnp.dot(mx_ref[...], wo_ref[...], preferred_element_type=F32)
    h_ref[...] = h
    ms = jnp.mean(h * h, axis=-1, keepdims=True)
    hn = h * lax.rsqrt(ms + EPS) * g_ref[...]
    hn_ref[...] = hn.astype(hn_ref.dtype)
    hnt_ref[...] = hn.T.astype(hnt_ref.dtype)


def out_projection(mixed, x2d, wo, g2, tm=512):
    T, D = x2d.shape
    tm = min(tm, T)
    return pl.pallas_call(
        _out_body,
        grid=(T // tm,),
        in_specs=[pl.BlockSpec((tm, D), lambda i: (i, 0)),
                  pl.BlockSpec((tm, D), lambda i: (i, 0)),
                  pl.BlockSpec((D, D), lambda i: (0, 0)),
                  pl.BlockSpec((1, D), lambda i: (0, 0))],
        out_specs=[pl.BlockSpec((tm, D), lambda i: (i, 0)),
                   pl.BlockSpec((tm, D), lambda i: (i, 0)),
                   pl.BlockSpec((D, tm), lambda i: (0, i))],
        out_shape=[jax.ShapeDtypeStruct((T, D), F32), jax.ShapeDtypeStruct((T, D), BF16),
                   jax.ShapeDtypeStruct((D, T), BF16)],
        compiler_params=_params("parallel"),
        name="out_projection",
    )(mixed, x2d, wo, g2.reshape(1, D).astype(F32))


def _extract_top(x, dst_ref, n, want_rank=False):
    rank = jnp.full(x.shape, float(n), F32) if want_rank else None
    for r in range(n):
        mx = jnp.max(x, axis=0, keepdims=True)
        dst_ref[pl.ds(r, 1), :] = mx
        hit = x >= mx
        if want_rank:
            rank = jnp.where(hit, float(r), rank)
        x = jnp.where(hit, NEG_INF, x)
    return rank


def _peer_query_body(hn_ref, wq_ref, keys_ref, u_ref, v_ref, n1_ref, e1_ref, r2_ref, e2_ref, ub_ref, vb_ref,
                     sv_ref, cs_ref):
    ub_ref[...] = u_ref[...].astype(BF16)
    vb_ref[...] = v_ref[...].astype(BF16)
    qp = jnp.dot(hn_ref[...], wq_ref[...], preferred_element_type=F32).astype(BF16)
    half = keys_ref.shape[-1]
    nt = (((1,), (1,)), ((), ()))
    for h in range(P_HEADS):
        lo = 2 * h * half
        st1 = lax.dot_general(keys_ref[h, 0], qp[:, lo:lo + half], nt, preferred_element_type=F32)
        st2 = lax.dot_general(keys_ref[h, 1], qp[:, lo + half:lo + 2 * half], nt,
                              preferred_element_type=F32)
        _extract_top(st1, sv_ref.at[0], P_TOPK)
        r2 = _extract_top(st2, sv_ref.at[1], P_TOPK, want_rank=True)
        sv1 = sv_ref[0]
        sv2 = sv_ref[1]
        parts = [sv1[0:1] + sv2]
        for a in range(1, 8):
            parts.append(sv1[a:a + 1] + sv2[0:8])
        parts.append(sv1[8:16] + sv2[0:1])
        _extract_top(jnp.concatenate(parts, axis=0), cs_ref, P_TOPK)
        cs = cs_ref[...]
        tau = cs[P_TOPK - 1:P_TOPK]
        z = jnp.sum(jnp.exp(cs - cs[0:1]), axis=0, keepdims=True)
        n1 = jnp.zeros(st1.shape, F32)
        for a in range(P_TOPK):
            va = sv1[a:a + 1]
            n_a = jnp.sum(jnp.where(va + sv2 >= tau, 1.0, 0.0), axis=0, keepdims=True)
            n1 = jnp.where(st1 == va, n_a, n1)
        e1 = jnp.exp(st1 - sv1[0:1])
        e2 = jnp.exp(st2 - sv2[0:1]) / z
        for tb in range(st1.shape[1] // LANES):
            cols = slice(tb * LANES, (tb + 1) * LANES)
            n1_ref[h, tb] = n1[:, cols]
            e1_ref[h, tb] = e1[:, cols]
            r2_ref[h, tb] = pltpu.bitcast(r2[:, cols].astype(BF16), jnp.uint32)
            e2_ref[h, tb] = pltpu.bitcast(e2[:, cols].astype(BF16), jnp.uint32)


def peer_query(hn, wq, keys, u, v, tt=256):
    T, D = hn.shape
    E = u.shape[0]
    er = E // (T // tt)
    tab = pl.BlockSpec((er, D), lambda i: (i, 0))
    H, _, nk, half = keys.shape
    tt = min(tt, T)
    spec = pl.BlockSpec((H, tt // LANES, nk, LANES), lambda i: (0, i, 0, 0))
    rows = jax.ShapeDtypeStruct((H, T // LANES, nk, LANES), F32)
    pspec = pl.BlockSpec((H, tt // LANES, nk // 2, LANES), lambda i: (0, i, 0, 0))
    tiles = jax.ShapeDtypeStruct((H, T // LANES, nk // 2, LANES), jnp.uint32)
    return pl.pallas_call(
        _peer_query_body,
        grid=(T // tt,),
        in_specs=[pl.BlockSpec((tt, D), lambda i: (i, 0)),
                  pl.BlockSpec(wq.shape, lambda i: (0, 0)),
                  pl.BlockSpec(keys.shape, lambda i: (0, 0, 0, 0)), tab, tab],
        out_specs=[spec, spec, pspec, pspec, tab, tab],
        out_shape=[rows, rows, tiles, tiles,
                   jax.ShapeDtypeStruct((E, D), BF16), jax.ShapeDtypeStruct((E, D), BF16)],
        scratch_shapes=[pltpu.VMEM((2, P_TOPK, tt), F32), pltpu.VMEM((P_TOPK, tt), F32)],
        compiler_params=_params("parallel"),
        name="peer_query",
    )(hn, wq, keys, u, v)


def _peer_dense_body(hnt_ref, u_ref, v_ref, n1_ref, e1_ref, r2_ref, e2_ref, h_ref, o_ref,
                     sc0_ref, sc1_ref, a0_ref, a1_ref, *, nk, rows_per_chunk, n_chunks):
    e = pl.program_id(1)

    @pl.when(e == 0)
    def _():
        o_ref[...] = h_ref[...]

    def stages(sc_w, sc_r, a_w, a_r, do_up=True, do_gate=True, do_down=True):
        tt = o_ref.shape[0]
        chunk = e - 1

        def anchored(lhs, dep):
            if dep is None:
                return lhs
            z = jnp.tile(dep, (BF16_ROWS // dep.shape[0], lhs.shape[1] // LANES)).astype(BF16)
            return jnp.concatenate([lhs[0:BF16_ROWS, :] + z, lhs[BF16_ROWS:, :]], axis=0)

        def up_piece(half, kb, dep):
            rows = slice(half * (tt // 2), (half + 1) * (tt // 2))
            ks = slice(kb * UP_K, (kb + 1) * UP_K)
            return jnp.dot(anchored(u_ref[:, ks], dep), hnt_ref[ks, rows], preferred_element_type=F32)

        def gate(tb, ii):
            i = chunk * rows_per_chunk + ii
            def key_row(ref, h):
                row = jnp.broadcast_to(ref[h, tb, pl.ds(i, 1), :], (BF16_ROWS, LANES)).astype(BF16)
                return jnp.tile(row, (nk // BF16_ROWS, 1))

            w = jnp.zeros((nk, LANES), BF16)
            for h in range(P_HEADS):
                w = w + jnp.where(pltpu.bitcast(r2_ref[h, tb], BF16) < key_row(n1_ref, h),
                                  pltpu.bitcast(e2_ref[h, tb], BF16) * key_row(e1_ref, h), 0.0)
            x = sc_r[ii * nk:(ii + 1) * nk, tb * LANES:(tb + 1) * LANES]
            act = 0.5 * x * (1.0 + lax.erf(x * (1.0 / math.sqrt(2.0))))
            prod = (act * w.astype(F32)).T
            a_w[tb * LANES:(tb + 1) * LANES, ii * nk:(ii + 1) * nk] = prod.astype(BF16)
            bits = pltpu.bitcast(prod[0:8, :], jnp.uint32)
            bits = lax.shift_right_logical(lax.shift_right_logical(bits, jnp.uint32(16)), jnp.uint32(16))
            return pltpu.bitcast(bits, F32)

        def down(n, sub, dep):
            cols = slice(n * MXU_COLS, (n + 1) * MXU_COLS)
            ex = slice(sub * DOWN_K, (sub + 1) * DOWN_K)
            o_ref[:, cols] += jnp.dot(anchored(a_r[:, ex], dep), v_ref[ex, cols], preferred_element_type=F32)

        blocks = [(tb, ii) for tb in range(tt // LANES) for ii in range(rows_per_chunk)]
        n_sub = u_ref.shape[0] // DOWN_K
        n_down = n_sub * (o_ref.shape[1] // MXU_COLS)
        n_kb = u_ref.shape[1] // UP_K
        per_gate = 2 * n_kb // (len(blocks) - n_down)
        dep = None
        g = 0
        for half in range(2 if do_up else 0):
            acc = None
            for kb in range(n_kb):
                piece = up_piece(half, kb, dep)
                acc = piece if acc is None else acc + piece
                if do_gate and (half * n_kb + kb + 1) % per_gate == 0:
                    dep = gate(*blocks[g])
                    g += 1
            sc_w[:, half * (tt // 2):(half + 1) * (tt // 2)] = acc
        if do_gate and not do_up:
            for blk in blocks[:len(blocks) - n_down]:
                dep = gate(*blk)
            g = len(blocks) - n_down
        for n in range(n_down):
            if do_down:
                down(n // n_sub, n % n_sub, dep)
            if do_gate:
                dep = gate(*blocks[g + n])

    steady = jnp.logical_and(e >= 2, e < n_chunks)

    @pl.when(e == 0)
    def _():
        stages(sc0_ref, None, None, None, do_gate=False, do_down=False)

    @pl.when(e == 1)
    def _():
        stages(sc1_ref, sc0_ref, a0_ref, None, do_down=False)

    @pl.when(jnp.logical_and(steady, e % 2 == 0))
    def _():
        stages(sc0_ref, sc1_ref, a1_ref, a0_ref)

    @pl.when(jnp.logical_and(steady, e % 2 == 1))
    def _():
        stages(sc1_ref, sc0_ref, a0_ref, a1_ref)

    @pl.when(e == n_chunks)
    def _():
        stages(None, sc1_ref, a1_ref, a0_ref, do_up=False)

    @pl.when(e == n_chunks + 1)
    def _():
        stages(None, None, None, a1_ref, do_up=False, do_gate=False)


def peer_dense(hnt, u, v, n1, e1, r2, e2, h1, tt=512, ec=1024):
    D, T = hnt.shape
    E = u.shape[0]
    H, _, nk, _ = n1.shape
    tt = min(tt, T)
    n_chunks = E // ec
    assert n_chunks % 2 == 0 and n_chunks >= 2
    tok = pl.BlockSpec((H, tt // LANES, nk, LANES), lambda t, e: (0, t, 0, 0))
    ptok = pl.BlockSpec((H, tt // LANES, nk // 2, LANES), lambda t, e: (0, t, 0, 0))
    return pl.pallas_call(
        functools.partial(_peer_dense_body, nk=nk, rows_per_chunk=ec // nk, n_chunks=n_chunks),
        grid=(T // tt, n_chunks + 2),
        in_specs=[pl.BlockSpec((D, tt), lambda t, e: (0, t)),
                  pl.BlockSpec((ec, D), lambda t, e: (jnp.minimum(e, n_chunks - 1), 0)),
                  pl.BlockSpec((ec, D), lambda t, e: (jnp.clip(e - 2, 0, n_chunks - 1), 0)),
                  tok, tok, ptok, ptok,
                  pl.BlockSpec((tt, D), lambda t, e: (t, 0), pipeline_mode=pl.Buffered(1))],
        out_specs=pl.BlockSpec((tt, D), lambda t, e: (t, 0)),
        out_shape=jax.ShapeDtypeStruct((T, D), F32),
        scratch_shapes=[pltpu.VMEM((ec, tt), F32), pltpu.VMEM((ec, tt), F32),
                        pltpu.VMEM((tt, ec), BF16), pltpu.VMEM((tt, ec), BF16)],
        compiler_params=_params("parallel", "arbitrary"),
        name="peer_dense",
    )(hnt, u, v, n1, e1, r2, e2, h1)


def kernel(x, norm1_g, w_in, w_fourier, w_attn, q_norm_g, k_norm_g, lambda_q1, lambda_k1,
           lambda_q2, lambda_k2, subln_g, w_out, norm2_g, w_query, sub_keys, expert_u, expert_v):
    B, S, D = x.shape
    T = B * S
    depth = w_in.shape[0]
    dh = D // (2 * A_HEADS)
    f_width = w_fourier.shape[1]
    qk_width = A_HEADS * 2 * dh
    v_width = w_attn.shape[1]
    o_q = f_width
    o_k = o_q + qk_width
    o_v = o_k + qk_width
    o_g = o_v + v_width
    h = x.reshape(T, D)
    for i in range(depth):
        lam_init = 0.8 - 0.6 * math.exp(-0.3 * i)
        xn = rmsnorm_rows(h, norm1_g[i])
        w_in_i = w_in[i]
        qk_gain = jnp.concatenate([jnp.tile(q_norm_g[i].astype(F32) * dh ** -0.5, 2 * A_HEADS),
                                   jnp.tile(k_norm_g[i].astype(F32), 2 * A_HEADS)]).reshape(1, 2 * qk_width)
        z = project(xn, w_in_i, 0, f_width, "cast", out_dtype=F32)
        qk = project(xn, w_in_i, o_q, 2 * qk_width, "headnorm", gain=qk_gain, group=dh)
        vv = project(xn, w_in_i, o_v, v_width, "cast")
        gates = project(xn, w_in_i, o_g, 2 * D, "sigmoid")

        fm = fourier_mix(z, B, S)
        lamv = jnp.stack([lambda_q1[i], lambda_k1[i], lambda_q2[i], lambda_k2[i]]).astype(F32)
        bound = (jnp.max(jnp.abs(q_norm_g[i])) * jnp.max(jnp.abs(k_norm_g[i]))).astype(F32) * (1.01 * dh ** 0.5)
        attend = functools.partial(diff_attention, qk, vv, lamv, subln_g[i], bound, B, S, lam_init)
        on = lax.cond(bound <= MAX_SAFE_SHIFT, lambda: attend(True), lambda: attend(False))

        mixed = gated_merge(fm, on, gates, w_fourier[i].astype(BF16), w_attn[i].astype(BF16))
        h1, hn, hnt = out_projection(mixed, h, w_out[i].astype(BF16), norm2_g[i])

        n1, e1, r2, e2, ub, vb = peer_query(hn, w_query[i].astype(BF16), sub_keys[i].astype(BF16),
                                            expert_u[i], expert_v[i])
        h = peer_dense(hnt, ub, vb, n1, e1, r2, e2, h1)
    return h.reshape(B, S, D)
```
